```python
import math
import numpy as np
import jax
import jax.numpy as jnp
from jax import lax

D_MODEL = 1024
BATCH = 8
SEQ = 4096
DEPTH = 1

HEAD_DIM = 64
N_HEADS_NSA = 8
N_KV_NSA = 2
GROUP_NSA = N_HEADS_NSA // N_KV_NSA
N_HEADS_FOX = 8
MIX_WIDTH = (N_HEADS_NSA + N_HEADS_FOX) * HEAD_DIM
NSA_Q_W = N_HEADS_NSA * HEAD_DIM
NSA_KV_W = N_KV_NSA * HEAD_DIM
FOX_W = N_HEADS_FOX * HEAD_DIM
PROJ_SIZES = (NSA_Q_W, NSA_KV_W, NSA_KV_W, NSA_KV_W, NSA_KV_W, NSA_KV_W, NSA_KV_W,
              3 * N_HEADS_NSA, FOX_W, FOX_W, FOX_W, N_HEADS_FOX)
IN_WIDTH = sum(PROJ_SIZES)
D_FF = 2816
D_PLE = 256
ROPE_THETA = 500000.0
ROT_DIM = HEAD_DIM // 4
CMP_BLOCK = 32
CMP_STRIDE = 16
CMP_HIDDEN = 256
SEL_BLOCK = 64
SEL_TOPK = 16
WINDOW = 512
NSA_Q_BLOCK = 32
FOX_Q_BLOCK = 128
FORCED_SCORE = 1e6
EPS = 1e-6
NEG = -1e30

kernel_name = 'hybrid_nsa_fox_macaron_ple'


def rmsnorm(x, g):
    xf = x.astype(jnp.float32)
    y = xf * lax.rsqrt(jnp.mean(xf * xf, axis=-1, keepdims=True) + EPS)
    return (y * g.astype(jnp.float32)).astype(x.dtype)


def swiglu(x, wg, wu, wd):
    return (jax.nn.silu(x @ wg) * (x @ wu)) @ wd


def split_heads(t, n):
    b, s, _ = t.shape
    return t.reshape(b, s, n, HEAD_DIM).transpose(0, 2, 1, 3)


def merge_heads(t):
    b, n, s, d = t.shape
    return t.transpose(0, 2, 1, 3).reshape(b, s, n * d)


def rope_tables(seq):
    pos = jnp.arange(seq, dtype=jnp.float32)
    inv = ROPE_THETA ** (-jnp.arange(0, ROT_DIM, 2, dtype=jnp.float32) / ROT_DIM)
    ang = pos[:, None] * inv[None, :]
    return jnp.cos(ang), jnp.sin(ang)


def partial_rope(x, cos, sin):
    half = ROT_DIM // 2
    xf = x.astype(jnp.float32)
    x1, x2, rest = xf[..., :half], xf[..., half:ROT_DIM], xf[..., ROT_DIM:]
    out = jnp.concatenate([x1 * cos - x2 * sin, x2 * cos + x1 * sin, rest], axis=-1)
    return out.astype(x.dtype)


def masked_softmax(s, mask):
    return jax.nn.softmax(jnp.where(mask, s.astype(jnp.float32), NEG), axis=-1)


def compress_blocks(tok, pos_emb, w1, w2):
    s = tok.shape[2]
    n_cmp = (s - CMP_BLOCK) // CMP_STRIDE + 1
    idx = np.arange(n_cmp)[:, None] * CMP_STRIDE + np.arange(CMP_BLOCK)[None, :]
    blk = tok[:, :, idx, :] + pos_emb
    blk = blk.reshape(blk.shape[0], blk.shape[1], n_cmp, CMP_BLOCK * HEAD_DIM)
    return jax.nn.silu(blk @ w1) @ w2


def nsa_attention(q, k_cmp, v_cmp, k_slc, v_slc, k_win, v_win, gates):
    b, h, s, d = q.shape
    qg = q.reshape(b, N_KV_NSA, GROUP_NSA, s, d)
    gg_all = gates.reshape(b, N_KV_NSA, GROUP_NSA, s, 3)
    n_cmp = k_cmp.shape[2]
    n_sel = s // SEL_BLOCK
    top = min(SEL_TOPK, n_sel)
    cmp_start = np.arange(n_cmp) * CMP_STRIDE
    cmp_end = jnp.asarray(cmp_start + CMP_BLOCK - 1)
    sel_start = np.arange(n_sel) * SEL_BLOCK
    overlap = jnp.asarray(((cmp_start[:, None] <= sel_start[None, :] + SEL_BLOCK - 1)
                           & (cmp_start[:, None] + CMP_BLOCK - 1 >= sel_start[None, :])).astype(np.float32))
    ks_blocks = k_slc.reshape(b, N_KV_NSA, n_sel, SEL_BLOCK, d)
    vs_blocks = v_slc.reshape(b, N_KV_NSA, n_sel, SEL_BLOCK, d)
    pad = ((0, 0), (0, 0), (WINDOW, 0), (0, 0))
    kw_pad = jnp.pad(k_win, pad)
    vw_pad = jnp.pad(v_win, pad)
    bi = np.arange(b)[:, None, None, None]
    hi = np.arange(N_KV_NSA)[None, :, None, None]
    blk_j = jnp.arange(n_sel)
    scale = HEAD_DIM ** -0.5
    qn = NSA_Q_BLOCK

    def block(qb):
        t0 = qb * qn
        tpos = t0 + jnp.arange(qn)
        qq = lax.dynamic_slice_in_dim(qg, t0, qn, axis=3)
        gg = lax.dynamic_slice_in_dim(gg_all, t0, qn, axis=3)
        m_c = cmp_end[None, :] <= tpos[:, None]
        s_c = jnp.einsum('bkgqd,bknd->bkgqn', qq, k_cmp) * scale
        p_c = masked_softmax(s_c, m_c) * m_c
        o_c = jnp.einsum('bkgqn,bknd->bkgqd', p_c.astype(v_cmp.dtype), v_cmp)
        imp = jnp.einsum('bkgqn,nj->bkqj', p_c, overlap)
        cur = tpos // SEL_BLOCK
        valid = blk_j[None, :] <= cur[:, None]
        forced = (blk_j[None, :] == 0) | (blk_j[None, :] == cur[:, None]) | (blk_j[None, :] == cur[:, None] - 1)
        imp = jnp.where(valid, jnp.where(forced, FORCED_SCORE, imp), -1.0)
        vals, sel = lax.top_k(imp, top)
        kg = ks_blocks[bi, hi, sel].reshape(b, N_KV_NSA, qn, top * SEL_BLOCK, d)
        vg = vs_blocks[bi, hi, sel].reshape(b, N_KV_NSA, qn, top * SEL_BLOCK, d)
        kpos = (sel[..., None] * SEL_BLOCK + jnp.arange(SEL_BLOCK)).reshape(b, N_KV_NSA, qn, top * SEL_BLOCK)
        m_s = (kpos <= tpos[:, None]) & jnp.repeat(vals >= 0, SEL_BLOCK, axis=-1)
        s_s = jnp.einsum('bkgqd,bkqnd->bkgqn', qq, kg) * scale
        p_s = masked_softmax(s_s, m_s[:, :, None])
        o_s = jnp.einsum('bkgqn,bkqnd->bkgqd', p_s.astype(vg.dtype), vg)
        kw = lax.dynamic_slice_in_dim(kw_pad, t0, qn + WINDOW, axis=2)
        vw = lax.dynamic_slice_in_dim(vw_pad, t0, qn + WINDOW, axis=2)
        kpos_w = t0 - WINDOW + jnp.arange(qn + WINDOW)
        diff = tpos[:, None] - kpos_w[None, :]
        m_w = (kpos_w[None, :] >= 0) & (diff >= 0) & (diff < WINDOW)
        s_w = jnp.einsum('bkgqd,bknd->bkgqn', qq, kw) * scale
        p_w = masked_softmax(s_w, m_w)
        o_w = jnp.einsum('bkgqn,bknd->bkgqd', p_w.astype(vw.dtype), vw)
        return gg[..., 0:1] * o_c + gg[..., 1:2] * o_s + gg[..., 2:3] * o_w

    out = lax.map(block, jnp.arange(s // qn))
    return jnp.moveaxis(out, 0, 3).reshape(b, h, s, d)


def forgetting_attention(q, k, v, log_f):
    b, h, s, d = q.shape
    c = jnp.cumsum(log_f, axis=-1)
    kpos = jnp.arange(s)
    scale = HEAD_DIM ** -0.5
    qn = FOX_Q_BLOCK

    def block(qb):
        t0 = qb * qn
        tpos = t0 + jnp.arange(qn)
        qq = lax.dynamic_slice_in_dim(q, t0, qn, axis=2)
        cq = lax.dynamic_slice_in_dim(c, t0, qn, axis=2)
        logits = (jnp.einsum('bhqd,bhsd->bhqs', qq, k).astype(jnp.float32) * scale
                  + cq[..., :, None] - c[:, :, None, :])
        p = masked_softmax(logits, kpos[None, :] <= tpos[:, None])
        return jnp.einsum('bhqs,bhsd->bhqd', p.astype(v.dtype), v)

    out = lax.map(block, jnp.arange(s // qn))
    return jnp.moveaxis(out, 0, 2).reshape(b, h, s, d)


def setup_inputs(seed: int = 0) -> dict:
    key = jax.random.key(seed)
    ks = iter(jax.random.split(key, 40))
    nrm = lambda shape, sc: jax.random.normal(next(ks), shape, jnp.float32) * sc
    gain = lambda n: 1.0 + nrm((DEPTH, n), 0.05)
    L = DEPTH
    return {
        'x': nrm((BATCH, SEQ, D_MODEL), 1.0),
        'p': nrm((DEPTH, BATCH, SEQ, D_PLE), 1.0),
        'ffn1_norm': gain(D_MODEL),
        'ffn1_wg': nrm((L, D_MODEL, D_FF), D_MODEL ** -0.5),
        'ffn1_wu': nrm((L, D_MODEL, D_FF), D_MODEL ** -0.5),
        'ffn1_wd': nrm((L, D_FF, D_MODEL), D_FF ** -0.5),
        'mix_norm': gain(D_MODEL),
        'w_in': nrm((L, D_MODEL, IN_WIDTH), D_MODEL ** -0.5),
        'b_forget': 4.0 + nrm((L, N_HEADS_FOX), 0.5),
        'q_norm_nsa': gain(HEAD_DIM),
        'k_norm_cmp': gain(HEAD_DIM),
        'k_norm_slc': gain(HEAD_DIM),
        'k_norm_win': gain(HEAD_DIM),
        'cmp_pos_k': nrm((L, CMP_BLOCK, HEAD_DIM), 0.02),
        'cmp_pos_v': nrm((L, CMP_BLOCK, HEAD_DIM), 0.02),
        'cmp_k_w1': nrm((L, CMP_BLOCK * HEAD_DIM, CMP_HIDDEN), (CMP_BLOCK * HEAD_DIM) ** -0.5),
        'cmp_k_w2': nrm((L, CMP_HIDDEN, HEAD_DIM), CMP_HIDDEN ** -0.5),
        'cmp_v_w1': nrm((L, CMP_BLOCK * HEAD_DIM, CMP_HIDDEN), (CMP_BLOCK * HEAD_DIM) ** -0.5),
        'cmp_v_w2': nrm((L, CMP_HIDDEN, HEAD_DIM), CMP_HIDDEN ** -0.5),
        'q_norm_fox': gain(HEAD_DIM),
        'k_norm_fox': gain(HEAD_DIM),
        'out_norm_nsa': gain(NSA_Q_W),
        'out_norm_fox': gain(FOX_W),
        'w_out': nrm((L, MIX_WIDTH, D_MODEL), MIX_WIDTH ** -0.5),
        'ffn2_norm': gain(D_MODEL),
        'ffn2_wg': nrm((L, D_MODEL, D_FF), D_MODEL ** -0.5),
        'ffn2_wu': nrm((L, D_MODEL, D_FF), D_MODEL ** -0.5),
        'ffn2_wd': nrm((L, D_FF, D_MODEL), D_FF ** -0.5),
        'ple_gate_norm': gain(D_MODEL),
        'ple_w_gate': nrm((L, D_MODEL, D_MODEL), D_MODEL ** -0.5),
        'ple_w_proj': nrm((L, D_PLE, D_MODEL), D_PLE ** -0.5),
        'ple_norm': gain(D_MODEL),
    }


def reference(x, p, ffn1_norm, ffn1_wg, ffn1_wu, ffn1_wd, mix_norm, w_in, b_forget,
              q_norm_nsa, k_norm_cmp, k_norm_slc, k_norm_win, cmp_pos_k, cmp_pos_v,
              cmp_k_w1, cmp_k_w2, cmp_v_w1, cmp_v_w2, q_norm_fox, k_norm_fox,
              out_norm_nsa, out_norm_fox, w_out, ffn2_norm, ffn2_wg, ffn2_wu, ffn2_wd,
              ple_gate_norm, ple_w_gate, ple_w_proj, ple_norm):
    b, s, _ = x.shape
    cos, sin = rope_tables(s)
    split_points = [int(v) for v in np.cumsum(PROJ_SIZES)[:-1]]
    h = x
    for i in range(DEPTH):
        h = h + 0.5 * swiglu(rmsnorm(h, ffn1_norm[i]), ffn1_wg[i], ffn1_wu[i], ffn1_wd[i])
        a = rmsnorm(h, mix_norm[i])
        u = a @ w_in[i]
        (qa, kc, vc, ksl, vsl, kwn, vwn, ga, qf, kf, vf, fl) = jnp.split(u, split_points, axis=-1)
        q_a = partial_rope(rmsnorm(split_heads(qa, N_HEADS_NSA), q_norm_nsa[i]), cos, sin)
        kc_tok = partial_rope(split_heads(kc, N_KV_NSA), cos, sin)
        k_cmp = rmsnorm(compress_blocks(kc_tok, cmp_pos_k[i], cmp_k_w1[i], cmp_k_w2[i]), k_norm_cmp[i])
        v_cmp = compress_blocks(split_heads(vc, N_KV_NSA), cmp_pos_v[i], cmp_v_w1[i], cmp_v_w2[i])
        k_slc = partial_rope(rmsnorm(split_heads(ksl, N_KV_NSA), k_norm_slc[i]), cos, sin)
        k_win = partial_rope(rmsnorm(split_heads(kwn, N_KV_NSA), k_norm_win[i]), cos, sin)
        gates = jax.nn.sigmoid(ga).reshape(b, s, N_HEADS_NSA, 3).transpose(0, 2, 1, 3)
        o_a = nsa_attention(q_a, k_cmp, v_cmp, k_slc, split_heads(vsl, N_KV_NSA),
                            k_win, split_heads(vwn, N_KV_NSA), gates)
        q_b = rmsnorm(split_heads(qf, N_HEADS_FOX), q_norm_fox[i])
        k_b = rmsnorm(split_heads(kf, N_HEADS_FOX), k_norm_fox[i])
        log_f = jax.nn.log_sigmoid((fl + b_forget[i]).astype(jnp.float32)).transpose(0, 2, 1)
        o_b = forgetting_attention(q_b, k_b, split_heads(vf, N_HEADS_FOX), log_f)
        mixed = jnp.concatenate([rmsnorm(merge_heads(o_a), out_norm_nsa[i]),
                                 rmsnorm(merge_heads(o_b), out_norm_fox[i])], axis=-1)
        h = h + mixed @ w_out[i]
        h = h + 0.5 * swiglu(rmsnorm(h, ffn2_norm[i]), ffn2_wg[i], ffn2_wu[i], ffn2_wd[i])
        gate = jax.nn.sigmoid(rmsnorm(h, ple_gate_norm[i]) @ ple_w_gate[i])
        e = rmsnorm(p[i] @ ple_w_proj[i], ple_norm[i])
        h = h + gate * e
    return h
```

```python
import functools

import numpy as np
import jax
import jax.numpy as jnp
from jax import lax
from jax.experimental import pallas as pl
from jax.experimental.pallas import tpu as pltpu

F32 = jnp.float32
BF16 = jnp.bfloat16

D_MODEL = 1024
HEAD_DIM = 64
N_HEADS_NSA = 8
N_KV_NSA = 2
GROUP_NSA = N_HEADS_NSA // N_KV_NSA
N_HEADS_FOX = 8
NSA_Q_W = N_HEADS_NSA * HEAD_DIM
NSA_KV_W = N_KV_NSA * HEAD_DIM
FOX_W = N_HEADS_FOX * HEAD_DIM
PROJ_SIZES = (NSA_Q_W, NSA_KV_W, NSA_KV_W, NSA_KV_W, NSA_KV_W, NSA_KV_W, NSA_KV_W,
              3 * N_HEADS_NSA, FOX_W, FOX_W, FOX_W, N_HEADS_FOX)
D_FF = 2816
D_PLE = 256
ROPE_THETA = 500000.0
ROT_DIM = HEAD_DIM // 4
CMP_BLOCK = 32
CMP_STRIDE = 16
CMP_HIDDEN = 256
SEL_BLOCK = 64
SEL_TOPK = 16
WINDOW = 512
FORCED_SCORE = 1e6
EPS = 1e-6
NEG = -1e30

LANES = 128
VMEM_LIMIT = 56 * 1024 * 1024

C_QA = 0
C_KC = C_QA + N_HEADS_NSA * LANES
C_VC = C_KC + LANES
C_KSL = C_VC + LANES
C_VSL = C_KSL + N_KV_NSA * LANES
C_KWN = C_VSL + N_KV_NSA * LANES
C_VWN = C_KWN + N_KV_NSA * LANES
C_GATE = C_VWN + N_KV_NSA * LANES
C_FQ = C_GATE + N_KV_NSA * LANES
C_FK = C_FQ + FOX_W
C_FV = C_FK + FOX_W
C_FL = C_FV + FOX_W
C_END = C_FL + LANES

FFN_TM = 1024
FFN_TF = 256
PROJ_TS = 512
NSA_TQ = 256
FOX_TQ = 256
OUT_TM = 512
PLE_TM = 512


def _dot(a, b):
    return jnp.dot(a, b, preferred_element_type=F32)


def _dot_nt(a, b):
    return lax.dot_general(a, b, (((1,), (1,)), ((), ())), preferred_element_type=F32)


def _params(*sem):
    return pltpu.CompilerParams(dimension_semantics=sem, vmem_limit_bytes=VMEM_LIMIT)


def _rms(x):
    return lax.rsqrt(jnp.mean(x * x, axis=-1, keepdims=True) + EPS)


def _ffn_kernel(x_ref, g_ref, wg_ref, wu_ref, wd_ref, o_ref, xn_ref, acc_ref):
    f = pl.program_id(1)

    @pl.when(f == 0)
    def _():
        x = x_ref[...]
        xn_ref[...] = (x * _rms(x) * g_ref[...]).astype(BF16)
        acc_ref[...] = jnp.zeros_like(acc_ref)

    xn = xn_ref[...]
    gate = _dot(xn, wg_ref[...])
    up = _dot(xn, wu_ref[...])
    act = (gate * jax.nn.sigmoid(gate) * up).astype(BF16)
    acc_ref[...] += _dot(act, wd_ref[...])

    @pl.when(f == pl.num_programs(1) - 1)
    def _():
        o_ref[...] = x_ref[...] + 0.5 * acc_ref[...]


def _ffn(x, g, wg, wu, wd):
    t, d = x.shape
    f = wg.shape[1]
    tm, tf = min(FFN_TM, t), FFN_TF
    return pl.pallas_call(
        _ffn_kernel,
        grid=(t // tm, f // tf),
        in_specs=[
            pl.BlockSpec((tm, d), lambda i, j: (i, 0)),
            pl.BlockSpec((1, d), lambda i, j: (0, 0)),
            pl.BlockSpec((d, tf), lambda i, j: (0, j)),
            pl.BlockSpec((d, tf), lambda i, j: (0, j)),
            pl.BlockSpec((tf, d), lambda i, j: (j, 0)),
        ],
        out_specs=pl.BlockSpec((tm, d), lambda i, j: (i, 0)),
        out_shape=jax.ShapeDtypeStruct((t, d), F32),
        scratch_shapes=[pltpu.VMEM((tm, d), BF16), pltpu.VMEM((tm, d), F32)],
        compiler_params=_params("parallel", "arbitrary"),
        name="ffn",
    )(x, g, wg, wu, wd)


def _log_sigmoid(x):
    return -(jnp.maximum(-x, 0.0) + jnp.log(1.0 + jnp.exp(-jnp.abs(x))))


def _split3(x):
    hi = x.astype(BF16)
    r1 = x - hi.astype(F32)
    mid = r1.astype(BF16)
    lo = (r1 - mid.astype(F32)).astype(BF16)
    return hi, mid, lo


def _inproj_kernel(h_ref, g_ref, w_ref, bf_ref, gains_ref, rope_ref,
                   qa_ref, kc_ref, vc_ref, ksl_ref, vsl_ref, kwn_ref, vwn_ref,
                   gate_ref, fq_ref, fk_ref, fv_ref, cum_ref, carry_ref):
    si = pl.program_id(1)
    ts = h_ref.shape[1]
    x = h_ref[0]
    a = (x * _rms(x) * g_ref[...]).astype(BF16)

    row = lax.broadcasted_iota(jnp.int32, (LANES, LANES), 0)
    col = lax.broadcasted_iota(jnp.int32, (LANES, LANES), 1)
    g_pair = jnp.where((row < HEAD_DIM) == (col < HEAD_DIM), 1.0, 0.0).astype(BF16)
    g_all = jnp.ones((LANES, LANES), BF16)
    cos_t, sin_lo, sin_hi = rope_ref[0], rope_ref[1], rope_ref[2]

    def proj(c0):
        return _dot(a, w_ref[:, c0:c0 + LANES])

    def norm(u, ones_mat, gain_row):
        u2 = u * u
        hi = u2.astype(BF16)
        lo = (u2 - hi.astype(F32)).astype(BF16)
        ms = (_dot(hi, ones_mat) + _dot(lo, ones_mat)) * (1.0 / HEAD_DIM)
        return u * lax.rsqrt(ms + EPS) * gains_ref[gain_row:gain_row + 1, :]

    def rope(u):
        return (u * cos_t + pltpu.roll(u, LANES - ROT_DIM // 2, 1) * sin_lo
                + pltpu.roll(u, ROT_DIM // 2, 1) * sin_hi)

    scale = HEAD_DIM ** -0.5
    for h in range(N_HEADS_NSA):
        y = rope(norm(proj(C_QA + h * LANES), g_all, 0)) * scale
        qa_ref[0, :, h * LANES:(h + 1) * LANES] = y.astype(BF16)

    kc_ref[0] = rope(proj(C_KC)).astype(BF16)
    vc_ref[0] = proj(C_VC).astype(BF16)

    pos = si * ts + lax.broadcasted_iota(jnp.int32, (ts, LANES), 0)
    lane = lax.broadcasted_iota(jnp.int32, (ts, LANES), 1)
    sel_onehot = jnp.where(lane == HEAD_DIM + pos // SEL_BLOCK, 1.0, 0.0)
    for kh in range(N_KV_NSA):
        sl = slice(kh * LANES, (kh + 1) * LANES)
        y = rope(norm(proj(C_KSL + kh * LANES), g_all, 1)) + sel_onehot
        ksl_ref[0, :, sl] = y.astype(BF16)
        vsl_ref[0, :, sl] = proj(C_VSL + kh * LANES).astype(BF16)
        y = rope(norm(proj(C_KWN + kh * LANES), g_all, 2))
        kwn_ref[0, :, sl] = y.astype(BF16)
        vwn_ref[0, :, sl] = proj(C_VWN + kh * LANES).astype(BF16)
        gate_ref[0, :, sl] = jax.nn.sigmoid(proj(C_GATE + kh * LANES))

    for c in range(FOX_W // LANES):
        sl = slice(c * LANES, (c + 1) * LANES)
        fq_ref[0, :, sl] = (norm(proj(C_FQ + c * LANES), g_pair, 3) * scale).astype(BF16)
        fk_ref[0, :, sl] = norm(proj(C_FK + c * LANES), g_pair, 4).astype(BF16)
        fv_ref[0, :, sl] = proj(C_FV + c * LANES).astype(BF16)

    @pl.when(si == 0)
    def _():
        carry_ref[...] = jnp.zeros_like(carry_ref)

    lf = _log_sigmoid(proj(C_FL) + bf_ref[...])
    r_i = lax.broadcasted_iota(jnp.int32, (ts, ts), 0)
    c_i = lax.broadcasted_iota(jnp.int32, (ts, ts), 1)
    tri = jnp.where(r_i >= c_i, 1.0, 0.0).astype(BF16)
    hi, mid, lo = _split3(lf)
    c = _dot(tri, hi) + _dot(tri, mid) + _dot(tri, lo) + carry_ref[0:1, :]
    cum_ref[0] = c
    carry_ref[...] = jnp.broadcast_to(c[ts - 1:ts, :], carry_ref.shape)


def _inproj(h, g, w_all, bf_row, gains, rope_tab):
    b, s, d = h.shape
    ts = min(PROJ_TS, s)
    tok = lambda c: pl.BlockSpec((1, ts, c), lambda i, j: (i, j, 0))
    const = lambda shape: pl.BlockSpec(shape, lambda i, j: (0,) * len(shape))
    widths = [N_HEADS_NSA * LANES, LANES, LANES] + [N_KV_NSA * LANES] * 4
    out_shape = [jax.ShapeDtypeStruct((b, s, c), BF16) for c in widths]
    out_shape.append(jax.ShapeDtypeStruct((b, s, N_KV_NSA * LANES), F32))
    out_shape += [jax.ShapeDtypeStruct((b, s, FOX_W), BF16)] * 3
    out_shape.append(jax.ShapeDtypeStruct((b, s, LANES), F32))
    out_specs = [tok(sh.shape[-1]) for sh in out_shape]
    return pl.pallas_call(
        _inproj_kernel,
        grid=(b, s // ts),
        in_specs=[
            tok(d),
            const((1, d)),
            const((d, C_END)),
            const((1, LANES)),
            const((8, LANES)),
            pl.BlockSpec((3, ts, LANES), lambda i, j: (0, j, 0)),
        ],
        out_specs=out_specs,
        out_shape=out_shape,
        scratch_shapes=[pltpu.VMEM((8, LANES), F32)],
        compiler_params=_params("parallel", "arbitrary"),
        name="inproj",
    )(h, g, w_all, bf_row, gains, rope_tab)


def _compress_kernel(r0k_ref, r1k_ref, r0v_ref, r1v_ref, w1k_ref, w1v_ref,
                     pk_ref, pv_ref, w2k_ref, w2v_ref, gain_ref, kc_ref, vc_ref):
    def mlp(r0_ref, r1_ref, w1_ref, p_ref, w2_ref, kh):
        top, bot = w1_ref[kh, 0], w1_ref[kh, 1]
        bias = _dot(p_ref[0], top) + _dot(p_ref[1], bot)
        hid = _dot(r0_ref[0], top) + _dot(r1_ref[0], bot) + bias[0:1, :]
        hid = hid * jax.nn.sigmoid(hid)
        return _dot(hid.astype(BF16), w2_ref[...])

    for kh in range(N_KV_NSA):
        kc = mlp(r0k_ref, r1k_ref, w1k_ref, pk_ref, w2k_ref, kh)
        ms = jnp.sum(kc * kc, axis=-1, keepdims=True) * (1.0 / HEAD_DIM)
        kc_ref[0, kh] = (kc * lax.rsqrt(ms + EPS) * gain_ref[...]).astype(BF16)
        vc_ref[0, kh] = mlp(r0v_ref, r1v_ref, w1v_ref, pv_ref, w2v_ref, kh).astype(BF16)


def _compress(r0k, r1k, r0v, r1v, w1k, w1v, pk, pv, w2k, w2v, gain):
    b, r, w = r0k.shape
    rows = pl.BlockSpec((1, r, w), lambda i: (i, 0, 0))
    const = lambda a: pl.BlockSpec(a.shape, lambda i: (0,) * a.ndim)
    out = jax.ShapeDtypeStruct((b, N_KV_NSA, r, LANES), BF16)
    out_spec = pl.BlockSpec((1, N_KV_NSA, r, LANES), lambda i: (i, 0, 0, 0))
    return pl.pallas_call(
        _compress_kernel,
        grid=(b,),
        in_specs=[rows, rows, rows, rows, const(w1k), const(w1v), const(pk), const(pv),
                  const(w2k), const(w2v), const(gain)],
        out_specs=[out_spec, out_spec],
        out_shape=[out, out],
        compiler_params=_params("parallel"),
        name="compress",
    )(r0k, r1k, r0v, r1v, w1k, w1v, pk, pv, w2k, w2v, gain)


def _nsa_kernel(q_ref, kc_ref, vc_ref, ksl_ref, vsl_ref, kwn_ref, vwn_ref, gate_ref, ov_ref,
                o_ref, qaug_ref, m_ref, l_ref, acc_ref):
    qi = pl.program_id(2)
    tq = q_ref.shape[1]
    s_len = ksl_ref.shape[1]
    n_cmp_rows = kc_ref.shape[2]
    n_sel = ov_ref.shape[0]
    top = min(SEL_TOPK, n_sel)
    t0 = qi * tq
    grp = GROUP_NSA

    def rep(x):
        return jnp.concatenate([x] * grp, axis=0)

    q_all = jnp.concatenate([q_ref[0, :, g * LANES:(g + 1) * LANES] for g in range(grp)], axis=0)

    def softmax_rows(s, mask):
        s = jnp.where(mask, s, NEG)
        mx = jnp.max(s, axis=-1, keepdims=True)
        e = jnp.where(mask, jnp.exp(s - mx), 0.0)
        den = jnp.sum(e, axis=-1, keepdims=True)
        return e / jnp.where(den > 0.0, den, 1.0)

    t_c = t0 + lax.broadcasted_iota(jnp.int32, (tq, n_cmp_rows), 0)
    n_c = lax.broadcasted_iota(jnp.int32, (tq, n_cmp_rows), 1)
    m_c = rep(n_c * CMP_STRIDE + (CMP_BLOCK - 1) <= t_c)
    p_c = softmax_rows(_dot_nt(q_all, kc_ref[0, 0]), m_c)
    o_c = _dot(p_c.astype(BF16), vc_ref[0, 0])

    p_sum = p_c[0:tq]
    for g in range(1, grp):
        p_sum = p_sum + p_c[g * tq:(g + 1) * tq]
    p_hi = p_sum.astype(BF16)
    p_lo = (p_sum - p_hi.astype(F32)).astype(BF16)
    ov = ov_ref[...]
    imp = _dot_nt(ov, p_hi) + _dot_nt(ov, p_lo)
    j_blk = lax.broadcasted_iota(jnp.int32, (n_sel, tq), 0)
    cur = (t0 + lax.broadcasted_iota(jnp.int32, (n_sel, tq), 1)) // SEL_BLOCK
    forced = (j_blk == 0) | (j_blk == cur) | (j_blk == cur - 1)
    imp = jnp.where(j_blk <= cur, jnp.where(forced, FORCED_SCORE, imp), -1.0)
    rank = jnp.zeros((n_sel, tq), jnp.int32)
    for i in range(n_sel):
        r_i = imp[i:i + 1, :]
        beats = (r_i > imp) | ((r_i == imp) & (j_blk > i))
        rank = rank + jnp.where(beats, 1, 0)
    chosen = (rank < top) & (imp >= 0.0)
    bias_t = jnp.where(chosen, 0.0, NEG)
    pieces = [jnp.zeros((HEAD_DIM, tq), F32), bias_t]
    if n_sel < HEAD_DIM:
        pieces.append(jnp.zeros((HEAD_DIM - n_sel, tq), F32))
    bias_x = jnp.transpose(jnp.concatenate(pieces, axis=0)).astype(BF16)
    qaug_ref[...] = q_all + rep(bias_x)

    m_ref[...] = jnp.full_like(m_ref, NEG)
    l_ref[...] = jnp.zeros_like(l_ref)
    acc_ref[...] = jnp.zeros_like(acc_ref)

    def slc_tile(k0, diagonal):
        kt = ksl_ref[0, pl.ds(k0, tq), :]
        vt = vsl_ref[0, pl.ds(k0, tq), :]
        s = _dot_nt(qaug_ref[...], kt)
        if diagonal:
            r_t = lax.broadcasted_iota(jnp.int32, (tq, tq), 0)
            c_t = lax.broadcasted_iota(jnp.int32, (tq, tq), 1)
            s = jnp.where(rep(c_t <= r_t), s, NEG)
        m_old = m_ref[...]
        m_new = jnp.maximum(m_old, jnp.max(s, axis=-1, keepdims=True))
        alpha = jnp.exp(m_old - m_new)
        p = jnp.exp(s - m_new)
        l_ref[...] = alpha * l_ref[...] + jnp.sum(p, axis=-1, keepdims=True)
        acc_ref[...] = alpha * acc_ref[...] + _dot(p.astype(BF16), vt)
        m_ref[...] = m_new

    def body(j, carry):
        slc_tile(pl.multiple_of(j * tq, tq), False)
        return carry

    lax.fori_loop(0, qi, body, 0)
    slc_tile(pl.multiple_of(t0, tq), True)
    o_s = acc_ref[...] / l_ref[...]

    span = min(WINDOW + tq, s_len)
    start = pl.multiple_of(jnp.maximum(t0 + tq - span, 0), tq)
    t_w = t0 + lax.broadcasted_iota(jnp.int32, (tq, span), 0)
    k_w = start + lax.broadcasted_iota(jnp.int32, (tq, span), 1)
    m_w = rep((k_w <= t_w) & (t_w - k_w < WINDOW))
    p_w = softmax_rows(_dot_nt(qaug_ref[...], kwn_ref[0, pl.ds(start, span), :]), m_w)
    o_w = _dot(p_w.astype(BF16), vwn_ref[0, pl.ds(start, span), :])

    gates = gate_ref[0]
    outs = []
    for g in range(grp):
        rows = slice(g * tq, (g + 1) * tq)
        outs.append(gates[:, 3 * g:3 * g + 1] * o_c[rows]
                    + gates[:, 3 * g + 1:3 * g + 2] * o_s[rows]
                    + gates[:, 3 * g + 2:3 * g + 3] * o_w[rows])
    for c in range(grp // 2):
        o_ref[0, :, c * LANES:(c + 1) * LANES] = outs[2 * c] + pltpu.roll(outs[2 * c + 1], HEAD_DIM, 1)


def _nsa(qa, kcx, vcx, kslx, vslx, kwnx, vwnx, gates, ov_t):
    b, s, _ = qa.shape
    tq = min(NSA_TQ, s)
    r = kcx.shape[2]
    gw = GROUP_NSA * LANES
    seq = pl.BlockSpec((1, s, LANES), lambda i, k, j: (i, 0, k))
    cmp_spec = pl.BlockSpec((1, 1, r, LANES), lambda i, k, j: (i, k, 0, 0))
    return pl.pallas_call(
        _nsa_kernel,
        grid=(b, N_KV_NSA, s // tq),
        in_specs=[
            pl.BlockSpec((1, tq, gw), lambda i, k, j: (i, j, k)),
            cmp_spec, cmp_spec, seq, seq, seq, seq,
            pl.BlockSpec((1, tq, LANES), lambda i, k, j: (i, j, k)),
            pl.BlockSpec(ov_t.shape, lambda i, k, j: (0, 0)),
        ],
        out_specs=pl.BlockSpec((1, tq, gw // 2), lambda i, k, j: (i, j, k)),
        out_shape=jax.ShapeDtypeStruct((b, s, NSA_Q_W), F32),
        scratch_shapes=[
            pltpu.VMEM((GROUP_NSA * tq, LANES), BF16),
            pltpu.VMEM((GROUP_NSA * tq, 1), F32),
            pltpu.VMEM((GROUP_NSA * tq, 1), F32),
            pltpu.VMEM((GROUP_NSA * tq, LANES), F32),
        ],
        compiler_params=_params("parallel", "parallel", "arbitrary"),
        name="nsa",
    )(qa, kcx, vcx, kslx, vslx, kwnx, vwnx, gates, ov_t)


def _fox_kernel(q_ref, k_ref, v_ref, cq_ref, ck_ref, o_ref, m_ref, l_ref, acc_ref):
    pair = pl.program_id(1)
    qi = pl.program_id(2)
    tq = q_ref.shape[1]
    t0 = qi * tq
    q = q_ref[0]
    lane = lax.broadcasted_iota(jnp.int32, (tq, LANES), 1)
    low = lane < HEAD_DIM
    zero = jnp.zeros_like(q)
    q_e = (jnp.where(low, q, zero), jnp.where(low, zero, q))
    cq_blk = cq_ref[0]
    cq_e = tuple(jnp.sum(jnp.where(lane == 2 * pair + e, cq_blk, 0.0), axis=-1, keepdims=True)
                 for e in range(2))

    m_ref[...] = jnp.full_like(m_ref, NEG)
    l_ref[...] = jnp.zeros_like(l_ref)
    acc_ref[...] = jnp.zeros_like(acc_ref)

    def tile(k0, diagonal):
        kt = k_ref[0, pl.ds(k0, tq), :]
        vt = v_ref[0, pl.ds(k0, tq), :]
        if diagonal:
            r_t = lax.broadcasted_iota(jnp.int32, (tq, tq), 0)
            c_t = lax.broadcasted_iota(jnp.int32, (tq, tq), 1)
            causal = c_t <= r_t
        for e in range(2):
            s = _dot_nt(q_e[e], kt) + cq_e[e] - ck_ref[0, 0, e:e + 1, pl.ds(k0, tq)]
            if diagonal:
                s = jnp.where(causal, s, NEG)
            m_old = m_ref[e]
            m_new = jnp.maximum(m_old, jnp.max(s, axis=-1, keepdims=True))
            alpha = jnp.exp(m_old - m_new)
            p = jnp.exp(s - m_new)
            l_ref[e] = alpha * l_ref[e] + jnp.sum(p, axis=-1, keepdims=True)
            acc_ref[e] = alpha * acc_ref[e] + _dot(p.astype(BF16), vt)
            m_ref[e] = m_new

    def body(j, carry):
        tile(pl.multiple_of(j * tq, tq), False)
        return carry

    lax.fori_loop(0, qi, body, 0)
    tile(pl.multiple_of(t0, tq), True)
    o_ref[0] = jnp.where(low, acc_ref[0] / l_ref[0], acc_ref[1] / l_ref[1])


def _fox(fq, fk, fv, cum, crow):
    b, s, _ = fq.shape
    tq = min(FOX_TQ, s)
    seq = pl.BlockSpec((1, s, LANES), lambda i, p, j: (i, 0, p))
    return pl.pallas_call(
        _fox_kernel,
        grid=(b, N_HEADS_FOX // 2, s // tq),
        in_specs=[
            pl.BlockSpec((1, tq, LANES), lambda i, p, j: (i, j, p)),
            seq, seq,
            pl.BlockSpec((1, tq, LANES), lambda i, p, j: (i, j, 0)),
            pl.BlockSpec((1, 1, 2, s), lambda i, p, j: (i, p, 0, 0)),
        ],
        out_specs=pl.BlockSpec((1, tq, LANES), lambda i, p, j: (i, j, p)),
        out_shape=jax.ShapeDtypeStruct((b, s, FOX_W), F32),
        scratch_shapes=[
            pltpu.VMEM((2, tq, 1), F32),
            pltpu.VMEM((2, tq, 1), F32),
            pltpu.VMEM((2, tq, LANES), F32),
        ],
        compiler_params=_params("parallel", "parallel", "arbitrary"),
        name="fox",
    )(fq, fk, fv, cum, crow)


def _outproj_kernel(h_ref, oa_ref, ob_ref, ga_ref, gb_ref, wa_ref, wb_ref, o_ref):
    oa = oa_ref[...]
    ob = ob_ref[...]
    na = (oa * _rms(oa) * ga_ref[...]).astype(BF16)
    nb = (ob * _rms(ob) * gb_ref[...]).astype(BF16)
    o_ref[...] = h_ref[...] + _dot(na, wa_ref[...]) + _dot(nb, wb_ref[...])


def _outproj(h, oa, ob, ga, gb, wa, wb):
    t, d = h.shape
    tm = min(OUT_TM, t)
    rows = lambda c: pl.BlockSpec((tm, c), lambda i: (i, 0))
    const = lambda a: pl.BlockSpec(a.shape, lambda i: (0,) * a.ndim)
    return pl.pallas_call(
        _outproj_kernel,
        grid=(t // tm,),
        in_specs=[rows(d), rows(NSA_Q_W), rows(FOX_W), const(ga), const(gb), const(wa), const(wb)],
        out_specs=rows(d),
        out_shape=jax.ShapeDtypeStruct((t, d), F32),
        compiler_params=_params("parallel"),
        name="outproj",
    )(h, oa, ob, ga, gb, wa, wb)


def _ple_kernel(h_ref, p_ref, gg_ref, gp_ref, wg_ref, wp_ref, o_ref):
    h = h_ref[...]
    hn = (h * _rms(h) * gg_ref[...]).astype(BF16)
    gate = jax.nn.sigmoid(_dot(hn, wg_ref[...]))
    e = _dot(p_ref[...].astype(BF16), wp_ref[...])
    o_ref[...] = h + gate * (e * _rms(e) * gp_ref[...])


def _ple(h, p, gg, gp, wg, wp):
    t, d = h.shape
    tm = min(PLE_TM, t)
    rows = lambda c: pl.BlockSpec((tm, c), lambda i: (i, 0))
    const = lambda a: pl.BlockSpec(a.shape, lambda i: (0,) * a.ndim)
    return pl.pallas_call(
        _ple_kernel,
        grid=(t // tm,),
        in_specs=[rows(d), rows(p.shape[1]), const(gg), const(gp), const(wg), const(wp)],
        out_specs=rows(d),
        out_shape=jax.ShapeDtypeStruct((t, d), F32),
        compiler_params=_params("parallel"),
        name="ple",
    )(h, p, gg, gp, wg, wp)


def _expand_heads(w, n):
    w = w.reshape(w.shape[0], n, HEAD_DIM)
    return jnp.concatenate([w, jnp.zeros_like(w)], axis=-1).reshape(w.shape[0], n * LANES)


def _pad_cols(w, width):
    return jnp.pad(w, ((0, 0), (0, width - w.shape[1])))


def _layout_w_in(w_in):
    splits = [int(v) for v in np.cumsum(PROJ_SIZES)[:-1]]
    qa, kc, vc, ksl, vsl, kwn, vwn, ga, qf, kf, vf, fl = jnp.split(w_in, splits, axis=-1)
    per_group = 3 * GROUP_NSA
    ga_x = jnp.concatenate([_pad_cols(ga[:, k * per_group:(k + 1) * per_group], LANES)
                            for k in range(N_KV_NSA)], axis=-1)
    cols = [_expand_heads(qa, N_HEADS_NSA), kc, vc,
            _expand_heads(ksl, N_KV_NSA), _expand_heads(vsl, N_KV_NSA),
            _expand_heads(kwn, N_KV_NSA), _expand_heads(vwn, N_KV_NSA),
            ga_x, qf, kf, vf, _pad_cols(fl, LANES)]
    return jnp.concatenate(cols, axis=-1).astype(BF16)


def _rope_tables(seq):
    pos = jnp.arange(seq, dtype=F32)
    inv = ROPE_THETA ** (-jnp.arange(0, ROT_DIM, 2, dtype=F32) / ROT_DIM)
    ang = pos[:, None] * inv[None, :]
    cos, sin = jnp.cos(ang), jnp.sin(ang)
    half = ROT_DIM // 2
    rest = HEAD_DIM - ROT_DIM
    ones = jnp.ones((seq, rest), F32)
    zeros = jnp.zeros((seq, rest), F32)
    zh = jnp.zeros((seq, half), F32)
    cos_t = jnp.concatenate([cos, cos, ones], axis=-1)
    sin_lo = jnp.concatenate([-sin, zh, zeros], axis=-1)
    sin_hi = jnp.concatenate([zh, sin, zeros], axis=-1)
    tile2 = lambda t: jnp.concatenate([t, t], axis=-1)
    return jnp.stack([tile2(cos_t), tile2(sin_lo), tile2(sin_hi)])


def _layout_cmp_w1(w1):
    hid = w1.shape[1]
    w = w1.reshape(2, CMP_STRIDE, HEAD_DIM, hid)
    z = jnp.zeros_like(w)
    per_head = [jnp.concatenate([w, z], axis=2), jnp.concatenate([z, w], axis=2)]
    return jnp.stack(per_head).reshape(N_KV_NSA, 2, CMP_STRIDE * LANES, hid).astype(BF16)


def _layout_cmp_pos(pos):
    p = pos.reshape(2, CMP_STRIDE, HEAD_DIM)
    p = jnp.concatenate([p, p], axis=-1).reshape(2, 1, CMP_STRIDE * LANES)
    return jnp.broadcast_to(p, (2, 8, CMP_STRIDE * LANES)).astype(BF16)


def _overlap_t(seq):
    rows = seq // CMP_STRIDE
    n_sel = seq // SEL_BLOCK
    cmp_start = np.arange(rows) * CMP_STRIDE
    sel_start = np.arange(n_sel) * SEL_BLOCK
    ov = ((cmp_start[None, :] <= sel_start[:, None] + SEL_BLOCK - 1)
          & (cmp_start[None, :] + CMP_BLOCK - 1 >= sel_start[:, None]))
    ov[:, (seq - CMP_BLOCK) // CMP_STRIDE + 1:] = False
    return jnp.asarray(ov.astype(np.float32), BF16)


def _row(v, width=None):
    v = v.reshape(1, -1).astype(F32)
    return v if width is None else _pad_cols(v, width)


def kernel(x, p, ffn1_norm, ffn1_wg, ffn1_wu, ffn1_wd, mix_norm, w_in, b_forget, q_norm_nsa, k_norm_cmp, k_norm_slc, k_norm_win, cmp_pos_k, cmp_pos_v, cmp_k_w1, cmp_k_w2, cmp_v_w1, cmp_v_w2, q_norm_fox, k_norm_fox, out_norm_nsa, out_norm_fox, w_out, ffn2_norm, ffn2_wg, ffn2_wu, ffn2_wd, ple_gate_norm, ple_w_gate, ple_w_proj, ple_norm):
    b, s, d = x.shape
    depth = ffn1_norm.shape[0]
    t = b * s
    rope_tab = _rope_tables(s)
    ov_t = _overlap_t(s)
    tile2 = lambda g: jnp.concatenate([g, g]).reshape(1, LANES).astype(F32)
    h = x.reshape(t, d)
    for i in range(depth):
        h = _ffn(h, _row(ffn1_norm[i]), ffn1_wg[i].astype(BF16), ffn1_wu[i].astype(BF16),
                 ffn1_wd[i].astype(BF16))

        gains = jnp.concatenate([tile2(q_norm_nsa[i]), tile2(k_norm_slc[i]), tile2(k_norm_win[i]),
                                 tile2(q_norm_fox[i]), tile2(k_norm_fox[i]),
                                 jnp.zeros((3, LANES), F32)], axis=0)
        (qa, kc, vc, kslx, vslx, kwnx, vwnx, gates, fq, fk, fv, cum) = _inproj(
            h.reshape(b, s, d), _row(mix_norm[i]), _layout_w_in(w_in[i]),
            _row(b_forget[i], LANES), gains, rope_tab)

        rows = s // CMP_STRIDE
        def shifted(tok):
            r0 = tok.reshape(b, rows, CMP_STRIDE * LANES)
            r1 = jnp.concatenate([r0[:, 1:], jnp.zeros_like(r0[:, :1])], axis=1)
            return r0, r1
        r0k, r1k = shifted(kc)
        r0v, r1v = shifted(vc)
        kcx, vcx = _compress(
            r0k, r1k, r0v, r1v, _layout_cmp_w1(cmp_k_w1[i]), _layout_cmp_w1(cmp_v_w1[i]),
            _layout_cmp_pos(cmp_pos_k[i]), _layout_cmp_pos(cmp_pos_v[i]),
            _pad_cols(cmp_k_w2[i], LANES).astype(BF16), _pad_cols(cmp_v_w2[i], LANES).astype(BF16),
            _row(k_norm_cmp[i], LANES))

        o_a = _nsa(qa, kcx, vcx, kslx, vslx, kwnx, vwnx, gates, ov_t)

        crow = jnp.transpose(cum[:, :, :N_HEADS_FOX], (0, 2, 1)).reshape(b, N_HEADS_FOX // 2, 2, s)
        o_b = _fox(fq, fk, fv, cum, crow)

        w_o = w_out[i].astype(BF16)
        h = _outproj(h, o_a.reshape(t, NSA_Q_W), o_b.reshape(t, FOX_W),
                     _row(out_norm_nsa[i]), _row(out_norm_fox[i]), w_o[:NSA_Q_W], w_o[NSA_Q_W:])

        h = _ffn(h, _row(ffn2_norm[i]), ffn2_wg[i].astype(BF16), ffn2_wu[i].astype(BF16),
                 ffn2_wd[i].astype(BF16))

        h = _ple(h, p[i].reshape(t, -1), _row(ple_gate_norm[i]), _row(ple_norm[i]),
                 ple_w_gate[i].astype(BF16), ple_w_proj[i].astype(BF16))
    return h.reshape(b, s, d)
```

```python
import numpy as np
import jax
import jax.numpy as jnp
from jax import lax
from jax.experimental import pallas as pl
from jax.experimental.pallas import tpu as pltpu

F32 = jnp.float32
BF16 = jnp.bfloat16

D_MODEL = 1024
HEAD_DIM = 64
N_HEADS_NSA = 8
N_KV_NSA = 2
GROUP_NSA = N_HEADS_NSA // N_KV_NSA
N_HEADS_FOX = 8
NSA_Q_W = N_HEADS_NSA * HEAD_DIM
NSA_KV_W = N_KV_NSA * HEAD_DIM
FOX_W = N_HEADS_FOX * HEAD_DIM
PROJ_SIZES = (NSA_Q_W, NSA_KV_W, NSA_KV_W, NSA_KV_W, NSA_KV_W, NSA_KV_W, NSA_KV_W,
              3 * N_HEADS_NSA, FOX_W, FOX_W, FOX_W, N_HEADS_FOX)
D_FF = 2816
D_PLE = 256
ROPE_THETA = 500000.0
ROT_DIM = HEAD_DIM // 4
CMP_BLOCK = 32
CMP_STRIDE = 16
CMP_HIDDEN = 256
SEL_BLOCK = 64
SEL_SHIFT = 6
SEL_TOPK = 16
WINDOW = 512
FORCED_SCORE = 1e6
EPS = 1e-6
NEG = -1e30

LANES = 128
BF16_ROWS = 16
VMEM_LIMIT = 56 * 1024 * 1024
AUG_STRIDE = 8

C_QA = 0
C_KC = C_QA + NSA_Q_W
C_VC = C_KC + LANES
C_KSL = C_VC + LANES
C_VSL = C_KSL + N_KV_NSA * LANES
C_KWN = C_VSL + LANES
C_VWN = C_KWN + N_KV_NSA * LANES
C_GATE = C_VWN + LANES
C_FQ = C_GATE + N_KV_NSA * LANES
C_FK = C_FQ + FOX_W
C_FV = C_FK + FOX_W
C_FL = C_FV + FOX_W
C_END = C_FL + LANES

FFN_TM = 1024
FFN_TF = 256
PROJ_TS = 512
NSA_TQ = 128
NSA_TK = 256
FOX_TQ = 256
FOX_TK = 256
OUT_TM = 512
PLE_TM = 512


def _dot(a, b):
    return jnp.dot(a, b, preferred_element_type=F32)


def _dot_nt(a, b):
    return lax.dot_general(a, b, (((1,), (1,)), ((), ())), preferred_element_type=F32)


def _params(*sem):
    return pltpu.CompilerParams(dimension_semantics=sem, vmem_limit_bytes=VMEM_LIMIT)


def _rms(x):
    return lax.rsqrt(jnp.mean(x * x, axis=-1, keepdims=True) + EPS)


def _split3(x):
    hi = x.astype(BF16)
    r1 = x - hi.astype(F32)
    mid = r1.astype(BF16)
    lo = (r1 - mid.astype(F32)).astype(BF16)
    return hi, mid, lo


def _ffn_kernel(x_ref, g_ref, wg_ref, wu_ref, wd_ref, o_ref, xn_ref, acc_ref):
    f = pl.program_id(1)

    @pl.when(f == 0)
    def _():
        x = x_ref[...]
        xn_ref[...] = (x * _rms(x) * g_ref[...]).astype(BF16)
        acc_ref[...] = jnp.zeros_like(acc_ref)

    xn = xn_ref[...]
    gate = _dot(xn, wg_ref[...])
    up = _dot(xn, wu_ref[...])
    act = (gate * jax.nn.sigmoid(gate) * up).astype(BF16)
    acc_ref[...] += _dot(act, wd_ref[...])

    @pl.when(f == pl.num_programs(1) - 1)
    def _():
        o_ref[...] = x_ref[...] + 0.5 * acc_ref[...]


def _ffn(x, g, wg, wu, wd):
    t, d = x.shape
    f = wg.shape[1]
    tm, tf = min(FFN_TM, t), FFN_TF
    return pl.pallas_call(
        _ffn_kernel,
        grid=(t // tm, f // tf),
        in_specs=[
            pl.BlockSpec((tm, d), lambda i, j: (i, 0)),
            pl.BlockSpec((1, d), lambda i, j: (0, 0)),
            pl.BlockSpec((d, tf), lambda i, j: (0, j)),
            pl.BlockSpec((d, tf), lambda i, j: (0, j)),
            pl.BlockSpec((tf, d), lambda i, j: (j, 0)),
        ],
        out_specs=pl.BlockSpec((tm, d), lambda i, j: (i, 0)),
        out_shape=jax.ShapeDtypeStruct((t, d), F32),
        scratch_shapes=[pltpu.VMEM((tm, d), BF16), pltpu.VMEM((tm, d), F32)],
        compiler_params=_params("parallel", "arbitrary"),
        name="ffn",
    )(x, g, wg, wu, wd)


def _log_sigmoid(x):
    return -(jnp.maximum(-x, 0.0) + jnp.log(1.0 + jnp.exp(-jnp.abs(x))))


def _inproj_kernel(h_ref, g_ref, w_ref, bf_ref, gains_ref, rope_ref,
                   qa_ref, kc_ref, vc_ref, ksl_ref, vsl_ref, kwn_ref, vwn_ref,
                   gate_ref, fq_ref, fk_ref, fv_ref, cum_ref, carry_ref):
    si = pl.program_id(1)
    ts = h_ref.shape[1]
    x = h_ref[0]
    a = (x * _rms(x) * g_ref[...]).astype(BF16)

    row = lax.broadcasted_iota(jnp.int32, (LANES, LANES), 0)
    col = lax.broadcasted_iota(jnp.int32, (LANES, LANES), 1)
    g_pair = jnp.where((row < HEAD_DIM) == (col < HEAD_DIM), 1.0, 0.0).astype(BF16)
    g_all = jnp.ones((LANES, LANES), BF16)
    cos_t, sin_lo, sin_hi = rope_ref[0], rope_ref[1], rope_ref[2]

    def proj(c0):
        return _dot(a, w_ref[:, c0:c0 + LANES])

    def norm(u, ones_mat, gain_row):
        u2 = u * u
        hi = u2.astype(BF16)
        lo = (u2 - hi.astype(F32)).astype(BF16)
        ms = (_dot(hi, ones_mat) + _dot(lo, ones_mat)) * (1.0 / HEAD_DIM)
        return u * lax.rsqrt(ms + EPS) * gains_ref[gain_row:gain_row + 1, :]

    def rope(u):
        return (u * cos_t + pltpu.roll(u, LANES - ROT_DIM // 2, 1) * sin_lo
                + pltpu.roll(u, ROT_DIM // 2, 1) * sin_hi)

    scale = HEAD_DIM ** -0.5
    for c in range(NSA_Q_W // LANES):
        sl = slice(c * LANES, (c + 1) * LANES)
        qa_ref[0, :, sl] = (rope(norm(proj(C_QA + c * LANES), g_pair, 0)) * scale).astype(BF16)

    kc_ref[0] = rope(proj(C_KC)).astype(BF16)
    vc_ref[0] = proj(C_VC).astype(BF16)
    vsl_ref[0] = proj(C_VSL).astype(BF16)
    vwn_ref[0] = proj(C_VWN).astype(BF16)

    pos = si * ts + lax.broadcasted_iota(jnp.int32, (ts, LANES), 0)
    lane = lax.broadcasted_iota(jnp.int32, (ts, LANES), 1)
    sel_onehot = jnp.where(lane == HEAD_DIM + jnp.right_shift(pos, SEL_SHIFT), 1.0, 0.0)
    for kh in range(N_KV_NSA):
        sl = slice(kh * LANES, (kh + 1) * LANES)
        ksl_ref[0, :, sl] = (rope(norm(proj(C_KSL + kh * LANES), g_all, 1)) + sel_onehot).astype(BF16)
        kwn_ref[0, :, sl] = rope(norm(proj(C_KWN + kh * LANES), g_all, 2)).astype(BF16)
        gate_ref[0, :, sl] = jax.nn.sigmoid(proj(C_GATE + kh * LANES))

    @pl.when(si == 0)
    def _():
        carry_ref[...] = jnp.zeros_like(carry_ref)

    lf = _log_sigmoid(proj(C_FL) + bf_ref[...])
    r_i = lax.broadcasted_iota(jnp.int32, (ts, ts), 0)
    c_i = lax.broadcasted_iota(jnp.int32, (ts, ts), 1)
    tri = jnp.where(r_i >= c_i, 1.0, 0.0).astype(BF16)
    hi, mid, lo = _split3(lf)
    c = _dot(tri, hi) + _dot(tri, mid) + _dot(tri, lo) + carry_ref[0:1, :]
    cum_ref[0] = c
    carry_ref[...] = jnp.broadcast_to(c[ts - 1:ts, :], carry_ref.shape)

    n_pairs = N_HEADS_FOX // 2
    p_row = lax.broadcasted_iota(jnp.int32, (LANES, n_pairs * LANES), 0)
    p_col = lax.broadcasted_iota(jnp.int32, (LANES, n_pairs * LANES), 1)
    target = jnp.right_shift(p_row, 1) * LANES + (p_row & 1) * AUG_STRIDE
    aug = None
    for i, part in enumerate(_split3(-c)):
        place = jnp.where((p_col == target + i) & (p_row < N_HEADS_FOX), 1.0, 0.0).astype(BF16)
        term = _dot(part, place)
        aug = term if aug is None else aug + term

    for c4 in range(n_pairs):
        sl = slice(c4 * LANES, (c4 + 1) * LANES)
        fq_ref[0, :, sl] = (norm(proj(C_FQ + c4 * LANES), g_pair, 3) * scale).astype(BF16)
        fv_ref[0, :, sl] = proj(C_FV + c4 * LANES).astype(BF16)
        fk_ref[0, :, 2 * c4 * LANES:(2 * c4 + 1) * LANES] = norm(proj(C_FK + c4 * LANES), g_pair, 4).astype(BF16)
        fk_ref[0, :, (2 * c4 + 1) * LANES:(2 * c4 + 2) * LANES] = aug[:, sl].astype(BF16)


def _inproj(h, g, w_all, bf_row, gains, rope_tab):
    b, s, d = h.shape
    ts = min(PROJ_TS, s)
    tok = lambda c: pl.BlockSpec((1, ts, c), lambda i, j: (i, j, 0))
    const = lambda shape: pl.BlockSpec(shape, lambda i, j: (0,) * len(shape))
    kvx = N_KV_NSA * LANES
    specs = [(NSA_Q_W, BF16), (LANES, BF16), (LANES, BF16), (kvx, BF16), (LANES, BF16),
             (kvx, BF16), (LANES, BF16), (kvx, F32), (FOX_W, BF16), (2 * FOX_W, BF16),
             (FOX_W, BF16), (LANES, F32)]
    out_shape = [jax.ShapeDtypeStruct((b, s, c), dt) for c, dt in specs]
    return pl.pallas_call(
        _inproj_kernel,
        grid=(b, s // ts),
        in_specs=[
            tok(d),
            const((1, d)),
            const((d, C_END)),
            const((1, LANES)),
            const((8, LANES)),
            pl.BlockSpec((3, ts, LANES), lambda i, j: (0, j, 0)),
        ],
        out_specs=[tok(c) for c, _ in specs],
        out_shape=out_shape,
        scratch_shapes=[pltpu.VMEM((8, LANES), F32)],
        compiler_params=_params("parallel", "arbitrary"),
        name="inproj",
    )(h, g, w_all, bf_row, gains, rope_tab)


def _compress_kernel(r0k_ref, r1k_ref, r0v_ref, r1v_ref, w1k_ref, w1v_ref,
                     pk_ref, pv_ref, w2k_ref, w2v_ref, gain_ref, kc_ref, vc_ref):
    def mlp(r0_ref, r1_ref, w1_ref, p_ref, w2_ref, kh):
        top, bot = w1_ref[kh, 0], w1_ref[kh, 1]
        bias = _dot(p_ref[0], top) + _dot(p_ref[1], bot)
        hid = _dot(r0_ref[0], top) + _dot(r1_ref[0], bot) + bias[0:1, :]
        hid = hid * jax.nn.sigmoid(hid)
        return _dot(hid.astype(BF16), w2_ref[...])

    for kh in range(N_KV_NSA):
        kc = mlp(r0k_ref, r1k_ref, w1k_ref, pk_ref, w2k_ref, kh)
        ms = jnp.sum(kc * kc, axis=-1, keepdims=True) * (1.0 / HEAD_DIM)
        kc_ref[0, kh] = (kc * lax.rsqrt(ms + EPS) * gain_ref[...]).astype(BF16)
        vc_ref[0, kh] = mlp(r0v_ref, r1v_ref, w1v_ref, pv_ref, w2v_ref, kh).astype(BF16)


def _compress(r0k, r1k, r0v, r1v, w1k, w1v, pk, pv, w2k, w2v, gain):
    b, r, w = r0k.shape
    rows = pl.BlockSpec((1, r, w), lambda i: (i, 0, 0))
    const = lambda a: pl.BlockSpec(a.shape, lambda i: (0,) * a.ndim)
    out = jax.ShapeDtypeStruct((b, N_KV_NSA, r, LANES), BF16)
    out_spec = pl.BlockSpec((1, N_KV_NSA, r, LANES), lambda i: (i, 0, 0, 0))
    return pl.pallas_call(
        _compress_kernel,
        grid=(b,),
        in_specs=[rows, rows, rows, rows, const(w1k), const(w1v), const(pk), const(pv),
                  const(w2k), const(w2v), const(gain)],
        out_specs=[out_spec, out_spec],
        out_shape=[out, out],
        compiler_params=_params("parallel"),
        name="compress",
    )(r0k, r1k, r0v, r1v, w1k, w1v, pk, pv, w2k, w2v, gain)


def _ones_rows(width):
    return jnp.ones((BF16_ROWS, width), BF16)


def _nsa_kernel(qt_ref, kc_ref, vct_ref, ksl_ref, vslt_ref, kwn_ref, vwnt_ref, gt_ref, ov_ref,
                o_ref, qaug_ref, m_ref, acc_ref):
    qi = pl.program_id(2)
    tq = qt_ref.shape[2]
    s_len = ksl_ref.shape[1]
    n_cmp_rows = kc_ref.shape[2]
    n_sel = ov_ref.shape[0]
    top = min(SEL_TOPK, n_sel)
    grp = GROUP_NSA
    cols = grp * tq
    tk = min(NSA_TK, s_len)
    t0 = qi * tq

    def tok_of_col(rows):
        return t0 + (lax.broadcasted_iota(jnp.int32, (rows, cols), 1) & (tq - 1))

    def softmax_cols(s, mask):
        s = jnp.where(mask, s, NEG)
        mx = jnp.max(s, axis=0, keepdims=True)
        e = jnp.where(mask, jnp.exp(s - mx), 0.0)
        den = jnp.sum(e, axis=0, keepdims=True)
        return e / jnp.where(den > 0.0, den, 1.0)

    qt = qt_ref[0]
    qaug_ref[0:HEAD_DIM, :] = jnp.concatenate(
        [qt[g * HEAD_DIM:(g + 1) * HEAD_DIM] for g in range(grp)], axis=1)
    qaug_ref[HEAD_DIM:, :] = jnp.zeros((HEAD_DIM, cols), BF16)

    n_c = lax.broadcasted_iota(jnp.int32, (n_cmp_rows, cols), 0)
    m_c = n_c * CMP_STRIDE + (CMP_BLOCK - 1) <= tok_of_col(n_cmp_rows)
    p_c = softmax_cols(_dot(kc_ref[0, 0], qaug_ref[...]), m_c)
    o_c = _dot(vct_ref[0, 0], p_c.astype(BF16))

    p_sum = p_c[:, 0:tq]
    for g in range(1, grp):
        p_sum = p_sum + p_c[:, g * tq:(g + 1) * tq]
    p_hi = p_sum.astype(BF16)
    p_lo = (p_sum - p_hi.astype(F32)).astype(BF16)
    ov = ov_ref[...]
    imp = _dot(ov, p_hi) + _dot(ov, p_lo)
    j_blk = lax.broadcasted_iota(jnp.int32, (n_sel, tq), 0)
    cur = jnp.right_shift(t0 + lax.broadcasted_iota(jnp.int32, (n_sel, tq), 1), SEL_SHIFT)
    forced = (j_blk == 0) | (j_blk == cur) | (j_blk == cur - 1)
    imp = jnp.where(j_blk <= cur, jnp.where(forced, FORCED_SCORE, imp), -1.0)
    rank = jnp.zeros((n_sel, tq), jnp.int32)
    for i in range(n_sel):
        r_i = imp[i:i + 1, :]
        beats = (r_i > imp) | ((r_i == imp) & (j_blk > i))
        rank = rank + jnp.where(beats, 1, 0)
    chosen = (rank < top) & (imp >= 0.0)
    bias_t = jnp.where(chosen, 0.0, NEG)
    if n_sel < HEAD_DIM:
        bias_t = jnp.concatenate([bias_t, jnp.zeros((HEAD_DIM - n_sel, tq), F32)], axis=0)
    qaug_ref[HEAD_DIM:, :] = jnp.concatenate([bias_t.astype(BF16)] * grp, axis=1)

    m_ref[...] = jnp.full_like(m_ref, NEG)
    acc_ref[...] = jnp.zeros_like(acc_ref)

    def slc_tile(k0, diagonal):
        s = _dot(ksl_ref[0, pl.ds(k0, tk), :], qaug_ref[...])
        if diagonal:
            key = k0 + lax.broadcasted_iota(jnp.int32, (tk, cols), 0)
            s = jnp.where(key <= tok_of_col(tk), s, NEG)
        m_old = m_ref[...]
        m_new = jnp.maximum(m_old, jnp.max(s, axis=0, keepdims=True))
        p = jnp.exp(s - m_new).astype(BF16)
        vt = jnp.concatenate([vslt_ref[0, :, pl.ds(k0, tk)], _ones_rows(tk)], axis=0)
        acc_ref[...] = jnp.exp(m_old - m_new) * acc_ref[...] + _dot(vt, p)
        m_ref[...] = m_new

    n_full = t0 // tk

    def body(j, carry):
        slc_tile(pl.multiple_of(j * tk, tk), False)
        return carry

    lax.fori_loop(0, n_full, body, 0)
    slc_tile(pl.multiple_of(n_full * tk, tk), True)
    o_s = acc_ref[0:HEAD_DIM, :] / acc_ref[HEAD_DIM:HEAD_DIM + 1, :]

    span = min(WINDOW + tq, s_len)
    start = pl.multiple_of(jnp.maximum(t0 + tq - span, 0), LANES)
    t_w = tok_of_col(span)
    k_w = start + lax.broadcasted_iota(jnp.int32, (span, cols), 0)
    m_w = (k_w <= t_w) & (t_w - k_w < WINDOW)
    p_w = softmax_cols(_dot(kwn_ref[0, pl.ds(start, span), :], qaug_ref[...]), m_w)
    o_w = _dot(vwnt_ref[0, :, pl.ds(start, span)], p_w.astype(BF16))

    gt = gt_ref[0]
    outs = []
    for g in range(grp):
        sl = slice(g * tq, (g + 1) * tq)
        outs.append(gt[3 * g:3 * g + 1] * o_c[:, sl] + gt[3 * g + 1:3 * g + 2] * o_s[:, sl]
                    + gt[3 * g + 2:3 * g + 3] * o_w[:, sl])
    o_ref[0] = jnp.concatenate(outs, axis=0)


def _nsa(qat, kcx, vct, kslx, vslt, kwnx, vwnt, gates_t, ov_t):
    b, _, s = qat.shape
    tq = min(NSA_TQ, s)
    assert tq & (tq - 1) == 0 and tq % LANES == 0
    r = kcx.shape[2]
    gw = GROUP_NSA * HEAD_DIM
    cols = GROUP_NSA * tq
    gate_rows = 16
    key_major = pl.BlockSpec((1, s, LANES), lambda i, k, j: (i, 0, k))
    val_major = pl.BlockSpec((1, HEAD_DIM, s), lambda i, k, j: (i, k, 0))
    return pl.pallas_call(
        _nsa_kernel,
        grid=(b, N_KV_NSA, s // tq),
        in_specs=[
            pl.BlockSpec((1, gw, tq), lambda i, k, j: (i, k, j)),
            pl.BlockSpec((1, 1, r, LANES), lambda i, k, j: (i, k, 0, 0)),
            pl.BlockSpec((1, 1, HEAD_DIM, r), lambda i, k, j: (i, k, 0, 0)),
            key_major, val_major, key_major, val_major,
            pl.BlockSpec((1, gate_rows, tq), lambda i, k, j: (i, k * (LANES // gate_rows), j)),
            pl.BlockSpec(ov_t.shape, lambda i, k, j: (0, 0)),
        ],
        out_specs=pl.BlockSpec((1, gw, tq), lambda i, k, j: (i, k, j)),
        out_shape=jax.ShapeDtypeStruct((b, NSA_Q_W, s), F32),
        scratch_shapes=[
            pltpu.VMEM((LANES, cols), BF16),
            pltpu.VMEM((1, cols), F32),
            pltpu.VMEM((HEAD_DIM + BF16_ROWS, cols), F32),
        ],
        compiler_params=_params("parallel", "parallel", "arbitrary"),
        name="nsa",
    )(qat, kcx, vct, kslx, vslt, kwnx, vwnt, gates_t, ov_t)


def _fox_kernel(qt_ref, k_ref, vt_ref, cq_ref, o_ref, qaug_ref, m_ref, acc_ref):
    pair = pl.program_id(1)
    qi = pl.program_id(2)
    tq = qt_ref.shape[2]
    tk = tq
    t0 = qi * tq
    qt = qt_ref[0]
    row = lax.broadcasted_iota(jnp.int32, (LANES, tq), 0)
    zero = jnp.zeros_like(qt)
    cq = []
    for e in range(2):
        qaug_ref[e, 0:LANES, :] = jnp.where((row >= e * HEAD_DIM) & (row < (e + 1) * HEAD_DIM), qt, zero)
        pick = (row >= e * AUG_STRIDE) & (row < e * AUG_STRIDE + 3)
        qaug_ref[e, LANES:, :] = jnp.where(pick, 1.0, 0.0).astype(BF16)
        cq.append(cq_ref[0, pl.ds(2 * pair + e, 1), :])

    m_ref[...] = jnp.full_like(m_ref, NEG)
    acc_ref[...] = jnp.zeros_like(acc_ref)

    def tile(k0, diagonal):
        kt = k_ref[0, pl.ds(k0, tk), :]
        for e in range(2):
            s = _dot(kt, qaug_ref[e])
            if diagonal:
                r_t = lax.broadcasted_iota(jnp.int32, (tk, tq), 0)
                c_t = lax.broadcasted_iota(jnp.int32, (tk, tq), 1)
                s = jnp.where(r_t <= c_t, s, NEG)
            m_old = m_ref[e]
            m_new = jnp.maximum(m_old, jnp.max(s, axis=0, keepdims=True) + cq[e])
            p = jnp.exp(s + (cq[e] - m_new)).astype(BF16)
            vt = jnp.concatenate([vt_ref[0, e * HEAD_DIM:(e + 1) * HEAD_DIM, pl.ds(k0, tk)],
                                  _ones_rows(tk)], axis=0)
            acc_ref[e] = jnp.exp(m_old - m_new) * acc_ref[e] + _dot(vt, p)
            m_ref[e] = m_new

    def body(j, carry):
        tile(pl.multiple_of(j * tk, tk), False)
        return carry

    lax.fori_loop(0, qi, body, 0)
    tile(pl.multiple_of(t0, tk), True)
    o_ref[0] = jnp.concatenate(
        [acc_ref[e, 0:HEAD_DIM, :] / acc_ref[e, HEAD_DIM:HEAD_DIM + 1, :] for e in range(2)], axis=0)


def _fox(fqt, fkx, fvt, cq):
    b, _, s = fqt.shape
    tq = min(FOX_TQ, s)
    return pl.pallas_call(
        _fox_kernel,
        grid=(b, N_HEADS_FOX // 2, s // tq),
        in_specs=[
            pl.BlockSpec((1, LANES, tq), lambda i, p, j: (i, p, j)),
            pl.BlockSpec((1, s, 2 * LANES), lambda i, p, j: (i, 0, p)),
            pl.BlockSpec((1, LANES, s), lambda i, p, j: (i, p, 0)),
            pl.BlockSpec((1, N_HEADS_FOX, tq), lambda i, p, j: (i, 0, j)),
        ],
        out_specs=pl.BlockSpec((1, LANES, tq), lambda i, p, j: (i, p, j)),
        out_shape=jax.ShapeDtypeStruct((b, FOX_W, s), F32),
        scratch_shapes=[
            pltpu.VMEM((2, 2 * LANES, tq), BF16),
            pltpu.VMEM((2, 1, tq), F32),
            pltpu.VMEM((2, HEAD_DIM + BF16_ROWS, tq), F32),
        ],
        compiler_params=_params("parallel", "parallel", "arbitrary"),
        name="fox",
    )(fqt, fkx, fvt, cq)


def _outproj_kernel(h_ref, oa_ref, ob_ref, ga_ref, gb_ref, wa_ref, wb_ref, o_ref):
    oa = oa_ref[...]
    ob = ob_ref[...]
    na = (oa * _rms(oa) * ga_ref[...]).astype(BF16)
    nb = (ob * _rms(ob) * gb_ref[...]).astype(BF16)
    o_ref[...] = h_ref[...] + _dot(na, wa_ref[...]) + _dot(nb, wb_ref[...])


def _outproj(h, oa, ob, ga, gb, wa, wb):
    t, d = h.shape
    tm = min(OUT_TM, t)
    rows = lambda c: pl.BlockSpec((tm, c), lambda i: (i, 0))
    const = lambda a: pl.BlockSpec(a.shape, lambda i: (0,) * a.ndim)
    return pl.pallas_call(
        _outproj_kernel,
        grid=(t // tm,),
        in_specs=[rows(d), rows(NSA_Q_W), rows(FOX_W), const(ga), const(gb), const(wa), const(wb)],
        out_specs=rows(d),
        out_shape=jax.ShapeDtypeStruct((t, d), F32),
        compiler_params=_params("parallel"),
        name="outproj",
    )(h, oa, ob, ga, gb, wa, wb)


def _ple_kernel(h_ref, p_ref, gg_ref, gp_ref, wg_ref, wp_ref, o_ref):
    h = h_ref[...]
    hn = (h * _rms(h) * gg_ref[...]).astype(BF16)
    gate = jax.nn.sigmoid(_dot(hn, wg_ref[...]))
    e = _dot(p_ref[...].astype(BF16), wp_ref[...])
    o_ref[...] = h + gate * (e * _rms(e) * gp_ref[...])


def _ple(h, p, gg, gp, wg, wp):
    t, d = h.shape
    tm = min(PLE_TM, t)
    rows = lambda c: pl.BlockSpec((tm, c), lambda i: (i, 0))
    const = lambda a: pl.BlockSpec(a.shape, lambda i: (0,) * a.ndim)
    return pl.pallas_call(
        _ple_kernel,
        grid=(t // tm,),
        in_specs=[rows(d), rows(p.shape[1]), const(gg), const(gp), const(wg), const(wp)],
        out_specs=rows(d),
        out_shape=jax.ShapeDtypeStruct((t, d), F32),
        compiler_params=_params("parallel"),
        name="ple",
    )(h, p, gg, gp, wg, wp)


def _expand_heads(w, n):
    w = w.reshape(w.shape[0], n, HEAD_DIM)
    return jnp.concatenate([w, jnp.zeros_like(w)], axis=-1).reshape(w.shape[0], n * LANES)


def _pad_cols(w, width):
    return jnp.pad(w, ((0, 0), (0, width - w.shape[1])))


def _layout_w_in(w_in):
    splits = [int(v) for v in np.cumsum(PROJ_SIZES)[:-1]]
    qa, kc, vc, ksl, vsl, kwn, vwn, ga, qf, kf, vf, fl = jnp.split(w_in, splits, axis=-1)
    per_group = 3 * GROUP_NSA
    ga_x = jnp.concatenate([_pad_cols(ga[:, k * per_group:(k + 1) * per_group], LANES)
                            for k in range(N_KV_NSA)], axis=-1)
    cols = [qa, kc, vc, _expand_heads(ksl, N_KV_NSA), vsl, _expand_heads(kwn, N_KV_NSA), vwn,
            ga_x, qf, kf, vf, _pad_cols(fl, LANES)]
    return jnp.concatenate(cols, axis=-1).astype(BF16)


def _rope_tables(seq):
    pos = jnp.arange(seq, dtype=F32)
    inv = ROPE_THETA ** (-jnp.arange(0, ROT_DIM, 2, dtype=F32) / ROT_DIM)
    ang = pos[:, None] * inv[None, :]
    cos, sin = jnp.cos(ang), jnp.sin(ang)
    half = ROT_DIM // 2
    rest = HEAD_DIM - ROT_DIM
    ones = jnp.ones((seq, rest), F32)
    zeros = jnp.zeros((seq, rest), F32)
    zh = jnp.zeros((seq, half), F32)
    cos_t = jnp.concatenate([cos, cos, ones], axis=-1)
    sin_lo = jnp.concatenate([-sin, zh, zeros], axis=-1)
    sin_hi = jnp.concatenate([zh, sin, zeros], axis=-1)
    tile2 = lambda t: jnp.concatenate([t, t], axis=-1)
    return jnp.stack([tile2(cos_t), tile2(sin_lo), tile2(sin_hi)])


def _layout_cmp_w1(w1):
    hid = w1.shape[1]
    w = w1.reshape(2, CMP_STRIDE, HEAD_DIM, hid)
    z = jnp.zeros_like(w)
    per_head = [jnp.concatenate([w, z], axis=2), jnp.concatenate([z, w], axis=2)]
    return jnp.stack(per_head).reshape(N_KV_NSA, 2, CMP_STRIDE * LANES, hid).astype(BF16)


def _layout_cmp_pos(pos):
    p = pos.reshape(2, CMP_STRIDE, HEAD_DIM)
    p = jnp.concatenate([p, p], axis=-1).reshape(2, 1, CMP_STRIDE * LANES)
    return jnp.broadcast_to(p, (2, 8, CMP_STRIDE * LANES)).astype(BF16)


def _overlap_t(seq):
    rows = seq // CMP_STRIDE
    n_sel = seq // SEL_BLOCK
    cmp_start = np.arange(rows) * CMP_STRIDE
    sel_start = np.arange(n_sel) * SEL_BLOCK
    ov = ((cmp_start[None, :] <= sel_start[:, None] + SEL_BLOCK - 1)
          & (cmp_start[None, :] + CMP_BLOCK - 1 >= sel_start[:, None]))
    ov[:, (seq - CMP_BLOCK) // CMP_STRIDE + 1:] = False
    return jnp.asarray(ov.astype(np.float32), BF16)


def _row(v, width=None):
    v = v.reshape(1, -1).astype(F32)
    return v if width is None else _pad_cols(v, width)


def _feature_major(a):
    return jnp.transpose(a, (0, 2, 1))


def kernel(x, p, ffn1_norm, ffn1_wg, ffn1_wu, ffn1_wd, mix_norm, w_in, b_forget, q_norm_nsa, k_norm_cmp, k_norm_slc, k_norm_win, cmp_pos_k, cmp_pos_v, cmp_k_w1, cmp_k_w2, cmp_v_w1, cmp_v_w2, q_norm_fox, k_norm_fox, out_norm_nsa, out_norm_fox, w_out, ffn2_norm, ffn2_wg, ffn2_wu, ffn2_wd, ple_gate_norm, ple_w_gate, ple_w_proj, ple_norm):
    b, s, d = x.shape
    assert s // SEL_BLOCK <= HEAD_DIM and s % max(NSA_TK, FOX_TK) == 0
    depth = ffn1_norm.shape[0]
    t = b * s
    rope_tab = _rope_tables(s)
    ov_t = _overlap_t(s)
    tile2 = lambda g: jnp.concatenate([g, g]).reshape(1, LANES).astype(F32)
    h = x.reshape(t, d)
    for i in range(depth):
        h = _ffn(h, _row(ffn1_norm[i]), ffn1_wg[i].astype(BF16), ffn1_wu[i].astype(BF16),
                 ffn1_wd[i].astype(BF16))

        gains = jnp.concatenate([tile2(q_norm_nsa[i]), tile2(k_norm_slc[i]), tile2(k_norm_win[i]),
                                 tile2(q_norm_fox[i]), tile2(k_norm_fox[i]),
                                 jnp.zeros((3, LANES), F32)], axis=0)
        (qa, kc, vc, kslx, vsl, kwnx, vwn, gates, fq, fkx, fv, cum) = _inproj(
            h.reshape(b, s, d), _row(mix_norm[i]), _layout_w_in(w_in[i]),
            _row(b_forget[i], LANES), gains, rope_tab)

        rows = s // CMP_STRIDE

        def shifted(tok):
            r0 = tok.reshape(b, rows, CMP_STRIDE * LANES)
            r1 = jnp.concatenate([r0[:, 1:], jnp.zeros_like(r0[:, :1])], axis=1)
            return r0, r1

        r0k, r1k = shifted(kc)
        r0v, r1v = shifted(vc)
        kcx, vcx = _compress(
            r0k, r1k, r0v, r1v, _layout_cmp_w1(cmp_k_w1[i]), _layout_cmp_w1(cmp_v_w1[i]),
            _layout_cmp_pos(cmp_pos_k[i]), _layout_cmp_pos(cmp_pos_v[i]),
            _pad_cols(cmp_k_w2[i], LANES).astype(BF16), _pad_cols(cmp_v_w2[i], LANES).astype(BF16),
            _row(k_norm_cmp[i], LANES))
        vct = jnp.transpose(vcx[..., :HEAD_DIM], (0, 1, 3, 2))

        o_at = _nsa(_feature_major(qa), kcx, vct, kslx, _feature_major(vsl), kwnx,
                    _feature_major(vwn), _feature_major(gates), ov_t)
        o_bt = _fox(_feature_major(fq), fkx, _feature_major(fv),
                    _feature_major(cum[:, :, :N_HEADS_FOX]))

        w_o = w_out[i].astype(BF16)
        h = _outproj(h, _feature_major(o_at).reshape(t, NSA_Q_W), _feature_major(o_bt).reshape(t, FOX_W),
                     _row(out_norm_nsa[i]), _row(out_norm_fox[i]), w_o[:NSA_Q_W], w_o[NSA_Q_W:])

        h = _ffn(h, _row(ffn2_norm[i]), ffn2_wg[i].astype(BF16), ffn2_wu[i].astype(BF16),
                 ffn2_wd[i].astype(BF16))

        h = _ple(h, p[i].reshape(t, -1), _row(ple_gate_norm[i]), _row(ple_norm[i]),
                 ple_w_gate[i].astype(BF16), ple_w_proj[i].astype(BF16))
    return h.reshape(b, s, d)
```

```python
import numpy as np
import jax
import jax.numpy as jnp
from jax import lax
from jax.experimental import pallas as pl
from jax.experimental.pallas import tpu as pltpu

F32 = jnp.float32
BF16 = jnp.bfloat16

D_MODEL = 1024
HEAD_DIM = 64
N_HEADS_NSA = 8
N_KV_NSA = 2
GROUP_NSA = N_HEADS_NSA // N_KV_NSA
N_HEADS_FOX = 8
NSA_Q_W = N_HEADS_NSA * HEAD_DIM
NSA_KV_W = N_KV_NSA * HEAD_DIM
FOX_W = N_HEADS_FOX * HEAD_DIM
PROJ_SIZES = (NSA_Q_W, NSA_KV_W, NSA_KV_W, NSA_KV_W, NSA_KV_W, NSA_KV_W, NSA_KV_W,
              3 * N_HEADS_NSA, FOX_W, FOX_W, FOX_W, N_HEADS_FOX)
D_FF = 2816
D_PLE = 256
ROPE_THETA = 500000.0
ROT_DIM = HEAD_DIM // 4
CMP_BLOCK = 32
CMP_STRIDE = 16
CMP_HIDDEN = 256
SEL_BLOCK = 64
SEL_SHIFT = 6
SEL_TOPK = 16
WINDOW = 512
FORCED_SCORE = 1e6
EPS = 1e-6
NEG = -1e30

LANES = 128
BF16_ROWS = 16
VMEM_LIMIT = 56 * 1024 * 1024
AUG_STRIDE = 8

C_QA = 0
C_KC = C_QA + NSA_Q_W
C_VC = C_KC + LANES
C_KSL = C_VC + LANES
C_VSL = C_KSL + N_KV_NSA * LANES
C_KWN = C_VSL + LANES
C_VWN = C_KWN + N_KV_NSA * LANES
C_GATE = C_VWN + LANES
C_FQ = C_GATE + N_KV_NSA * LANES
C_FK = C_FQ + FOX_W
C_FV = C_FK + FOX_W
C_FL = C_FV + FOX_W
C_END = C_FL + LANES

FFN_TM = 1024
FFN_TF = 256
PROJ_TS = 512
NSA_TQ = 128
NSA_TK = 256
FOX_TQ = 512
FOX_TK = 256
OUT_TM = 512
PLE_TM = 512


def _dot(a, b):
    return jnp.dot(a, b, preferred_element_type=F32)


def _dot_nt(a, b):
    return lax.dot_general(a, b, (((1,), (1,)), ((), ())), preferred_element_type=F32)


def _params(*sem):
    return pltpu.CompilerParams(dimension_semantics=sem, vmem_limit_bytes=VMEM_LIMIT)


def _rms(x):
    return lax.rsqrt(jnp.mean(x * x, axis=-1, keepdims=True) + EPS)


def _split3(x):
    hi = x.astype(BF16)
    r1 = x - hi.astype(F32)
    mid = r1.astype(BF16)
    lo = (r1 - mid.astype(F32)).astype(BF16)
    return hi, mid, lo


def _ffn_kernel(x_ref, g_ref, wg_ref, wu_ref, wd_ref, o_ref, xn_ref, acc_ref):
    f = pl.program_id(1)

    @pl.when(f == 0)
    def _():
        x = x_ref[...]
        xn_ref[...] = (x * _rms(x) * g_ref[...]).astype(BF16)
        acc_ref[...] = jnp.zeros_like(acc_ref)

    xn = xn_ref[...]
    gate = _dot(xn, wg_ref[...])
    up = _dot(xn, wu_ref[...])
    act = (gate * jax.nn.sigmoid(gate) * up).astype(BF16)
    acc_ref[...] += _dot(act, wd_ref[...])

    @pl.when(f == pl.num_programs(1) - 1)
    def _():
        o_ref[...] = x_ref[...] + 0.5 * acc_ref[...]


def _ffn(x, g, wg, wu, wd):
    t, d = x.shape
    f = wg.shape[1]
    tm, tf = min(FFN_TM, t), FFN_TF
    return pl.pallas_call(
        _ffn_kernel,
        grid=(t // tm, f // tf),
        in_specs=[
            pl.BlockSpec((tm, d), lambda i, j: (i, 0)),
            pl.BlockSpec((1, d), lambda i, j: (0, 0)),
            pl.BlockSpec((d, tf), lambda i, j: (0, j)),
            pl.BlockSpec((d, tf), lambda i, j: (0, j)),
            pl.BlockSpec((tf, d), lambda i, j: (j, 0)),
        ],
        out_specs=pl.BlockSpec((tm, d), lambda i, j: (i, 0)),
        out_shape=jax.ShapeDtypeStruct((t, d), F32),
        scratch_shapes=[pltpu.VMEM((tm, d), BF16), pltpu.VMEM((tm, d), F32)],
        compiler_params=_params("parallel", "arbitrary"),
        name="ffn",
    )(x, g, wg, wu, wd)


def _log_sigmoid(x):
    return -(jnp.maximum(-x, 0.0) + jnp.log(1.0 + jnp.exp(-jnp.abs(x))))


def _inproj_kernel(h_ref, g_ref, w_ref, bf_ref, gains_ref, rope_ref,
                   qa_ref, kc_ref, vc_ref, ksl_ref, vsl_ref, kwn_ref, vwn_ref,
                   gate_ref, fq_ref, fk_ref, fv_ref, cum_ref, carry_ref):
    si = pl.program_id(1)
    ts = h_ref.shape[1]
    x = h_ref[0]
    a = (x * _rms(x) * g_ref[...]).astype(BF16)

    row = lax.broadcasted_iota(jnp.int32, (LANES, LANES), 0)
    col = lax.broadcasted_iota(jnp.int32, (LANES, LANES), 1)
    g_pair = jnp.where((row < HEAD_DIM) == (col < HEAD_DIM), 1.0, 0.0).astype(BF16)
    g_all = jnp.ones((LANES, LANES), BF16)
    cos_t, sin_lo, sin_hi = rope_ref[0], rope_ref[1], rope_ref[2]

    def proj(c0):
        return _dot(a, w_ref[:, c0:c0 + LANES])

    def norm(u, ones_mat, gain_row):
        u2 = u * u
        hi = u2.astype(BF16)
        lo = (u2 - hi.astype(F32)).astype(BF16)
        ms = (_dot(hi, ones_mat) + _dot(lo, ones_mat)) * (1.0 / HEAD_DIM)
        return u * lax.rsqrt(ms + EPS) * gains_ref[gain_row:gain_row + 1, :]

    def rope(u):
        return (u * cos_t + pltpu.roll(u, LANES - ROT_DIM // 2, 1) * sin_lo
                + pltpu.roll(u, ROT_DIM // 2, 1) * sin_hi)

    scale = HEAD_DIM ** -0.5
    for c in range(NSA_Q_W // LANES):
        sl = slice(c * LANES, (c + 1) * LANES)
        qa_ref[0, :, sl] = (rope(norm(proj(C_QA + c * LANES), g_pair, 0)) * scale).astype(BF16)

    kc_ref[0] = rope(proj(C_KC)).astype(BF16)
    vc_ref[0] = proj(C_VC).astype(BF16)
    vsl_ref[0] = proj(C_VSL).astype(BF16)
    vwn_ref[0] = proj(C_VWN).astype(BF16)

    pos = si * ts + lax.broadcasted_iota(jnp.int32, (ts, LANES), 0)
    lane = lax.broadcasted_iota(jnp.int32, (ts, LANES), 1)
    sel_onehot = jnp.where(lane == HEAD_DIM + jnp.right_shift(pos, SEL_SHIFT), 1.0, 0.0)
    for kh in range(N_KV_NSA):
        sl = slice(kh * LANES, (kh + 1) * LANES)
        ksl_ref[0, :, sl] = (rope(norm(proj(C_KSL + kh * LANES), g_all, 1)) + sel_onehot).astype(BF16)
        kwn_ref[0, :, sl] = rope(norm(proj(C_KWN + kh * LANES), g_all, 2)).astype(BF16)
        gate_ref[0, :, sl] = jax.nn.sigmoid(proj(C_GATE + kh * LANES))

    @pl.when(si == 0)
    def _():
        carry_ref[...] = jnp.zeros_like(carry_ref)

    lf = _log_sigmoid(proj(C_FL) + bf_ref[...])
    r_i = lax.broadcasted_iota(jnp.int32, (ts, ts), 0)
    c_i = lax.broadcasted_iota(jnp.int32, (ts, ts), 1)
    tri = jnp.where(r_i >= c_i, 1.0, 0.0).astype(BF16)
    hi, mid, lo = _split3(lf)
    c = _dot(tri, hi) + _dot(tri, mid) + _dot(tri, lo) + carry_ref[0:1, :]
    cum_ref[0] = c
    carry_ref[...] = jnp.broadcast_to(c[ts - 1:ts, :], carry_ref.shape)

    n_pairs = N_HEADS_FOX // 2
    p_row = lax.broadcasted_iota(jnp.int32, (LANES, n_pairs * LANES), 0)
    p_col = lax.broadcasted_iota(jnp.int32, (LANES, n_pairs * LANES), 1)
    target = jnp.right_shift(p_row, 1) * LANES + (p_row & 1) * AUG_STRIDE
    aug = None
    for i, part in enumerate(_split3(-c)):
        place = jnp.where((p_col == target + i) & (p_row < N_HEADS_FOX), 1.0, 0.0).astype(BF16)
        term = _dot(part, place)
        aug = term if aug is None else aug + term

    for c4 in range(n_pairs):
        sl = slice(c4 * LANES, (c4 + 1) * LANES)
        fq_ref[0, :, sl] = (norm(proj(C_FQ + c4 * LANES), g_pair, 3) * scale).astype(BF16)
        fv_ref[0, :, sl] = proj(C_FV + c4 * LANES).astype(BF16)
        fk_ref[0, :, 2 * c4 * LANES:(2 * c4 + 1) * LANES] = norm(proj(C_FK + c4 * LANES), g_pair, 4).astype(BF16)
        fk_ref[0, :, (2 * c4 + 1) * LANES:(2 * c4 + 2) * LANES] = aug[:, sl].astype(BF16)


def _inproj(h, g, w_all, bf_row, gains, rope_tab):
    b, s, d = h.shape
    ts = min(PROJ_TS, s)
    tok = lambda c: pl.BlockSpec((1, ts, c), lambda i, j: (i, j, 0))
    const = lambda shape: pl.BlockSpec(shape, lambda i, j: (0,) * len(shape))
    kvx = N_KV_NSA * LANES
    specs = [(NSA_Q_W, BF16), (LANES, BF16), (LANES, BF16), (kvx, BF16), (LANES, BF16),
             (kvx, BF16), (LANES, BF16), (kvx, F32), (FOX_W, BF16), (2 * FOX_W, BF16),
             (FOX_W, BF16), (LANES, F32)]
    out_shape = [jax.ShapeDtypeStruct((b, s, c), dt) for c, dt in specs]
    return pl.pallas_call(
        _inproj_kernel,
        grid=(b, s // ts),
        in_specs=[
            tok(d),
            const((1, d)),
            const((d, C_END)),
            const((1, LANES)),
            const((8, LANES)),
            pl.BlockSpec((3, ts, LANES), lambda i, j: (0, j, 0)),
        ],
        out_specs=[tok(c) for c, _ in specs],
        out_shape=out_shape,
        scratch_shapes=[pltpu.VMEM((8, LANES), F32)],
        compiler_params=_params("parallel", "arbitrary"),
        name="inproj",
    )(h, g, w_all, bf_row, gains, rope_tab)


def _compress_kernel(r0k_ref, r1k_ref, r0v_ref, r1v_ref, w1k_ref, w1v_ref,
                     pk_ref, pv_ref, w2k_ref, w2v_ref, gain_ref, kc_ref, vc_ref):
    def mlp(r0_ref, r1_ref, w1_ref, p_ref, w2_ref, kh):
        top, bot = w1_ref[kh, 0], w1_ref[kh, 1]
        bias = _dot(p_ref[0], top) + _dot(p_ref[1], bot)
        hid = _dot(r0_ref[0], top) + _dot(r1_ref[0], bot) + bias[0:1, :]
        hid = hid * jax.nn.sigmoid(hid)
        return _dot(hid.astype(BF16), w2_ref[...])

    for kh in range(N_KV_NSA):
        kc = mlp(r0k_ref, r1k_ref, w1k_ref, pk_ref, w2k_ref, kh)
        ms = jnp.sum(kc * kc, axis=-1, keepdims=True) * (1.0 / HEAD_DIM)
        kc_ref[0, kh] = (kc * lax.rsqrt(ms + EPS) * gain_ref[...]).astype(BF16)
        vc_ref[0, kh] = mlp(r0v_ref, r1v_ref, w1v_ref, pv_ref, w2v_ref, kh).astype(BF16)


def _compress(r0k, r1k, r0v, r1v, w1k, w1v, pk, pv, w2k, w2v, gain):
    b, r, w = r0k.shape
    rows = pl.BlockSpec((1, r, w), lambda i: (i, 0, 0))
    const = lambda a: pl.BlockSpec(a.shape, lambda i: (0,) * a.ndim)
    out = jax.ShapeDtypeStruct((b, N_KV_NSA, r, LANES), BF16)
    out_spec = pl.BlockSpec((1, N_KV_NSA, r, LANES), lambda i: (i, 0, 0, 0))
    return pl.pallas_call(
        _compress_kernel,
        grid=(b,),
        in_specs=[rows, rows, rows, rows, const(w1k), const(w1v), const(pk), const(pv),
                  const(w2k), const(w2v), const(gain)],
        out_specs=[out_spec, out_spec],
        out_shape=[out, out],
        compiler_params=_params("parallel"),
        name="compress",
    )(r0k, r1k, r0v, r1v, w1k, w1v, pk, pv, w2k, w2v, gain)


def _ones_rows(width):
    return jnp.ones((BF16_ROWS, width), BF16)


def _pipelined_sweep(n_full, n_tail, tk, scores, masked, consume, s_ref, mt_ref):
    def stash(s_list):
        for e, s in enumerate(s_list):
            s_ref[e] = s
            mt_ref[e] = jnp.max(s, axis=0, keepdims=True)

    def body(j, carry):
        k0 = pl.multiple_of(j * tk, tk)
        s_new = scores(k0)
        consume(pl.multiple_of(k0 - tk, tk))
        stash(s_new)
        return carry

    def tail_step(k0):
        s_new = scores(k0)
        consume(pl.multiple_of(k0 - tk, tk))
        stash(masked(s_new, k0))

    k_tail = pl.multiple_of(n_full * tk, tk)

    @pl.when(n_full == 0)
    def _():
        stash(masked(scores(k_tail), k_tail))

    @pl.when(n_full > 0)
    def _():
        stash(scores(0))
        lax.fori_loop(1, n_full, body, 0)
        tail_step(k_tail)

    for i in range(1, n_tail):
        tail_step(pl.multiple_of(k_tail + i * tk, tk))
    consume(pl.multiple_of(k_tail + (n_tail - 1) * tk, tk))


def _nsa_kernel(qt_ref, kc_ref, vct_ref, ksl_ref, vslt_ref, kwn_ref, vwnt_ref, gt_ref, ov_ref,
                o_ref, qaug_ref, m_ref, acc_ref, s_ref, mt_ref):
    qi = pl.program_id(2)
    tq = qt_ref.shape[2]
    s_len = ksl_ref.shape[1]
    n_cmp_rows = kc_ref.shape[2]
    n_sel = ov_ref.shape[0]
    top = min(SEL_TOPK, n_sel)
    grp = GROUP_NSA
    cols = grp * tq
    tk = min(NSA_TK, s_len)
    t0 = qi * tq

    def tok_of_col(rows):
        return t0 + (lax.broadcasted_iota(jnp.int32, (rows, cols), 1) & (tq - 1))

    def tok_of_tile(rows):
        return t0 + lax.broadcasted_iota(jnp.int32, (rows, tq), 1)

    def softmax_cols(s, keep):
        s = s + jnp.concatenate([jnp.where(keep, 0.0, NEG)] * grp, axis=1)
        mx = jnp.max(s, axis=0, keepdims=True)
        e = jnp.exp(s - mx)
        den = jnp.sum(e, axis=0, keepdims=True)
        return e, jnp.where(mx > 0.5 * NEG, 1.0 / den, 0.0)

    qt = qt_ref[0]
    qaug_ref[0:HEAD_DIM, :] = jnp.concatenate(
        [qt[g * HEAD_DIM:(g + 1) * HEAD_DIM] for g in range(grp)], axis=1)
    qaug_ref[HEAD_DIM:, :] = jnp.zeros((HEAD_DIM, cols), BF16)

    n_c = lax.broadcasted_iota(jnp.int32, (n_cmp_rows, tq), 0)
    keep_c = n_c * CMP_STRIDE + (CMP_BLOCK - 1) <= tok_of_tile(n_cmp_rows)
    e_c, inv_c = softmax_cols(_dot(kc_ref[0, 0], qaug_ref[...]), keep_c)
    o_c = _dot(vct_ref[0, 0], e_c.astype(BF16)) * inv_c

    p_sum = e_c[:, 0:tq] * inv_c[:, 0:tq]
    for g in range(1, grp):
        p_sum = p_sum + e_c[:, g * tq:(g + 1) * tq] * inv_c[:, g * tq:(g + 1) * tq]
    p_hi = p_sum.astype(BF16)
    p_lo = (p_sum - p_hi.astype(F32)).astype(BF16)
    ov = ov_ref[...]
    imp = _dot(ov, p_hi) + _dot(ov, p_lo)
    j_blk = lax.broadcasted_iota(jnp.int32, (n_sel, tq), 0)
    cur = jnp.right_shift(t0 + lax.broadcasted_iota(jnp.int32, (n_sel, tq), 1), SEL_SHIFT)
    forced = (j_blk == 0) | (j_blk == cur) | (j_blk == cur - 1)
    imp = jnp.where(j_blk <= cur, jnp.where(forced, FORCED_SCORE, imp), -1.0)
    sub = lax.broadcasted_iota(jnp.int32, (8, tq), 0)
    groups = [imp[8 * v:8 * v + 8] for v in range(n_sel // 8)]
    ranks = [jnp.zeros((8, tq), F32) for _ in groups]
    for i in range(n_sel):
        r_i = jnp.broadcast_to(imp[i:i + 1, :], (8, tq))
        for v, g_v in enumerate(groups):
            if i < 8 * v:
                beats = r_i >= g_v
            elif i >= 8 * v + 8:
                beats = r_i > g_v
            else:
                beats = (r_i > g_v) | ((r_i == g_v) & (sub > i - 8 * v))
            ranks[v] = ranks[v] + jnp.where(beats, 1.0, 0.0)
    chosen = (jnp.concatenate(ranks, axis=0) < top) & (imp >= 0.0)
    bias_t = jnp.where(chosen, 0.0, NEG)
    if n_sel < HEAD_DIM:
        bias_t = jnp.concatenate([bias_t, jnp.zeros((HEAD_DIM - n_sel, tq), F32)], axis=0)
    qaug_ref[HEAD_DIM:, :] = jnp.concatenate([bias_t.astype(BF16)] * grp, axis=1)

    m_ref[...] = jnp.full_like(m_ref, NEG)
    acc_ref[...] = jnp.zeros_like(acc_ref)

    def slc_scores(k0):
        return [_dot(ksl_ref[0, pl.ds(k0, tk), :], qaug_ref[...])]

    def slc_masked(s_list, k0):
        key = k0 + lax.broadcasted_iota(jnp.int32, (tk, cols), 0)
        return [jnp.where(key <= tok_of_col(tk), s_list[0], NEG)]

    def slc_consume(k0):
        m_old = m_ref[...]
        m_new = jnp.maximum(m_old, mt_ref[0])
        p = jnp.exp(s_ref[0] - m_new).astype(BF16)
        vt = jnp.concatenate([vslt_ref[0, :, pl.ds(k0, tk)], _ones_rows(tk)], axis=0)
        acc_ref[...] = jnp.exp(m_old - m_new) * acc_ref[...] + _dot(vt, p)
        m_ref[...] = m_new

    _pipelined_sweep(t0 // tk, max(1, tq // tk), tk, slc_scores, slc_masked, slc_consume, s_ref, mt_ref)
    o_s = acc_ref[0:HEAD_DIM, :] * (1.0 / acc_ref[HEAD_DIM:HEAD_DIM + 1, :])

    span = min(WINDOW + tq, s_len)
    start = pl.multiple_of(jnp.maximum(t0 + tq - span, 0), LANES)
    t_w = tok_of_tile(span)
    k_w = start + lax.broadcasted_iota(jnp.int32, (span, tq), 0)
    keep_w = (k_w <= t_w) & (t_w - k_w < WINDOW)
    e_w, inv_w = softmax_cols(_dot(kwn_ref[0, pl.ds(start, span), :], qaug_ref[...]), keep_w)
    o_w = _dot(vwnt_ref[0, :, pl.ds(start, span)], e_w.astype(BF16)) * inv_w

    gt = gt_ref[0]
    outs = []
    for g in range(grp):
        sl = slice(g * tq, (g + 1) * tq)
        outs.append(gt[3 * g:3 * g + 1] * o_c[:, sl] + gt[3 * g + 1:3 * g + 2] * o_s[:, sl]
                    + gt[3 * g + 2:3 * g + 3] * o_w[:, sl])
    o_ref[0] = jnp.concatenate(outs, axis=0)


def _nsa(qat, kcx, vct, kslx, vslt, kwnx, vwnt, gates_t, ov_t):
    b, _, s = qat.shape
    tq = min(NSA_TQ, s)
    assert tq & (tq - 1) == 0 and tq % LANES == 0
    r = kcx.shape[2]
    gw = GROUP_NSA * HEAD_DIM
    cols = GROUP_NSA * tq
    gate_rows = 16
    key_major = pl.BlockSpec((1, s, LANES), lambda i, k, j: (i, 0, k))
    val_major = pl.BlockSpec((1, HEAD_DIM, s), lambda i, k, j: (i, k, 0))
    return pl.pallas_call(
        _nsa_kernel,
        grid=(b, N_KV_NSA, s // tq),
        in_specs=[
            pl.BlockSpec((1, gw, tq), lambda i, k, j: (i, k, j)),
            pl.BlockSpec((1, 1, r, LANES), lambda i, k, j: (i, k, 0, 0)),
            pl.BlockSpec((1, 1, HEAD_DIM, r), lambda i, k, j: (i, k, 0, 0)),
            key_major, val_major, key_major, val_major,
            pl.BlockSpec((1, gate_rows, tq), lambda i, k, j: (i, k * (LANES // gate_rows), j)),
            pl.BlockSpec(ov_t.shape, lambda i, k, j: (0, 0)),
        ],
        out_specs=pl.BlockSpec((1, gw, tq), lambda i, k, j: (i, k, j)),
        out_shape=jax.ShapeDtypeStruct((b, NSA_Q_W, s), F32),
        scratch_shapes=[
            pltpu.VMEM((LANES, cols), BF16),
            pltpu.VMEM((1, cols), F32),
            pltpu.VMEM((HEAD_DIM + BF16_ROWS, cols), F32),
            pltpu.VMEM((1, min(NSA_TK, s), cols), F32),
            pltpu.VMEM((1, 1, cols), F32),
        ],
        compiler_params=_params("parallel", "parallel", "arbitrary"),
        name="nsa",
    )(qat, kcx, vct, kslx, vslt, kwnx, vwnt, gates_t, ov_t)


def _fox_kernel(qt_ref, k_ref, vt_ref, cq_ref, o_ref, qaug_ref, m_ref, acc_ref, s_ref, mt_ref):
    pair = pl.program_id(1)
    qi = pl.program_id(2)
    tq = qt_ref.shape[2]
    tk = min(FOX_TK, k_ref.shape[1])
    t0 = qi * tq
    qt = qt_ref[0]
    row = lax.broadcasted_iota(jnp.int32, (LANES, tq), 0)
    zero = jnp.zeros_like(qt)
    cq = []
    for e in range(2):
        qaug_ref[e, 0:LANES, :] = jnp.where((row >= e * HEAD_DIM) & (row < (e + 1) * HEAD_DIM), qt, zero)
        pick = (row >= e * AUG_STRIDE) & (row < e * AUG_STRIDE + 3)
        qaug_ref[e, LANES:, :] = jnp.where(pick, 1.0, 0.0).astype(BF16)
        cq.append(cq_ref[0, pl.ds(2 * pair + e, 1), :])

    m_ref[...] = jnp.full_like(m_ref, NEG)
    acc_ref[...] = jnp.zeros_like(acc_ref)

    def scores(k0):
        kt = k_ref[0, pl.ds(k0, tk), :]
        return [_dot(kt, qaug_ref[e]) for e in range(2)]

    def masked(s_list, k0):
        key = k0 + lax.broadcasted_iota(jnp.int32, (tk, tq), 0)
        tok = t0 + lax.broadcasted_iota(jnp.int32, (tk, tq), 1)
        return [jnp.where(key <= tok, sv, NEG) for sv in s_list]

    def consume(k0):
        for e in range(2):
            m_old = m_ref[e]
            m_new = jnp.maximum(m_old, mt_ref[e] + cq[e])
            p = jnp.exp(s_ref[e] + (cq[e] - m_new)).astype(BF16)
            vt = jnp.concatenate([vt_ref[0, e * HEAD_DIM:(e + 1) * HEAD_DIM, pl.ds(k0, tk)],
                                  _ones_rows(tk)], axis=0)
            acc_ref[e] = jnp.exp(m_old - m_new) * acc_ref[e] + _dot(vt, p)
            m_ref[e] = m_new

    _pipelined_sweep(t0 // tk, max(1, tq // tk), tk, scores, masked, consume, s_ref, mt_ref)
    o_ref[0] = jnp.concatenate(
        [acc_ref[e, 0:HEAD_DIM, :] * (1.0 / acc_ref[e, HEAD_DIM:HEAD_DIM + 1, :]) for e in range(2)], axis=0)


def _fox(fqt, fkx, fvt, cq):
    b, _, s = fqt.shape
    tq = min(FOX_TQ, s)
    return pl.pallas_call(
        _fox_kernel,
        grid=(b, N_HEADS_FOX // 2, s // tq),
        in_specs=[
            pl.BlockSpec((1, LANES, tq), lambda i, p, j: (i, p, j)),
            pl.BlockSpec((1, s, 2 * LANES), lambda i, p, j: (i, 0, p)),
            pl.BlockSpec((1, LANES, s), lambda i, p, j: (i, p, 0)),
            pl.BlockSpec((1, N_HEADS_FOX, tq), lambda i, p, j: (i, 0, j)),
        ],
        out_specs=pl.BlockSpec((1, LANES, tq), lambda i, p, j: (i, p, j)),
        out_shape=jax.ShapeDtypeStruct((b, FOX_W, s), F32),
        scratch_shapes=[
            pltpu.VMEM((2, 2 * LANES, tq), BF16),
            pltpu.VMEM((2, 1, tq), F32),
            pltpu.VMEM((2, HEAD_DIM + BF16_ROWS, tq), F32),
            pltpu.VMEM((2, min(FOX_TK, s), tq), F32),
            pltpu.VMEM((2, 1, tq), F32),
        ],
        compiler_params=_params("parallel", "parallel", "arbitrary"),
        name="fox",
    )(fqt, fkx, fvt, cq)


def _outproj_kernel(h_ref, oa_ref, ob_ref, ga_ref, gb_ref, wa_ref, wb_ref, o_ref):
    oa = oa_ref[...]
    ob = ob_ref[...]
    na = (oa * _rms(oa) * ga_ref[...]).astype(BF16)
    nb = (ob * _rms(ob) * gb_ref[...]).astype(BF16)
    o_ref[...] = h_ref[...] + _dot(na, wa_ref[...]) + _dot(nb, wb_ref[...])


def _outproj(h, oa, ob, ga, gb, wa, wb):
    t, d = h.shape
    tm = min(OUT_TM, t)
    rows = lambda c: pl.BlockSpec((tm, c), lambda i: (i, 0))
    const = lambda a: pl.BlockSpec(a.shape, lambda i: (0,) * a.ndim)
    return pl.pallas_call(
        _outproj_kernel,
        grid=(t // tm,),
        in_specs=[rows(d), rows(NSA_Q_W), rows(FOX_W), const(ga), const(gb), const(wa), const(wb)],
        out_specs=rows(d),
        out_shape=jax.ShapeDtypeStruct((t, d), F32),
        compiler_params=_params("parallel"),
        name="outproj",
    )(h, oa, ob, ga, gb, wa, wb)


def _ple_kernel(h_ref, p_ref, gg_ref, gp_ref, wg_ref, wp_ref, o_ref):
    h = h_ref[...]
    hn = (h * _rms(h) * gg_ref[...]).astype(BF16)
    gate = jax.nn.sigmoid(_dot(hn, wg_ref[...]))
    e = _dot(p_ref[...].astype(BF16), wp_ref[...])
    o_ref[...] = h + gate * (e * _rms(e) * gp_ref[...])


def _ple(h, p, gg, gp, wg, wp):
    t, d = h.shape
    tm = min(PLE_TM, t)
    rows = lambda c: pl.BlockSpec((tm, c), lambda i: (i, 0))
    const = lambda a: pl.BlockSpec(a.shape, lambda i: (0,) * a.ndim)
    return pl.pallas_call(
        _ple_kernel,
        grid=(t // tm,),
        in_specs=[rows(d), rows(p.shape[1]), const(gg), const(gp), const(wg), const(wp)],
        out_specs=rows(d),
        out_shape=jax.ShapeDtypeStruct((t, d), F32),
        compiler_params=_params("parallel"),
        name="ple",
    )(h, p, gg, gp, wg, wp)


def _expand_heads(w, n):
    w = w.reshape(w.shape[0], n, HEAD_DIM)
    return jnp.concatenate([w, jnp.zeros_like(w)], axis=-1).reshape(w.shape[0], n * LANES)


def _pad_cols(w, width):
    return jnp.pad(w, ((0, 0), (0, width - w.shape[1])))


def _layout_w_in(w_in):
    splits = [int(v) for v in np.cumsum(PROJ_SIZES)[:-1]]
    qa, kc, vc, ksl, vsl, kwn, vwn, ga, qf, kf, vf, fl = jnp.split(w_in, splits, axis=-1)
    per_group = 3 * GROUP_NSA
    ga_x = jnp.concatenate([_pad_cols(ga[:, k * per_group:(k + 1) * per_group], LANES)
                            for k in range(N_KV_NSA)], axis=-1)
    cols = [qa, kc, vc, _expand_heads(ksl, N_KV_NSA), vsl, _expand_heads(kwn, N_KV_NSA), vwn,
            ga_x, qf, kf, vf, _pad_cols(fl, LANES)]
    return jnp.concatenate(cols, axis=-1).astype(BF16)


def _rope_tables(seq):
    pos = jnp.arange(seq, dtype=F32)
    inv = ROPE_THETA ** (-jnp.arange(0, ROT_DIM, 2, dtype=F32) / ROT_DIM)
    ang = pos[:, None] * inv[None, :]
    cos, sin = jnp.cos(ang), jnp.sin(ang)
    half = ROT_DIM // 2
    rest = HEAD_DIM - ROT_DIM
    ones = jnp.ones((seq, rest), F32)
    zeros = jnp.zeros((seq, rest), F32)
    zh = jnp.zeros((seq, half), F32)
    cos_t = jnp.concatenate([cos, cos, ones], axis=-1)
    sin_lo = jnp.concatenate([-sin, zh, zeros], axis=-1)
    sin_hi = jnp.concatenate([zh, sin, zeros], axis=-1)
    tile2 = lambda t: jnp.concatenate([t, t], axis=-1)
    return jnp.stack([tile2(cos_t), tile2(sin_lo), tile2(sin_hi)])


def _layout_cmp_w1(w1):
    hid = w1.shape[1]
    w = w1.reshape(2, CMP_STRIDE, HEAD_DIM, hid)
    z = jnp.zeros_like(w)
    per_head = [jnp.concatenate([w, z], axis=2), jnp.concatenate([z, w], axis=2)]
    return jnp.stack(per_head).reshape(N_KV_NSA, 2, CMP_STRIDE * LANES, hid).astype(BF16)


def _layout_cmp_pos(pos):
    p = pos.reshape(2, CMP_STRIDE, HEAD_DIM)
    p = jnp.concatenate([p, p], axis=-1).reshape(2, 1, CMP_STRIDE * LANES)
    return jnp.broadcast_to(p, (2, 8, CMP_STRIDE * LANES)).astype(BF16)


def _overlap_t(seq):
    rows = seq // CMP_STRIDE
    n_sel = seq // SEL_BLOCK
    cmp_start = np.arange(rows) * CMP_STRIDE
    sel_start = np.arange(n_sel) * SEL_BLOCK
    ov = ((cmp_start[None, :] <= sel_start[:, None] + SEL_BLOCK - 1)
          & (cmp_start[None, :] + CMP_BLOCK - 1 >= sel_start[:, None]))
    ov[:, (seq - CMP_BLOCK) // CMP_STRIDE + 1:] = False
    return jnp.asarray(ov.astype(np.float32), BF16)


def _row(v, width=None):
    v = v.reshape(1, -1).astype(F32)
    return v if width is None else _pad_cols(v, width)


def _feature_major(a):
    return jnp.transpose(a, (0, 2, 1))


def kernel(x, p, ffn1_norm, ffn1_wg, ffn1_wu, ffn1_wd, mix_norm, w_in, b_forget, q_norm_nsa, k_norm_cmp, k_norm_slc, k_norm_win, cmp_pos_k, cmp_pos_v, cmp_k_w1, cmp_k_w2, cmp_v_w1, cmp_v_w2, q_norm_fox, k_norm_fox, out_norm_nsa, out_norm_fox, w_out, ffn2_norm, ffn2_wg, ffn2_wu, ffn2_wd, ple_gate_norm, ple_w_gate, ple_w_proj, ple_norm):
    b, s, d = x.shape
    assert s // SEL_BLOCK <= HEAD_DIM and (s // SEL_BLOCK) % 8 == 0
    assert s % max(NSA_TK, FOX_TK, FOX_TQ) == 0
    depth = ffn1_norm.shape[0]
    t = b * s
    rope_tab = _rope_tables(s)
    ov_t = _overlap_t(s)
    tile2 = lambda g: jnp.concatenate([g, g]).reshape(1, LANES).astype(F32)
    h = x.reshape(t, d)
    for i in range(depth):
        h = _ffn(h, _row(ffn1_norm[i]), ffn1_wg[i].astype(BF16), ffn1_wu[i].astype(BF16),
                 ffn1_wd[i].astype(BF16))

        gains = jnp.concatenate([tile2(q_norm_nsa[i]), tile2(k_norm_slc[i]), tile2(k_norm_win[i]),
                                 tile2(q_norm_fox[i]), tile2(k_norm_fox[i]),
                                 jnp.zeros((3, LANES), F32)], axis=0)
        (qa, kc, vc, kslx, vsl, kwnx, vwn, gates, fq, fkx, fv, cum) = _inproj(
            h.reshape(b, s, d), _row(mix_norm[i]), _layout_w_in(w_in[i]),
            _row(b_forget[i], LANES), gains, rope_tab)

        rows = s // CMP_STRIDE

        def shifted(tok):
            r0 = tok.reshape(b, rows, CMP_STRIDE * LANES)
            r1 = jnp.concatenate([r0[:, 1:], jnp.zeros_like(r0[:, :1])], axis=1)
            return r0, r1

        r0k, r1k = shifted(kc)
        r0v, r1v = shifted(vc)
        kcx, vcx = _compress(
            r0k, r1k, r0v, r1v, _layout_cmp_w1(cmp_k_w1[i]), _layout_cmp_w1(cmp_v_w1[i]),
            _layout_cmp_pos(cmp_pos_k[i]), _layout_cmp_pos(cmp_pos_v[i]),
            _pad_cols(cmp_k_w2[i], LANES).astype(BF16), _pad_cols(cmp_v_w2[i], LANES).astype(BF16),
            _row(k_norm_cmp[i], LANES))
        vct = jnp.transpose(vcx[..., :HEAD_DIM], (0, 1, 3, 2))

        o_at = _nsa(_feature_major(qa), kcx, vct, kslx, _feature_major(vsl), kwnx,
                    _feature_major(vwn), _feature_major(gates), ov_t)
        o_bt = _fox(_feature_major(fq), fkx, _feature_major(fv),
                    _feature_major(cum[:, :, :N_HEADS_FOX]))

        w_o = w_out[i].astype(BF16)
        h = _outproj(h, _feature_major(o_at).reshape(t, NSA_Q_W), _feature_major(o_bt).reshape(t, FOX_W),
                     _row(out_norm_nsa[i]), _row(out_norm_fox[i]), w_o[:NSA_Q_W], w_o[NSA_Q_W:])

        h = _ffn(h, _row(ffn2_norm[i]), ffn2_wg[i].astype(BF16), ffn2_wu[i].astype(BF16),
                 ffn2_wd[i].astype(BF16))

        h = _ple(h, p[i].reshape(t, -1), _row(ple_gate_norm[i]), _row(ple_norm[i]),
                 ple_w_gate[i].astype(BF16), ple_w_proj[i].astype(BF16))
    return h.reshape(b, s, d)
```

```python
import numpy as np
import jax
import jax.numpy as jnp
from jax import lax
from jax.experimental import pallas as pl
from jax.experimental.pallas import tpu as pltpu

F32 = jnp.float32
BF16 = jnp.bfloat16

D_MODEL = 1024
HEAD_DIM = 64
N_HEADS_NSA = 8
N_KV_NSA = 2
GROUP_NSA = N_HEADS_NSA // N_KV_NSA
N_HEADS_FOX = 8
NSA_Q_W = N_HEADS_NSA * HEAD_DIM
NSA_KV_W = N_KV_NSA * HEAD_DIM
FOX_W = N_HEADS_FOX * HEAD_DIM
PROJ_SIZES = (NSA_Q_W, NSA_KV_W, NSA_KV_W, NSA_KV_W, NSA_KV_W, NSA_KV_W, NSA_KV_W,
              3 * N_HEADS_NSA, FOX_W, FOX_W, FOX_W, N_HEADS_FOX)
D_FF = 2816
D_PLE = 256
ROPE_THETA = 500000.0
ROT_DIM = HEAD_DIM // 4
CMP_BLOCK = 32
CMP_STRIDE = 16
CMP_HIDDEN = 256
SEL_BLOCK = 64
SEL_SHIFT = 6
SEL_TOPK = 16
WINDOW = 512
FORCED_SCORE = 1e6
EPS = 1e-6
NEG = -1e30
LOG2E = 1.4426950408889634

LANES = 128
BF16_ROWS = 16
VMEM_LIMIT = 56 * 1024 * 1024
AUG_STRIDE = 8

C_QA = 0
C_KCVC = C_QA + NSA_Q_W
C_KSL = C_KCVC + 2 * LANES
C_KWN = C_KSL + N_KV_NSA * LANES
C_VSLWN = C_KWN + N_KV_NSA * LANES
C_GATE = C_VSLWN + 2 * LANES
C_FQ = C_GATE + N_KV_NSA * LANES
C_FK = C_FQ + FOX_W
C_FV = C_FK + FOX_W
C_FL = C_FV + FOX_W
C_END = C_FL + 2 * LANES

FFN_TM = 1024
FFN_TF = 256
PROJ_TS = 512
NSA_TQ = 256
NSA_TK = 256
FOX_TQ = 512
FOX_TK = 256
OUT_TM = 512
PLE_TM = 512


def _dot(a, b):
    return jnp.dot(a, b, preferred_element_type=F32)


def _dot_nt(a, b):
    return lax.dot_general(a, b, (((1,), (1,)), ((), ())), preferred_element_type=F32)


def _params(*sem):
    return pltpu.CompilerParams(dimension_semantics=sem, vmem_limit_bytes=VMEM_LIMIT)


def _rms(x):
    return lax.rsqrt(jnp.mean(x * x, axis=-1, keepdims=True) + EPS)


def _split3(x):
    hi = x.astype(BF16)
    r1 = x - hi.astype(F32)
    mid = r1.astype(BF16)
    lo = (r1 - mid.astype(F32)).astype(BF16)
    return hi, mid, lo


def _ffn_kernel(x_ref, g_ref, wg_ref, wu_ref, wd_ref, o_ref, xn_ref, acc_ref):
    f = pl.program_id(1)

    @pl.when(f == 0)
    def _():
        x = x_ref[...]
        xn_ref[...] = (x * _rms(x) * g_ref[...]).astype(BF16)
        acc_ref[...] = jnp.zeros_like(acc_ref)

    xn = xn_ref[...]
    gate = _dot(xn, wg_ref[...])
    up = _dot(xn, wu_ref[...])
    act = (gate * jax.nn.sigmoid(gate) * up).astype(BF16)
    acc_ref[...] += _dot(act, wd_ref[...])

    @pl.when(f == pl.num_programs(1) - 1)
    def _():
        o_ref[...] = x_ref[...] + 0.5 * acc_ref[...]


def _ffn(x, g, wg, wu, wd):
    t, d = x.shape
    f = wg.shape[1]
    tm, tf = min(FFN_TM, t), FFN_TF
    return pl.pallas_call(
        _ffn_kernel,
        grid=(t // tm, f // tf),
        in_specs=[
            pl.BlockSpec((tm, d), lambda i, j: (i, 0)),
            pl.BlockSpec((1, d), lambda i, j: (0, 0)),
            pl.BlockSpec((d, tf), lambda i, j: (0, j)),
            pl.BlockSpec((d, tf), lambda i, j: (0, j)),
            pl.BlockSpec((tf, d), lambda i, j: (j, 0)),
        ],
        out_specs=pl.BlockSpec((tm, d), lambda i, j: (i, 0)),
        out_shape=jax.ShapeDtypeStruct((t, d), F32),
        scratch_shapes=[pltpu.VMEM((tm, d), BF16), pltpu.VMEM((tm, d), F32)],
        compiler_params=_params("parallel", "arbitrary"),
        name="ffn",
    )(x, g, wg, wu, wd)


def _log_sigmoid(x):
    return -(jnp.maximum(-x, 0.0) + jnp.log(1.0 + jnp.exp(-jnp.abs(x))))


def _inproj_kernel(h_ref, g_ref, w_ref, bf_ref, gains_ref, rope_ref,
                   qa_ref, kc_ref, vc_ref, ksl_ref, vsl_ref, kwn_ref, vwn_ref,
                   gate_ref, fq_ref, fk_ref, fv_ref, cum_ref, carry_ref):
    si = pl.program_id(1)
    ts = h_ref.shape[1]
    wide = 2 * LANES
    x = h_ref[0]
    a = (x * _rms(x) * g_ref[...]).astype(BF16)

    row = lax.broadcasted_iota(jnp.int32, (2 * wide, wide), 0) & (wide - 1)
    col = lax.broadcasted_iota(jnp.int32, (2 * wide, wide), 1)
    ones_head = jnp.where(jnp.right_shift(row, 6) == jnp.right_shift(col, 6), 1.0, 0.0).astype(BF16)
    ones_chunk = jnp.where(jnp.right_shift(row, 7) == jnp.right_shift(col, 7), 1.0, 0.0).astype(BF16)
    cos_t, sin_lo, sin_hi = rope_ref[0], rope_ref[1], rope_ref[2]

    def proj(c0):
        return _dot(a, w_ref[:, c0:c0 + wide])

    def norm(u, ones_mat, gain_row):
        u2 = u * u
        hi = u2.astype(BF16)
        lo = (u2 - hi.astype(F32)).astype(BF16)
        ms = _dot(jnp.concatenate([hi, lo], axis=1), ones_mat) * (1.0 / HEAD_DIM)
        return u * lax.rsqrt(ms + EPS) * gains_ref[gain_row:gain_row + 1, :]

    def rope_half(u):
        return (u * cos_t + pltpu.roll(u, LANES - ROT_DIM // 2, 1) * sin_lo
                + pltpu.roll(u, ROT_DIM // 2, 1) * sin_hi)

    def rope(u):
        return jnp.concatenate([rope_half(u[:, :LANES]), rope_half(u[:, LANES:])], axis=1)

    scale = HEAD_DIM ** -0.5 * LOG2E
    for c in range(NSA_Q_W // wide):
        sl = slice(c * wide, (c + 1) * wide)
        qa_ref[0, :, sl] = (rope(norm(proj(C_QA + c * wide), ones_head, 0)) * scale).astype(BF16)

    u = proj(C_KCVC)
    kc_ref[0] = rope_half(u[:, :LANES]).astype(BF16)
    vc_ref[0] = u[:, LANES:].astype(BF16)
    u = proj(C_VSLWN)
    vsl_ref[0] = u[:, :LANES].astype(BF16)
    vwn_ref[0] = u[:, LANES:].astype(BF16)

    pos = si * ts + lax.broadcasted_iota(jnp.int32, (ts, wide), 0)
    lane = lax.broadcasted_iota(jnp.int32, (ts, wide), 1) & (LANES - 1)
    sel_onehot = jnp.where(lane == HEAD_DIM + jnp.right_shift(pos, SEL_SHIFT), 1.0, 0.0)
    ksl_ref[0] = (rope(norm(proj(C_KSL), ones_chunk, 1)) + sel_onehot).astype(BF16)
    kwn_ref[0] = rope(norm(proj(C_KWN), ones_chunk, 2)).astype(BF16)
    gate_ref[0] = jax.nn.sigmoid(proj(C_GATE))

    @pl.when(si == 0)
    def _():
        carry_ref[...] = jnp.zeros_like(carry_ref)

    lf = _log_sigmoid(proj(C_FL)[:, :LANES] + bf_ref[...])
    r_i = lax.broadcasted_iota(jnp.int32, (ts, ts), 0)
    c_i = lax.broadcasted_iota(jnp.int32, (ts, ts), 1)
    tri = jnp.where(r_i >= c_i, 1.0, 0.0).astype(BF16)
    hi, mid, lo = _split3(lf)
    c = _dot(tri, hi) + _dot(tri, mid) + _dot(tri, lo) + carry_ref[0:1, :]
    carry_ref[...] = jnp.broadcast_to(c[ts - 1:ts, :], carry_ref.shape)
    c = c * LOG2E
    cum_ref[0] = c

    n_pairs = N_HEADS_FOX // 2
    p_row = lax.broadcasted_iota(jnp.int32, (3 * LANES, n_pairs * LANES), 0)
    p_col = lax.broadcasted_iota(jnp.int32, (3 * LANES, n_pairs * LANES), 1)
    head = p_row & (LANES - 1)
    part = jnp.right_shift(p_row, 7)
    target = jnp.right_shift(head, 1) * LANES + (head & 1) * AUG_STRIDE + part
    place = jnp.where((p_col == target) & (head < N_HEADS_FOX), 1.0, 0.0).astype(BF16)
    aug = _dot(jnp.concatenate(_split3(-c), axis=1), place)

    for c2 in range(FOX_W // wide):
        sl = slice(c2 * wide, (c2 + 1) * wide)
        fq_ref[0, :, sl] = (norm(proj(C_FQ + c2 * wide), ones_head, 3) * scale).astype(BF16)
        fv_ref[0, :, sl] = proj(C_FV + c2 * wide).astype(BF16)
        fk = norm(proj(C_FK + c2 * wide), ones_head, 4).astype(BF16)
        for e in range(2):
            pair = 2 * c2 + e
            fk_ref[0, :, 2 * pair * LANES:(2 * pair + 1) * LANES] = fk[:, e * LANES:(e + 1) * LANES]
            fk_ref[0, :, (2 * pair + 1) * LANES:(2 * pair + 2) * LANES] = (
                aug[:, pair * LANES:(pair + 1) * LANES].astype(BF16))


def _inproj(h, g, w_all, bf_row, gains, rope_tab):
    b, s, d = h.shape
    ts = min(PROJ_TS, s)
    tok = lambda c: pl.BlockSpec((1, ts, c), lambda i, j: (i, j, 0))
    const = lambda shape: pl.BlockSpec(shape, lambda i, j: (0,) * len(shape))
    kvx = N_KV_NSA * LANES
    specs = [(NSA_Q_W, BF16), (LANES, BF16), (LANES, BF16), (kvx, BF16), (LANES, BF16),
             (kvx, BF16), (LANES, BF16), (kvx, F32), (FOX_W, BF16), (2 * FOX_W, BF16),
             (FOX_W, BF16), (LANES, F32)]
    out_shape = [jax.ShapeDtypeStruct((b, s, c), dt) for c, dt in specs]
    return pl.pallas_call(
        _inproj_kernel,
        grid=(b, s // ts),
        in_specs=[
            tok(d),
            const((1, d)),
            const((d, C_END)),
            const((1, LANES)),
            const((8, 2 * LANES)),
            pl.BlockSpec((3, ts, LANES), lambda i, j: (0, j, 0)),
        ],
        out_specs=[tok(c) for c, _ in specs],
        out_shape=out_shape,
        scratch_shapes=[pltpu.VMEM((8, LANES), F32)],
        compiler_params=_params("parallel", "arbitrary"),
        name="inproj",
    )(h, g, w_all, bf_row, gains, rope_tab)


def _compress_kernel(r0k_ref, r1k_ref, r0v_ref, r1v_ref, w1k_ref, w1v_ref,
                     pk_ref, pv_ref, w2k_ref, w2v_ref, gain_ref, kc_ref, vc_ref):
    def mlp(r0_ref, r1_ref, w1_ref, p_ref, w2_ref, kh):
        top, bot = w1_ref[kh, 0], w1_ref[kh, 1]
        bias = _dot(p_ref[0], top) + _dot(p_ref[1], bot)
        hid = _dot(r0_ref[0], top) + _dot(r1_ref[0], bot) + bias[0:1, :]
        hid = hid * jax.nn.sigmoid(hid)
        return _dot(hid.astype(BF16), w2_ref[...])

    for kh in range(N_KV_NSA):
        kc = mlp(r0k_ref, r1k_ref, w1k_ref, pk_ref, w2k_ref, kh)
        ms = jnp.sum(kc * kc, axis=-1, keepdims=True) * (1.0 / HEAD_DIM)
        kc_ref[0, kh] = (kc * lax.rsqrt(ms + EPS) * gain_ref[...]).astype(BF16)
        vc_ref[0, kh] = mlp(r0v_ref, r1v_ref, w1v_ref, pv_ref, w2v_ref, kh).astype(BF16)


def _compress(r0k, r1k, r0v, r1v, w1k, w1v, pk, pv, w2k, w2v, gain):
    b, r, w = r0k.shape
    rows = pl.BlockSpec((1, r, w), lambda i: (i, 0, 0))
    const = lambda a: pl.BlockSpec(a.shape, lambda i: (0,) * a.ndim)
    out = jax.ShapeDtypeStruct((b, N_KV_NSA, r, LANES), BF16)
    out_spec = pl.BlockSpec((1, N_KV_NSA, r, LANES), lambda i: (i, 0, 0, 0))
    return pl.pallas_call(
        _compress_kernel,
        grid=(b,),
        in_specs=[rows, rows, rows, rows, const(w1k), const(w1v), const(pk), const(pv),
                  const(w2k), const(w2v), const(gain)],
        out_specs=[out_spec, out_spec],
        out_shape=[out, out],
        compiler_params=_params("parallel"),
        name="compress",
    )(r0k, r1k, r0v, r1v, w1k, w1v, pk, pv, w2k, w2v, gain)


def _ones_rows(width):
    return jnp.ones((BF16_ROWS, width), BF16)


def _pipelined_sweep(n_full, n_tail, tk, scores, masked, consume, s_ref, mt_ref):
    def stash(s_list):
        for e, s in enumerate(s_list):
            s_ref[e] = s
            mt_ref[e] = jnp.max(s, axis=0, keepdims=True)

    def body(j, carry):
        k0 = pl.multiple_of(j * tk, tk)
        s_new = scores(k0)
        consume(pl.multiple_of(k0 - tk, tk))
        stash(s_new)
        return carry

    def tail_step(k0):
        s_new = scores(k0)
        consume(pl.multiple_of(k0 - tk, tk))
        stash(masked(s_new, k0))

    k_tail = pl.multiple_of(n_full * tk, tk)

    @pl.when(n_full == 0)
    def _():
        stash(masked(scores(k_tail), k_tail))

    @pl.when(n_full > 0)
    def _():
        stash(scores(0))
        lax.fori_loop(1, n_full, body, 0)
        tail_step(k_tail)

    for i in range(1, n_tail):
        tail_step(pl.multiple_of(k_tail + i * tk, tk))
    consume(pl.multiple_of(k_tail + (n_tail - 1) * tk, tk))


def _nsa_kernel(qt_ref, kc_ref, vct_ref, ksl_ref, vslt_ref, kwn_ref, vwnt_ref, gt_ref, ov_ref,
                o_ref, qaug_ref, m_ref, acc_ref, s_ref, mt_ref):
    qi = pl.program_id(2)
    tq = qt_ref.shape[2]
    s_len = ksl_ref.shape[1]
    n_cmp_rows = kc_ref.shape[2]
    n_sel = ov_ref.shape[0]
    top = min(SEL_TOPK, n_sel)
    grp = GROUP_NSA
    cols = grp * tq
    tk = min(NSA_TK, s_len)
    t0 = qi * tq

    def tok_of_col(rows):
        return t0 + (lax.broadcasted_iota(jnp.int32, (rows, cols), 1) & (tq - 1))

    def tok_of_tile(rows):
        return t0 + lax.broadcasted_iota(jnp.int32, (rows, tq), 1)

    def softmax_cols(s, keep):
        s = s + jnp.concatenate([jnp.where(keep, 0.0, NEG)] * grp, axis=1)
        mx = jnp.max(s, axis=0, keepdims=True)
        e = jnp.exp2(s - mx)
        den = jnp.sum(e, axis=0, keepdims=True)
        return e, jnp.where(mx > 0.5 * NEG, 1.0 / den, 0.0)

    qt = qt_ref[0]
    qaug_ref[0:HEAD_DIM, :] = jnp.concatenate(
        [qt[g * HEAD_DIM:(g + 1) * HEAD_DIM] for g in range(grp)], axis=1)
    qaug_ref[HEAD_DIM:, :] = jnp.zeros((HEAD_DIM, cols), BF16)

    n_c = lax.broadcasted_iota(jnp.int32, (n_cmp_rows, tq), 0)
    keep_c = n_c * CMP_STRIDE + (CMP_BLOCK - 1) <= tok_of_tile(n_cmp_rows)
    e_c, inv_c = softmax_cols(_dot(kc_ref[0, 0], qaug_ref[...]), keep_c)
    o_c = _dot(vct_ref[0, 0], e_c.astype(BF16)) * inv_c

    p_sum = e_c[:, 0:tq] * inv_c[:, 0:tq]
    for g in range(1, grp):
        p_sum = p_sum + e_c[:, g * tq:(g + 1) * tq] * inv_c[:, g * tq:(g + 1) * tq]
    p_hi = p_sum.astype(BF16)
    p_lo = (p_sum - p_hi.astype(F32)).astype(BF16)
    ov = ov_ref[...]
    imp = _dot(ov, p_hi) + _dot(ov, p_lo)
    j_blk = lax.broadcasted_iota(jnp.int32, (n_sel, tq), 0)
    cur = jnp.right_shift(t0 + lax.broadcasted_iota(jnp.int32, (n_sel, tq), 1), SEL_SHIFT)
    forced = (j_blk == 0) | (j_blk == cur) | (j_blk == cur - 1)
    imp = jnp.where(j_blk <= cur, jnp.where(forced, FORCED_SCORE, imp), -1.0)
    sub = lax.broadcasted_iota(jnp.int32, (8, tq), 0)
    groups = [imp[8 * v:8 * v + 8] for v in range(n_sel // 8)]
    ranks = [jnp.zeros((8, tq), F32) for _ in groups]
    for i in range(n_sel):
        r_i = jnp.broadcast_to(imp[i:i + 1, :], (8, tq))
        for v, g_v in enumerate(groups):
            if i < 8 * v:
                beats = r_i >= g_v
            elif i >= 8 * v + 8:
                beats = r_i > g_v
            else:
                beats = (r_i > g_v) | ((r_i == g_v) & (sub > i - 8 * v))
            ranks[v] = ranks[v] + jnp.where(beats, 1.0, 0.0)
    chosen = (jnp.concatenate(ranks, axis=0) < top) & (imp >= 0.0)
    bias_t = jnp.where(chosen, 0.0, NEG)
    if n_sel < HEAD_DIM:
        bias_t = jnp.concatenate([bias_t, jnp.zeros((HEAD_DIM - n_sel, tq), F32)], axis=0)
    qaug_ref[HEAD_DIM:, :] = jnp.concatenate([bias_t.astype(BF16)] * grp, axis=1)

    m_ref[...] = jnp.full_like(m_ref, NEG)
    acc_ref[...] = jnp.zeros_like(acc_ref)

    def slc_scores(k0):
        return [_dot(ksl_ref[0, pl.ds(k0, tk), :], qaug_ref[...])]

    def slc_masked(s_list, k0):
        key = k0 + lax.broadcasted_iota(jnp.int32, (tk, tq), 0)
        bias = jnp.where(key <= tok_of_tile(tk), 0.0, NEG)
        return [s_list[0] + jnp.concatenate([bias] * grp, axis=1)]

    def slc_consume(k0):
        m_old = m_ref[...]
        m_new = jnp.maximum(m_old, mt_ref[0])
        p = jnp.exp2(s_ref[0] - m_new).astype(BF16)
        vt = jnp.concatenate([vslt_ref[0, :, pl.ds(k0, tk)], _ones_rows(tk)], axis=0)
        acc_ref[...] = jnp.exp2(m_old - m_new) * acc_ref[...] + _dot(vt, p)
        m_ref[...] = m_new

    _pipelined_sweep(t0 // tk, max(1, tq // tk), tk, slc_scores, slc_masked, slc_consume, s_ref, mt_ref)
    o_s = acc_ref[0:HEAD_DIM, :] * (1.0 / acc_ref[HEAD_DIM:HEAD_DIM + 1, :])

    span = min(WINDOW + tq, s_len)
    start = pl.multiple_of(jnp.maximum(t0 + tq - span, 0), LANES)
    t_w = tok_of_tile(span)
    k_w = start + lax.broadcasted_iota(jnp.int32, (span, tq), 0)
    keep_w = (k_w <= t_w) & (t_w - k_w < WINDOW)
    e_w, inv_w = softmax_cols(_dot(kwn_ref[0, pl.ds(start, span), :], qaug_ref[...]), keep_w)
    o_w = _dot(vwnt_ref[0, :, pl.ds(start, span)], e_w.astype(BF16)) * inv_w

    gt = gt_ref[0]
    outs = []
    for g in range(grp):
        sl = slice(g * tq, (g + 1) * tq)
        outs.append(gt[3 * g:3 * g + 1] * o_c[:, sl] + gt[3 * g + 1:3 * g + 2] * o_s[:, sl]
                    + gt[3 * g + 2:3 * g + 3] * o_w[:, sl])
    o_ref[0] = jnp.concatenate(outs, axis=0)


def _nsa(qat, kcx, vct, kslx, vslt, kwnx, vwnt, gates_t, ov_t):
    b, _, s = qat.shape
    tq = min(NSA_TQ, s)
    assert tq & (tq - 1) == 0 and tq % LANES == 0
    r = kcx.shape[2]
    gw = GROUP_NSA * HEAD_DIM
    cols = GROUP_NSA * tq
    gate_rows = 16
    key_major = pl.BlockSpec((1, s, LANES), lambda i, k, j: (i, 0, k))
    val_major = pl.BlockSpec((1, HEAD_DIM, s), lambda i, k, j: (i, k, 0))
    return pl.pallas_call(
        _nsa_kernel,
        grid=(b, N_KV_NSA, s // tq),
        in_specs=[
            pl.BlockSpec((1, gw, tq), lambda i, k, j: (i, k, j)),
            pl.BlockSpec((1, 1, r, LANES), lambda i, k, j: (i, k, 0, 0)),
            pl.BlockSpec((1, 1, HEAD_DIM, r), lambda i, k, j: (i, k, 0, 0)),
            key_major, val_major, key_major, val_major,
            pl.BlockSpec((1, gate_rows, tq), lambda i, k, j: (i, k * (LANES // gate_rows), j)),
            pl.BlockSpec(ov_t.shape, lambda i, k, j: (0, 0)),
        ],
        out_specs=pl.BlockSpec((1, gw, tq), lambda i, k, j: (i, k, j)),
        out_shape=jax.ShapeDtypeStruct((b, NSA_Q_W, s), F32),
        scratch_shapes=[
            pltpu.VMEM((LANES, cols), BF16),
            pltpu.VMEM((1, cols), F32),
            pltpu.VMEM((HEAD_DIM + BF16_ROWS, cols), F32),
            pltpu.VMEM((1, min(NSA_TK, s), cols), F32),
            pltpu.VMEM((1, 1, cols), F32),
        ],
        compiler_params=_params("parallel", "parallel", "arbitrary"),
        name="nsa",
    )(qat, kcx, vct, kslx, vslt, kwnx, vwnt, gates_t, ov_t)


def _fox_kernel(qt_ref, k_ref, vt_ref, cq_ref, o_ref, qaug_ref, m_ref, acc_ref, s_ref, mt_ref):
    pair = pl.program_id(1)
    qi = pl.program_id(2)
    tq = qt_ref.shape[2]
    tk = min(FOX_TK, k_ref.shape[1])
    t0 = qi * tq
    qt = qt_ref[0]
    row = lax.broadcasted_iota(jnp.int32, (LANES, tq), 0)
    zero = jnp.zeros_like(qt)
    cq = []
    for e in range(2):
        qaug_ref[e, 0:LANES, :] = jnp.where((row >= e * HEAD_DIM) & (row < (e + 1) * HEAD_DIM), qt, zero)
        pick = (row >= e * AUG_STRIDE) & (row < e * AUG_STRIDE + 3)
        qaug_ref[e, LANES:, :] = jnp.where(pick, 1.0, 0.0).astype(BF16)
        cq.append(cq_ref[0, pl.ds(2 * pair + e, 1), :])

    m_ref[...] = jnp.full_like(m_ref, NEG)
    acc_ref[...] = jnp.zeros_like(acc_ref)

    def scores(k0):
        kt = k_ref[0, pl.ds(k0, tk), :]
        return [_dot(kt, qaug_ref[e]) for e in range(2)]

    def masked(s_list, k0):
        key = k0 + lax.broadcasted_iota(jnp.int32, (tk, tq), 0)
        tok = t0 + lax.broadcasted_iota(jnp.int32, (tk, tq), 1)
        return [jnp.where(key <= tok, sv, NEG) for sv in s_list]

    def consume(k0):
        for e in range(2):
            m_old = m_ref[e]
            m_new = jnp.maximum(m_old, mt_ref[e] + cq[e])
            p = jnp.exp2(s_ref[e] + (cq[e] - m_new)).astype(BF16)
            vt = jnp.concatenate([vt_ref[0, e * HEAD_DIM:(e + 1) * HEAD_DIM, pl.ds(k0, tk)],
                                  _ones_rows(tk)], axis=0)
            acc_ref[e] = jnp.exp2(m_old - m_new) * acc_ref[e] + _dot(vt, p)
            m_ref[e] = m_new

    _pipelined_sweep(t0 // tk, max(1, tq // tk), tk, scores, masked, consume, s_ref, mt_ref)
    o_ref[0] = jnp.concatenate(
        [acc_ref[e, 0:HEAD_DIM, :] * (1.0 / acc_ref[e, HEAD_DIM:HEAD_DIM + 1, :]) for e in range(2)], axis=0)


def _fox(fqt, fkx, fvt, cq):
    b, _, s = fqt.shape
    tq = min(FOX_TQ, s)
    return pl.pallas_call(
        _fox_kernel,
        grid=(b, N_HEADS_FOX // 2, s // tq),
        in_specs=[
            pl.BlockSpec((1, LANES, tq), lambda i, p, j: (i, p, j)),
            pl.BlockSpec((1, s, 2 * LANES), lambda i, p, j: (i, 0, p)),
            pl.BlockSpec((1, LANES, s), lambda i, p, j: (i, p, 0)),
            pl.BlockSpec((1, N_HEADS_FOX, tq), lambda i, p, j: (i, 0, j)),
        ],
        out_specs=pl.BlockSpec((1, LANES, tq), lambda i, p, j: (i, p, j)),
        out_shape=jax.ShapeDtypeStruct((b, FOX_W, s), F32),
        scratch_shapes=[
            pltpu.VMEM((2, 2 * LANES, tq), BF16),
            pltpu.VMEM((2, 1, tq), F32),
            pltpu.VMEM((2, HEAD_DIM + BF16_ROWS, tq), F32),
            pltpu.VMEM((2, min(FOX_TK, s), tq), F32),
            pltpu.VMEM((2, 1, tq), F32),
        ],
        compiler_params=_params("parallel", "parallel", "arbitrary"),
        name="fox",
    )(fqt, fkx, fvt, cq)


def _outproj_kernel(h_ref, oa_ref, ob_ref, ga_ref, gb_ref, wa_ref, wb_ref, o_ref):
    oa = oa_ref[...]
    ob = ob_ref[...]
    na = (oa * _rms(oa) * ga_ref[...]).astype(BF16)
    nb = (ob * _rms(ob) * gb_ref[...]).astype(BF16)
    o_ref[...] = h_ref[...] + _dot(na, wa_ref[...]) + _dot(nb, wb_ref[...])


def _outproj(h, oa, ob, ga, gb, wa, wb):
    t, d = h.shape
    tm = min(OUT_TM, t)
    rows = lambda c: pl.BlockSpec((tm, c), lambda i: (i, 0))
    const = lambda a: pl.BlockSpec(a.shape, lambda i: (0,) * a.ndim)
    return pl.pallas_call(
        _outproj_kernel,
        grid=(t // tm,),
        in_specs=[rows(d), rows(NSA_Q_W), rows(FOX_W), const(ga), const(gb), const(wa), const(wb)],
        out_specs=rows(d),
        out_shape=jax.ShapeDtypeStruct((t, d), F32),
        compiler_params=_params("parallel"),
        name="outproj",
    )(h, oa, ob, ga, gb, wa, wb)


def _ple_kernel(h_ref, p_ref, gg_ref, gp_ref, wg_ref, wp_ref, o_ref):
    h = h_ref[...]
    hn = (h * _rms(h) * gg_ref[...]).astype(BF16)
    gate = jax.nn.sigmoid(_dot(hn, wg_ref[...]))
    e = _dot(p_ref[...].astype(BF16), wp_ref[...])
    o_ref[...] = h + gate * (e * _rms(e) * gp_ref[...])


def _ple(h, p, gg, gp, wg, wp):
    t, d = h.shape
    tm = min(PLE_TM, t)
    rows = lambda c: pl.BlockSpec((tm, c), lambda i: (i, 0))
    const = lambda a: pl.BlockSpec(a.shape, lambda i: (0,) * a.ndim)
    return pl.pallas_call(
        _ple_kernel,
        grid=(t // tm,),
        in_specs=[rows(d), rows(p.shape[1]), const(gg), const(gp), const(wg), const(wp)],
        out_specs=rows(d),
        out_shape=jax.ShapeDtypeStruct((t, d), F32),
        compiler_params=_params("parallel"),
        name="ple",
    )(h, p, gg, gp, wg, wp)


def _expand_heads(w, n):
    w = w.reshape(w.shape[0], n, HEAD_DIM)
    return jnp.concatenate([w, jnp.zeros_like(w)], axis=-1).reshape(w.shape[0], n * LANES)


def _pad_cols(w, width):
    return jnp.pad(w, ((0, 0), (0, width - w.shape[1])))


def _layout_w_in(w_in):
    splits = [int(v) for v in np.cumsum(PROJ_SIZES)[:-1]]
    qa, kc, vc, ksl, vsl, kwn, vwn, ga, qf, kf, vf, fl = jnp.split(w_in, splits, axis=-1)
    per_group = 3 * GROUP_NSA
    ga_x = jnp.concatenate([_pad_cols(ga[:, k * per_group:(k + 1) * per_group], LANES)
                            for k in range(N_KV_NSA)], axis=-1)
    cols = [qa, kc, vc, _expand_heads(ksl, N_KV_NSA), _expand_heads(kwn, N_KV_NSA), vsl, vwn,
            ga_x, qf, kf, vf, _pad_cols(fl, 2 * LANES)]
    return jnp.concatenate(cols, axis=-1).astype(BF16)


def _rope_tables(seq):
    pos = jnp.arange(seq, dtype=F32)
    inv = ROPE_THETA ** (-jnp.arange(0, ROT_DIM, 2, dtype=F32) / ROT_DIM)
    ang = pos[:, None] * inv[None, :]
    cos, sin = jnp.cos(ang), jnp.sin(ang)
    half = ROT_DIM // 2
    rest = HEAD_DIM - ROT_DIM
    ones = jnp.ones((seq, rest), F32)
    zeros = jnp.zeros((seq, rest), F32)
    zh = jnp.zeros((seq, half), F32)
    cos_t = jnp.concatenate([cos, cos, ones], axis=-1)
    sin_lo = jnp.concatenate([-sin, zh, zeros], axis=-1)
    sin_hi = jnp.concatenate([zh, sin, zeros], axis=-1)
    tile2 = lambda t: jnp.concatenate([t, t], axis=-1)
    return jnp.stack([tile2(cos_t), tile2(sin_lo), tile2(sin_hi)])


def _layout_cmp_w1(w1):
    hid = w1.shape[1]
    w = w1.reshape(2, CMP_STRIDE, HEAD_DIM, hid)
    z = jnp.zeros_like(w)
    per_head = [jnp.concatenate([w, z], axis=2), jnp.concatenate([z, w], axis=2)]
    return jnp.stack(per_head).reshape(N_KV_NSA, 2, CMP_STRIDE * LANES, hid).astype(BF16)


def _layout_cmp_pos(pos):
    p = pos.reshape(2, CMP_STRIDE, HEAD_DIM)
    p = jnp.concatenate([p, p], axis=-1).reshape(2, 1, CMP_STRIDE * LANES)
    return jnp.broadcast_to(p, (2, 8, CMP_STRIDE * LANES)).astype(BF16)


def _overlap_t(seq):
    rows = seq // CMP_STRIDE
    n_sel = seq // SEL_BLOCK
    cmp_start = np.arange(rows) * CMP_STRIDE
    sel_start = np.arange(n_sel) * SEL_BLOCK
    ov = ((cmp_start[None, :] <= sel_start[:, None] + SEL_BLOCK - 1)
          & (cmp_start[None, :] + CMP_BLOCK - 1 >= sel_start[:, None]))
    ov[:, (seq - CMP_BLOCK) // CMP_STRIDE + 1:] = False
    return jnp.asarray(ov.astype(np.float32), BF16)


def _row(v, width=None):
    v = v.reshape(1, -1).astype(F32)
    return v if width is None else _pad_cols(v, width)


def _feature_major(a):
    return jnp.transpose(a, (0, 2, 1))


def kernel(x, p, ffn1_norm, ffn1_wg, ffn1_wu, ffn1_wd, mix_norm, w_in, b_forget, q_norm_nsa, k_norm_cmp, k_norm_slc, k_norm_win, cmp_pos_k, cmp_pos_v, cmp_k_w1, cmp_k_w2, cmp_v_w1, cmp_v_w2, q_norm_fox, k_norm_fox, out_norm_nsa, out_norm_fox, w_out, ffn2_norm, ffn2_wg, ffn2_wu, ffn2_wd, ple_gate_norm, ple_w_gate, ple_w_proj, ple_norm):
    b, s, d = x.shape
    assert s // SEL_BLOCK <= HEAD_DIM and (s // SEL_BLOCK) % 8 == 0
    assert s % max(NSA_TK, FOX_TK, FOX_TQ) == 0
    depth = ffn1_norm.shape[0]
    t = b * s
    rope_tab = _rope_tables(s)
    ov_t = _overlap_t(s)
    tile4 = lambda g: jnp.concatenate([g] * 4).reshape(1, 2 * LANES).astype(F32)
    h = x.reshape(t, d)
    for i in range(depth):
        h = _ffn(h, _row(ffn1_norm[i]), ffn1_wg[i].astype(BF16), ffn1_wu[i].astype(BF16),
                 ffn1_wd[i].astype(BF16))

        gains = jnp.concatenate([tile4(q_norm_nsa[i]), tile4(k_norm_slc[i]), tile4(k_norm_win[i]),
                                 tile4(q_norm_fox[i]), tile4(k_norm_fox[i]),
                                 jnp.zeros((3, 2 * LANES), F32)], axis=0)
        (qa, kc, vc, kslx, vsl, kwnx, vwn, gates, fq, fkx, fv, cum) = _inproj(
            h.reshape(b, s, d), _row(mix_norm[i]), _layout_w_in(w_in[i]),
            _row(b_forget[i], LANES), gains, rope_tab)

        rows = s // CMP_STRIDE

        def shifted(tok):
            r0 = tok.reshape(b, rows, CMP_STRIDE * LANES)
            r1 = jnp.concatenate([r0[:, 1:], jnp.zeros_like(r0[:, :1])], axis=1)
            return r0, r1

        r0k, r1k = shifted(kc)
        r0v, r1v = shifted(vc)
        kcx, vcx = _compress(
            r0k, r1k, r0v, r1v, _layout_cmp_w1(cmp_k_w1[i]), _layout_cmp_w1(cmp_v_w1[i]),
            _layout_cmp_pos(cmp_pos_k[i]), _layout_cmp_pos(cmp_pos_v[i]),
            _pad_cols(cmp_k_w2[i], LANES).astype(BF16), _pad_cols(cmp_v_w2[i], LANES).astype(BF16),
            _row(k_norm_cmp[i], LANES))
        vct = jnp.transpose(vcx[..., :HEAD_DIM], (0, 1, 3, 2))

        o_at = _nsa(_feature_major(qa), kcx, vct, kslx, _feature_major(vsl), kwnx,
                    _feature_major(vwn), _feature_major(gates), ov_t)
        o_bt = _fox(_feature_major(fq), fkx, _feature_major(fv),
                    _feature_major(cum[:, :, :N_HEADS_FOX]))

        w_o = w_out[i].astype(BF16)
        h = _outproj(h, _feature_major(o_at).reshape(t, NSA_Q_W), _feature_major(o_bt).reshape(t, FOX_W),
                     _row(out_norm_nsa[i]), _row(out_norm_fox[i]), w_o[:NSA_Q_W], w_o[NSA_Q_W:])

        h = _ffn(h, _row(ffn2_norm[i]), ffn2_wg[i].astype(BF16), ffn2_wu[i].astype(BF16),
                 ffn2_wd[i].astype(BF16))

        h = _ple(h, p[i].reshape(t, -1), _row(ple_gate_norm[i]), _row(ple_norm[i]),
                 ple_w_gate[i].astype(BF16), ple_w_proj[i].astype(BF16))
    return h.reshape(b, s, d)
```

```python
import numpy as np
import jax
import jax.numpy as jnp
from jax import lax
from jax.experimental import pallas as pl
from jax.experimental.pallas import tpu as pltpu

F32 = jnp.float32
BF16 = jnp.bfloat16

D_MODEL = 1024
HEAD_DIM = 64
N_HEADS_NSA = 8
N_KV_NSA = 2
GROUP_NSA = N_HEADS_NSA // N_KV_NSA
N_HEADS_FOX = 8
NSA_Q_W = N_HEADS_NSA * HEAD_DIM
NSA_KV_W = N_KV_NSA * HEAD_DIM
FOX_W = N_HEADS_FOX * HEAD_DIM
PROJ_SIZES = (NSA_Q_W, NSA_KV_W, NSA_KV_W, NSA_KV_W, NSA_KV_W, NSA_KV_W, NSA_KV_W,
              3 * N_HEADS_NSA, FOX_W, FOX_W, FOX_W, N_HEADS_FOX)
D_FF = 2816
D_PLE = 256
ROPE_THETA = 500000.0
ROT_DIM = HEAD_DIM // 4
CMP_BLOCK = 32
CMP_STRIDE = 16
CMP_HIDDEN = 256
SEL_BLOCK = 64
SEL_SHIFT = 6
SEL_TOPK = 16
WINDOW = 512
FORCED_SCORE = 1e6
EPS = 1e-6
NEG = -1e30
LOG2E = 1.4426950408889634

LANES = 128
BF16_ROWS = 16
VMEM_LIMIT = 56 * 1024 * 1024
AUG_STRIDE = 8

C_QA = 0
C_KCVC = C_QA + NSA_Q_W
C_KSL = C_KCVC + 2 * LANES
C_KWN = C_KSL + N_KV_NSA * LANES
C_VSLWN = C_KWN + N_KV_NSA * LANES
C_GATE = C_VSLWN + 2 * LANES
C_FQ = C_GATE + N_KV_NSA * LANES
C_FK = C_FQ + FOX_W
C_FV = C_FK + FOX_W
C_FL = C_FV + FOX_W
C_END = C_FL + 2 * LANES

FFN_TM = 1024
FFN_TF = 256
PROJ_TS = 512
NSA_TQ = 256
NSA_TK = 256
FOX_TQ = 512
FOX_TK = 256
OUT_TM = 512
PLE_TM = 512


def _dot(a, b):
    return jnp.dot(a, b, preferred_element_type=F32)


def _dot_nt(a, b):
    return lax.dot_general(a, b, (((1,), (1,)), ((), ())), preferred_element_type=F32)


def _params(*sem):
    return pltpu.CompilerParams(dimension_semantics=sem, vmem_limit_bytes=VMEM_LIMIT)


def _rms(x):
    return lax.rsqrt(jnp.mean(x * x, axis=-1, keepdims=True) + EPS)


def _split3(x):
    hi = x.astype(BF16)
    r1 = x - hi.astype(F32)
    mid = r1.astype(BF16)
    lo = (r1 - mid.astype(F32)).astype(BF16)
    return hi, mid, lo


def _ffn_kernel(x_ref, g_ref, wg_ref, wu_ref, wd_ref, o_ref, xn_ref, acc_ref):
    f = pl.program_id(1)

    @pl.when(f == 0)
    def _():
        x = x_ref[...]
        xn_ref[...] = (x * _rms(x) * g_ref[...]).astype(BF16)
        acc_ref[...] = jnp.zeros_like(acc_ref)

    xn = xn_ref[...]
    gate = _dot(xn, wg_ref[...])
    up = _dot(xn, wu_ref[...])
    act = (gate * jax.nn.sigmoid(gate) * up).astype(BF16)
    acc_ref[...] += _dot(act, wd_ref[...])

    @pl.when(f == pl.num_programs(1) - 1)
    def _():
        o_ref[...] = x_ref[...] + 0.5 * acc_ref[...]


def _ffn(x, g, wg, wu, wd):
    t, d = x.shape
    f = wg.shape[1]
    tm, tf = min(FFN_TM, t), FFN_TF
    return pl.pallas_call(
        _ffn_kernel,
        grid=(t // tm, f // tf),
        in_specs=[
            pl.BlockSpec((tm, d), lambda i, j: (i, 0)),
            pl.BlockSpec((1, d), lambda i, j: (0, 0)),
            pl.BlockSpec((d, tf), lambda i, j: (0, j)),
            pl.BlockSpec((d, tf), lambda i, j: (0, j)),
            pl.BlockSpec((tf, d), lambda i, j: (j, 0)),
        ],
        out_specs=pl.BlockSpec((tm, d), lambda i, j: (i, 0)),
        out_shape=jax.ShapeDtypeStruct((t, d), F32),
        scratch_shapes=[pltpu.VMEM((tm, d), BF16), pltpu.VMEM((tm, d), F32)],
        compiler_params=_params("parallel", "arbitrary"),
        name="ffn",
    )(x, g, wg, wu, wd)


def _log_sigmoid(x):
    return -(jnp.maximum(-x, 0.0) + jnp.log(1.0 + jnp.exp(-jnp.abs(x))))


def _inproj_kernel(h_ref, g_ref, w_ref, bf_ref, gains_ref, rope_ref,
                   qa_ref, kc_ref, vc_ref, ksl_ref, vsl_ref, kwn_ref, vwn_ref,
                   gate_ref, fq_ref, fk_ref, fv_ref, cum_ref, carry_ref):
    si = pl.program_id(1)
    ts = h_ref.shape[1]
    wide = 2 * LANES
    x = h_ref[0]
    a = (x * _rms(x) * g_ref[...]).astype(BF16)

    row = lax.broadcasted_iota(jnp.int32, (2 * wide, wide), 0) & (wide - 1)
    col = lax.broadcasted_iota(jnp.int32, (2 * wide, wide), 1)
    ones_head = jnp.where(jnp.right_shift(row, 6) == jnp.right_shift(col, 6), 1.0, 0.0).astype(BF16)
    ones_chunk = jnp.where(jnp.right_shift(row, 7) == jnp.right_shift(col, 7), 1.0, 0.0).astype(BF16)
    cos_t, sin_lo, sin_hi = rope_ref[0], rope_ref[1], rope_ref[2]

    def proj(c0):
        return _dot(a, w_ref[:, c0:c0 + wide])

    def norm(u, ones_mat, gain_row):
        u2 = u * u
        hi = u2.astype(BF16)
        lo = (u2 - hi.astype(F32)).astype(BF16)
        ms = _dot(jnp.concatenate([hi, lo], axis=1), ones_mat) * (1.0 / HEAD_DIM)
        return u * lax.rsqrt(ms + EPS) * gains_ref[gain_row:gain_row + 1, :]

    def rope_half(u):
        return (u * cos_t + pltpu.roll(u, LANES - ROT_DIM // 2, 1) * sin_lo
                + pltpu.roll(u, ROT_DIM // 2, 1) * sin_hi)

    def rope(u):
        return jnp.concatenate([rope_half(u[:, :LANES]), rope_half(u[:, LANES:])], axis=1)

    def feature_major(u):
        return jnp.transpose(u.astype(BF16))

    scale = HEAD_DIM ** -0.5 * LOG2E
    for c in range(NSA_Q_W // wide):
        sl = slice(c * wide, (c + 1) * wide)
        qa_ref[0, sl, :] = feature_major(rope(norm(proj(C_QA + c * wide), ones_head, 0)) * scale)

    u = proj(C_KCVC)
    kc_ref[0] = rope_half(u[:, :LANES]).astype(BF16)
    vc_ref[0] = u[:, LANES:].astype(BF16)
    u = proj(C_VSLWN)
    u = feature_major(u)
    vsl_ref[0] = u[:LANES]
    vwn_ref[0] = u[LANES:]

    pos = si * ts + lax.broadcasted_iota(jnp.int32, (ts, wide), 0)
    lane = lax.broadcasted_iota(jnp.int32, (ts, wide), 1) & (LANES - 1)
    sel_onehot = jnp.where(lane == HEAD_DIM + jnp.right_shift(pos, SEL_SHIFT), 1.0, 0.0)
    ksl_ref[0] = (rope(norm(proj(C_KSL), ones_chunk, 1)) + sel_onehot).astype(BF16)
    kwn_ref[0] = rope(norm(proj(C_KWN), ones_chunk, 2)).astype(BF16)
    gate_ref[0] = jnp.transpose(jax.nn.sigmoid(proj(C_GATE)))

    @pl.when(si == 0)
    def _():
        carry_ref[...] = jnp.zeros_like(carry_ref)

    lf = _log_sigmoid(proj(C_FL)[:, :LANES] + bf_ref[...])
    r_i = lax.broadcasted_iota(jnp.int32, (ts, ts), 0)
    c_i = lax.broadcasted_iota(jnp.int32, (ts, ts), 1)
    tri = jnp.where(r_i >= c_i, 1.0, 0.0).astype(BF16)
    hi, mid, lo = _split3(lf)
    c = _dot(tri, hi) + _dot(tri, mid) + _dot(tri, lo) + carry_ref[0:1, :]
    carry_ref[...] = jnp.broadcast_to(c[ts - 1:ts, :], carry_ref.shape)
    c = c * LOG2E
    cum_ref[0] = jnp.transpose(c)[:N_HEADS_FOX]

    n_pairs = N_HEADS_FOX // 2
    p_row = lax.broadcasted_iota(jnp.int32, (3 * LANES, n_pairs * LANES), 0)
    p_col = lax.broadcasted_iota(jnp.int32, (3 * LANES, n_pairs * LANES), 1)
    head = p_row & (LANES - 1)
    part = jnp.right_shift(p_row, 7)
    target = jnp.right_shift(head, 1) * LANES + (head & 1) * AUG_STRIDE + part
    place = jnp.where((p_col == target) & (head < N_HEADS_FOX), 1.0, 0.0).astype(BF16)
    aug = _dot(jnp.concatenate(_split3(-c), axis=1), place)

    for c2 in range(FOX_W // wide):
        sl = slice(c2 * wide, (c2 + 1) * wide)
        fq_ref[0, sl, :] = feature_major(norm(proj(C_FQ + c2 * wide), ones_head, 3) * scale)
        fv_ref[0, sl, :] = feature_major(proj(C_FV + c2 * wide))
        fk = norm(proj(C_FK + c2 * wide), ones_head, 4).astype(BF16)
        for e in range(2):
            pair = 2 * c2 + e
            fk_ref[0, :, 2 * pair * LANES:(2 * pair + 1) * LANES] = fk[:, e * LANES:(e + 1) * LANES]
            fk_ref[0, :, (2 * pair + 1) * LANES:(2 * pair + 2) * LANES] = (
                aug[:, pair * LANES:(pair + 1) * LANES].astype(BF16))


def _inproj(h, g, w_all, bf_row, gains, rope_tab):
    b, s, d = h.shape
    ts = min(PROJ_TS, s)
    tok = lambda c: pl.BlockSpec((1, ts, c), lambda i, j: (i, j, 0))
    const = lambda shape: pl.BlockSpec(shape, lambda i, j: (0,) * len(shape))
    feat = lambda c: pl.BlockSpec((1, c, ts), lambda i, j: (i, 0, j))
    kvx = N_KV_NSA * LANES
    specs = [(NSA_Q_W, BF16, True), (LANES, BF16, False), (LANES, BF16, False), (kvx, BF16, False),
             (LANES, BF16, True), (kvx, BF16, False), (LANES, BF16, True), (kvx, F32, True),
             (FOX_W, BF16, True), (2 * FOX_W, BF16, False), (FOX_W, BF16, True),
             (N_HEADS_FOX, F32, True)]
    out_shape = [jax.ShapeDtypeStruct((b, c, s) if fm else (b, s, c), dt) for c, dt, fm in specs]
    return pl.pallas_call(
        _inproj_kernel,
        grid=(b, s // ts),
        in_specs=[
            tok(d),
            const((1, d)),
            const((d, C_END)),
            const((1, LANES)),
            const((8, 2 * LANES)),
            pl.BlockSpec((3, ts, LANES), lambda i, j: (0, j, 0)),
        ],
        out_specs=[feat(c) if fm else tok(c) for c, _, fm in specs],
        out_shape=out_shape,
        scratch_shapes=[pltpu.VMEM((8, LANES), F32)],
        compiler_params=_params("parallel", "arbitrary"),
        name="inproj",
    )(h, g, w_all, bf_row, gains, rope_tab)


def _compress_kernel(r0k_ref, r1k_ref, r0v_ref, r1v_ref, w1k_ref, w1v_ref,
                     pk_ref, pv_ref, w2k_ref, w2v_ref, gain_ref, kc_ref, vc_ref):
    def mlp(r0_ref, r1_ref, w1_ref, p_ref, w2_ref, kh):
        top, bot = w1_ref[kh, 0], w1_ref[kh, 1]
        bias = _dot(p_ref[0], top) + _dot(p_ref[1], bot)
        hid = _dot(r0_ref[0], top) + _dot(r1_ref[0], bot) + bias[0:1, :]
        hid = hid * jax.nn.sigmoid(hid)
        return _dot(hid.astype(BF16), w2_ref[...])

    for kh in range(N_KV_NSA):
        kc = mlp(r0k_ref, r1k_ref, w1k_ref, pk_ref, w2k_ref, kh)
        ms = jnp.sum(kc * kc, axis=-1, keepdims=True) * (1.0 / HEAD_DIM)
        kc_ref[0, kh] = (kc * lax.rsqrt(ms + EPS) * gain_ref[...]).astype(BF16)
        vc_ref[0, kh] = mlp(r0v_ref, r1v_ref, w1v_ref, pv_ref, w2v_ref, kh).astype(BF16)


def _compress(r0k, r1k, r0v, r1v, w1k, w1v, pk, pv, w2k, w2v, gain):
    b, r, w = r0k.shape
    rows = pl.BlockSpec((1, r, w), lambda i: (i, 0, 0))
    const = lambda a: pl.BlockSpec(a.shape, lambda i: (0,) * a.ndim)
    out = jax.ShapeDtypeStruct((b, N_KV_NSA, r, LANES), BF16)
    out_spec = pl.BlockSpec((1, N_KV_NSA, r, LANES), lambda i: (i, 0, 0, 0))
    return pl.pallas_call(
        _compress_kernel,
        grid=(b,),
        in_specs=[rows, rows, rows, rows, const(w1k), const(w1v), const(pk), const(pv),
                  const(w2k), const(w2v), const(gain)],
        out_specs=[out_spec, out_spec],
        out_shape=[out, out],
        compiler_params=_params("parallel"),
        name="compress",
    )(r0k, r1k, r0v, r1v, w1k, w1v, pk, pv, w2k, w2v, gain)


def _ones_rows(width):
    return jnp.ones((BF16_ROWS, width), BF16)


def _pipelined_sweep(n_full, n_tail, tk, scores, masked, consume, s_ref, mt_ref):
    def stash(s_list):
        for e, s in enumerate(s_list):
            s_ref[e] = s
            mt_ref[e] = jnp.max(s, axis=0, keepdims=True)

    def body(j, carry):
        k0 = pl.multiple_of(j * tk, tk)
        s_new = scores(k0)
        consume(pl.multiple_of(k0 - tk, tk))
        stash(s_new)
        return carry

    def tail_step(k0):
        s_new = scores(k0)
        consume(pl.multiple_of(k0 - tk, tk))
        stash(masked(s_new, k0))

    k_tail = pl.multiple_of(n_full * tk, tk)

    @pl.when(n_full == 0)
    def _():
        stash(masked(scores(k_tail), k_tail))

    @pl.when(n_full > 0)
    def _():
        stash(scores(0))
        lax.fori_loop(1, n_full, body, 0)
        tail_step(k_tail)

    for i in range(1, n_tail):
        tail_step(pl.multiple_of(k_tail + i * tk, tk))
    consume(pl.multiple_of(k_tail + (n_tail - 1) * tk, tk))


def _nsa_kernel(qt_ref, kc_ref, vct_ref, ksl_ref, vslt_ref, kwn_ref, vwnt_ref, gt_ref, ov_ref,
                o_ref, qaug_ref, m_ref, acc_ref, s_ref, mt_ref):
    qi = pl.program_id(2)
    tq = qt_ref.shape[2]
    s_len = ksl_ref.shape[1]
    n_cmp_rows = kc_ref.shape[2]
    n_sel = ov_ref.shape[0]
    top = min(SEL_TOPK, n_sel)
    grp = GROUP_NSA
    cols = grp * tq
    tk = min(NSA_TK, s_len)
    t0 = qi * tq

    def tok_of_col(rows):
        return t0 + (lax.broadcasted_iota(jnp.int32, (rows, cols), 1) & (tq - 1))

    def tok_of_tile(rows):
        return t0 + lax.broadcasted_iota(jnp.int32, (rows, tq), 1)

    def softmax_cols(s, keep):
        s = s + jnp.concatenate([jnp.where(keep, 0.0, NEG)] * grp, axis=1)
        mx = jnp.max(s, axis=0, keepdims=True)
        e = jnp.exp2(s - mx)
        den = jnp.sum(e, axis=0, keepdims=True)
        return e, jnp.where(mx > 0.5 * NEG, 1.0 / den, 0.0)

    qt = qt_ref[0]
    qaug_ref[0:HEAD_DIM, :] = jnp.concatenate(
        [qt[g * HEAD_DIM:(g + 1) * HEAD_DIM] for g in range(grp)], axis=1)
    qaug_ref[HEAD_DIM:, :] = jnp.zeros((HEAD_DIM, cols), BF16)

    n_c = lax.broadcasted_iota(jnp.int32, (n_cmp_rows, tq), 0)
    keep_c = n_c * CMP_STRIDE + (CMP_BLOCK - 1) <= tok_of_tile(n_cmp_rows)
    e_c, inv_c = softmax_cols(_dot(kc_ref[0, 0], qaug_ref[...]), keep_c)
    o_c = _dot(vct_ref[0, 0], e_c.astype(BF16)) * inv_c

    p_sum = e_c[:, 0:tq] * inv_c[:, 0:tq]
    for g in range(1, grp):
        p_sum = p_sum + e_c[:, g * tq:(g + 1) * tq] * inv_c[:, g * tq:(g + 1) * tq]
    p_hi = p_sum.astype(BF16)
    p_lo = (p_sum - p_hi.astype(F32)).astype(BF16)
    ov = ov_ref[...]
    imp = _dot(ov, p_hi) + _dot(ov, p_lo)
    j_blk = lax.broadcasted_iota(jnp.int32, (n_sel, tq), 0)
    cur = jnp.right_shift(t0 + lax.broadcasted_iota(jnp.int32, (n_sel, tq), 1), SEL_SHIFT)
    forced = (j_blk == 0) | (j_blk == cur) | (j_blk == cur - 1)
    imp = jnp.where(j_blk <= cur, jnp.where(forced, FORCED_SCORE, imp), -1.0)
    sub = lax.broadcasted_iota(jnp.int32, (8, tq), 0)
    groups = [imp[8 * v:8 * v + 8] for v in range(n_sel // 8)]
    ranks = [jnp.zeros((8, tq), F32) for _ in groups]
    for i in range(n_sel):
        r_i = jnp.broadcast_to(imp[i:i + 1, :], (8, tq))
        for v, g_v in enumerate(groups):
            if i < 8 * v:
                beats = r_i >= g_v
            elif i >= 8 * v + 8:
                beats = r_i > g_v
            else:
                beats = (r_i > g_v) | ((r_i == g_v) & (sub > i - 8 * v))
            ranks[v] = ranks[v] + jnp.where(beats, 1.0, 0.0)
    chosen = (jnp.concatenate(ranks, axis=0) < top) & (imp >= 0.0)
    bias_t = jnp.where(chosen, 0.0, NEG)
    if n_sel < HEAD_DIM:
        bias_t = jnp.concatenate([bias_t, jnp.zeros((HEAD_DIM - n_sel, tq), F32)], axis=0)
    qaug_ref[HEAD_DIM:, :] = jnp.concatenate([bias_t.astype(BF16)] * grp, axis=1)

    m_ref[...] = jnp.full_like(m_ref, NEG)
    acc_ref[...] = jnp.zeros_like(acc_ref)

    def slc_scores(k0):
        return [_dot(ksl_ref[0, pl.ds(k0, tk), :], qaug_ref[...])]

    def slc_masked(s_list, k0):
        key = k0 + lax.broadcasted_iota(jnp.int32, (tk, tq), 0)
        bias = jnp.where(key <= tok_of_tile(tk), 0.0, NEG)
        return [s_list[0] + jnp.concatenate([bias] * grp, axis=1)]

    def slc_consume(k0):
        m_old = m_ref[...]
        m_new = jnp.maximum(m_old, mt_ref[0])
        p = jnp.exp2(s_ref[0] - m_new).astype(BF16)
        vt = jnp.concatenate([vslt_ref[0, :, pl.ds(k0, tk)], _ones_rows(tk)], axis=0)
        acc_ref[...] = jnp.exp2(m_old - m_new) * acc_ref[...] + _dot(vt, p)
        m_ref[...] = m_new

    _pipelined_sweep(t0 // tk, max(1, tq // tk), tk, slc_scores, slc_masked, slc_consume, s_ref, mt_ref)
    o_s = acc_ref[0:HEAD_DIM, :] * (1.0 / acc_ref[HEAD_DIM:HEAD_DIM + 1, :])

    span = min(WINDOW + tq, s_len)
    start = pl.multiple_of(jnp.maximum(t0 + tq - span, 0), LANES)
    t_w = tok_of_tile(span)
    k_w = start + lax.broadcasted_iota(jnp.int32, (span, tq), 0)
    keep_w = (k_w <= t_w) & (t_w - k_w < WINDOW)
    e_w, inv_w = softmax_cols(_dot(kwn_ref[0, pl.ds(start, span), :], qaug_ref[...]), keep_w)
    o_w = _dot(vwnt_ref[0, :, pl.ds(start, span)], e_w.astype(BF16)) * inv_w

    gt = gt_ref[0]
    outs = []
    for g in range(grp):
        sl = slice(g * tq, (g + 1) * tq)
        outs.append(gt[3 * g:3 * g + 1] * o_c[:, sl] + gt[3 * g + 1:3 * g + 2] * o_s[:, sl]
                    + gt[3 * g + 2:3 * g + 3] * o_w[:, sl])
    o_ref[0] = jnp.transpose(jnp.concatenate(outs, axis=0))


def _nsa(qat, kcx, vct, kslx, vslt, kwnx, vwnt, gates_t, ov_t):
    b, _, s = qat.shape
    tq = min(NSA_TQ, s)
    assert tq & (tq - 1) == 0 and tq % LANES == 0
    r = kcx.shape[2]
    gw = GROUP_NSA * HEAD_DIM
    cols = GROUP_NSA * tq
    gate_rows = 16
    key_major = pl.BlockSpec((1, s, LANES), lambda i, k, j: (i, 0, k))
    val_major = pl.BlockSpec((1, HEAD_DIM, s), lambda i, k, j: (i, k, 0))
    return pl.pallas_call(
        _nsa_kernel,
        grid=(b, N_KV_NSA, s // tq),
        in_specs=[
            pl.BlockSpec((1, gw, tq), lambda i, k, j: (i, k, j)),
            pl.BlockSpec((1, 1, r, LANES), lambda i, k, j: (i, k, 0, 0)),
            pl.BlockSpec((1, 1, HEAD_DIM, r), lambda i, k, j: (i, k, 0, 0)),
            key_major, val_major, key_major, val_major,
            pl.BlockSpec((1, gate_rows, tq), lambda i, k, j: (i, k * (LANES // gate_rows), j)),
            pl.BlockSpec(ov_t.shape, lambda i, k, j: (0, 0)),
        ],
        out_specs=pl.BlockSpec((1, tq, gw), lambda i, k, j: (i, j, k)),
        out_shape=jax.ShapeDtypeStruct((b, s, NSA_Q_W), F32),
        scratch_shapes=[
            pltpu.VMEM((LANES, cols), BF16),
            pltpu.VMEM((1, cols), F32),
            pltpu.VMEM((HEAD_DIM + BF16_ROWS, cols), F32),
            pltpu.VMEM((1, min(NSA_TK, s), cols), F32),
            pltpu.VMEM((1, 1, cols), F32),
        ],
        compiler_params=_params("parallel", "parallel", "arbitrary"),
        name="nsa",
    )(qat, kcx, vct, kslx, vslt, kwnx, vwnt, gates_t, ov_t)


def _fox_kernel(qt_ref, k_ref, vt_ref, cq_ref, o_ref, qaug_ref, m_ref, acc_ref, s_ref, mt_ref):
    pair = pl.program_id(1)
    qi = pl.program_id(2)
    tq = qt_ref.shape[2]
    tk = min(FOX_TK, k_ref.shape[1])
    t0 = qi * tq
    qt = qt_ref[0]
    row = lax.broadcasted_iota(jnp.int32, (LANES, tq), 0)
    zero = jnp.zeros_like(qt)
    cq = []
    for e in range(2):
        qaug_ref[e, 0:LANES, :] = jnp.where((row >= e * HEAD_DIM) & (row < (e + 1) * HEAD_DIM), qt, zero)
        pick = (row >= e * AUG_STRIDE) & (row < e * AUG_STRIDE + 3)
        qaug_ref[e, LANES:, :] = jnp.where(pick, 1.0, 0.0).astype(BF16)
        cq.append(cq_ref[0, pl.ds(2 * pair + e, 1), :])

    m_ref[...] = jnp.full_like(m_ref, NEG)
    acc_ref[...] = jnp.zeros_like(acc_ref)

    def scores(k0):
        kt = k_ref[0, pl.ds(k0, tk), :]
        return [_dot(kt, qaug_ref[e]) for e in range(2)]

    def masked(s_list, k0):
        key = k0 + lax.broadcasted_iota(jnp.int32, (tk, tq), 0)
        tok = t0 + lax.broadcasted_iota(jnp.int32, (tk, tq), 1)
        return [jnp.where(key <= tok, sv, NEG) for sv in s_list]

    def consume(k0):
        for e in range(2):
            m_old = m_ref[e]
            m_new = jnp.maximum(m_old, mt_ref[e] + cq[e])
            p = jnp.exp2(s_ref[e] + (cq[e] - m_new)).astype(BF16)
            vt = jnp.concatenate([vt_ref[0, e * HEAD_DIM:(e + 1) * HEAD_DIM, pl.ds(k0, tk)],
                                  _ones_rows(tk)], axis=0)
            acc_ref[e] = jnp.exp2(m_old - m_new) * acc_ref[e] + _dot(vt, p)
            m_ref[e] = m_new

    _pipelined_sweep(t0 // tk, max(1, tq // tk), tk, scores, masked, consume, s_ref, mt_ref)
    o_ref[0] = jnp.transpose(jnp.concatenate(
        [acc_ref[e, 0:HEAD_DIM, :] * (1.0 / acc_ref[e, HEAD_DIM:HEAD_DIM + 1, :]) for e in range(2)], axis=0))


def _fox(fqt, fkx, fvt, cq):
    b, _, s = fqt.shape
    tq = min(FOX_TQ, s)
    return pl.pallas_call(
        _fox_kernel,
        grid=(b, N_HEADS_FOX // 2, s // tq),
        in_specs=[
            pl.BlockSpec((1, LANES, tq), lambda i, p, j: (i, p, j)),
            pl.BlockSpec((1, s, 2 * LANES), lambda i, p, j: (i, 0, p)),
            pl.BlockSpec((1, LANES, s), lambda i, p, j: (i, p, 0)),
            pl.BlockSpec((1, N_HEADS_FOX, tq), lambda i, p, j: (i, 0, j)),
        ],
        out_specs=pl.BlockSpec((1, tq, LANES), lambda i, p, j: (i, j, p)),
        out_shape=jax.ShapeDtypeStruct((b, s, FOX_W), F32),
        scratch_shapes=[
            pltpu.VMEM((2, 2 * LANES, tq), BF16),
            pltpu.VMEM((2, 1, tq), F32),
            pltpu.VMEM((2, HEAD_DIM + BF16_ROWS, tq), F32),
            pltpu.VMEM((2, min(FOX_TK, s), tq), F32),
            pltpu.VMEM((2, 1, tq), F32),
        ],
        compiler_params=_params("parallel", "parallel", "arbitrary"),
        name="fox",
    )(fqt, fkx, fvt, cq)


def _outproj_kernel(h_ref, oa_ref, ob_ref, ga_ref, gb_ref, wa_ref, wb_ref, o_ref):
    oa = oa_ref[...]
    ob = ob_ref[...]
    na = (oa * _rms(oa) * ga_ref[...]).astype(BF16)
    nb = (ob * _rms(ob) * gb_ref[...]).astype(BF16)
    o_ref[...] = h_ref[...] + _dot(na, wa_ref[...]) + _dot(nb, wb_ref[...])


def _outproj(h, oa, ob, ga, gb, wa, wb):
    t, d = h.shape
    tm = min(OUT_TM, t)
    rows = lambda c: pl.BlockSpec((tm, c), lambda i: (i, 0))
    const = lambda a: pl.BlockSpec(a.shape, lambda i: (0,) * a.ndim)
    return pl.pallas_call(
        _outproj_kernel,
        grid=(t // tm,),
        in_specs=[rows(d), rows(NSA_Q_W), rows(FOX_W), const(ga), const(gb), const(wa), const(wb)],
        out_specs=rows(d),
        out_shape=jax.ShapeDtypeStruct((t, d), F32),
        compiler_params=_params("parallel"),
        name="outproj",
    )(h, oa, ob, ga, gb, wa, wb)


def _ple_kernel(h_ref, p_ref, gg_ref, gp_ref, wg_ref, wp_ref, o_ref):
    h = h_ref[...]
    hn = (h * _rms(h) * gg_ref[...]).astype(BF16)
    gate = jax.nn.sigmoid(_dot(hn, wg_ref[...]))
    e = _dot(p_ref[...].astype(BF16), wp_ref[...])
    o_ref[...] = h + gate * (e * _rms(e) * gp_ref[...])


def _ple(h, p, gg, gp, wg, wp):
    t, d = h.shape
    tm = min(PLE_TM, t)
    rows = lambda c: pl.BlockSpec((tm, c), lambda i: (i, 0))
    const = lambda a: pl.BlockSpec(a.shape, lambda i: (0,) * a.ndim)
    return pl.pallas_call(
        _ple_kernel,
        grid=(t // tm,),
        in_specs=[rows(d), rows(p.shape[1]), const(gg), const(gp), const(wg), const(wp)],
        out_specs=rows(d),
        out_shape=jax.ShapeDtypeStruct((t, d), F32),
        compiler_params=_params("parallel"),
        name="ple",
    )(h, p, gg, gp, wg, wp)


def _expand_heads(w, n):
    w = w.reshape(w.shape[0], n, HEAD_DIM)
    return jnp.concatenate([w, jnp.zeros_like(w)], axis=-1).reshape(w.shape[0], n * LANES)


def _pad_cols(w, width):
    return jnp.pad(w, ((0, 0), (0, width - w.shape[1])))


def _layout_w_in(w_in):
    splits = [int(v) for v in np.cumsum(PROJ_SIZES)[:-1]]
    qa, kc, vc, ksl, vsl, kwn, vwn, ga, qf, kf, vf, fl = jnp.split(w_in, splits, axis=-1)
    per_group = 3 * GROUP_NSA
    ga_x = jnp.concatenate([_pad_cols(ga[:, k * per_group:(k + 1) * per_group], LANES)
                            for k in range(N_KV_NSA)], axis=-1)
    cols = [qa, kc, vc, _expand_heads(ksl, N_KV_NSA), _expand_heads(kwn, N_KV_NSA), vsl, vwn,
            ga_x, qf, kf, vf, _pad_cols(fl, 2 * LANES)]
    return jnp.concatenate(cols, axis=-1).astype(BF16)


def _rope_tables(seq):
    pos = jnp.arange(seq, dtype=F32)
    inv = ROPE_THETA ** (-jnp.arange(0, ROT_DIM, 2, dtype=F32) / ROT_DIM)
    ang = pos[:, None] * inv[None, :]
    cos, sin = jnp.cos(ang), jnp.sin(ang)
    half = ROT_DIM // 2
    rest = HEAD_DIM - ROT_DIM
    ones = jnp.ones((seq, rest), F32)
    zeros = jnp.zeros((seq, rest), F32)
    zh = jnp.zeros((seq, half), F32)
    cos_t = jnp.concatenate([cos, cos, ones], axis=-1)
    sin_lo = jnp.concatenate([-sin, zh, zeros], axis=-1)
    sin_hi = jnp.concatenate([zh, sin, zeros], axis=-1)
    tile2 = lambda t: jnp.concatenate([t, t], axis=-1)
    return jnp.stack([tile2(cos_t), tile2(sin_lo), tile2(sin_hi)])


def _layout_cmp_w1(w1):
    hid = w1.shape[1]
    w = w1.reshape(2, CMP_STRIDE, HEAD_DIM, hid)
    z = jnp.zeros_like(w)
    per_head = [jnp.concatenate([w, z], axis=2), jnp.concatenate([z, w], axis=2)]
    return jnp.stack(per_head).reshape(N_KV_NSA, 2, CMP_STRIDE * LANES, hid).astype(BF16)


def _layout_cmp_pos(pos):
    p = pos.reshape(2, CMP_STRIDE, HEAD_DIM)
    p = jnp.concatenate([p, p], axis=-1).reshape(2, 1, CMP_STRIDE * LANES)
    return jnp.broadcast_to(p, (2, 8, CMP_STRIDE * LANES)).astype(BF16)


def _overlap_t(seq):
    rows = seq // CMP_STRIDE
    n_sel = seq // SEL_BLOCK
    cmp_start = np.arange(rows) * CMP_STRIDE
    sel_start = np.arange(n_sel) * SEL_BLOCK
    ov = ((cmp_start[None, :] <= sel_start[:, None] + SEL_BLOCK - 1)
          & (cmp_start[None, :] + CMP_BLOCK - 1 >= sel_start[:, None]))
    ov[:, (seq - CMP_BLOCK) // CMP_STRIDE + 1:] = False
    return jnp.asarray(ov.astype(np.float32), BF16)


def _row(v, width=None):
    v = v.reshape(1, -1).astype(F32)
    return v if width is None else _pad_cols(v, width)


def kernel(x, p, ffn1_norm, ffn1_wg, ffn1_wu, ffn1_wd, mix_norm, w_in, b_forget, q_norm_nsa, k_norm_cmp, k_norm_slc, k_norm_win, cmp_pos_k, cmp_pos_v, cmp_k_w1, cmp_k_w2, cmp_v_w1, cmp_v_w2, q_norm_fox, k_norm_fox, out_norm_nsa, out_norm_fox, w_out, ffn2_norm, ffn2_wg, ffn2_wu, ffn2_wd, ple_gate_norm, ple_w_gate, ple_w_proj, ple_norm):
    b, s, d = x.shape
    assert s // SEL_BLOCK <= HEAD_DIM and (s // SEL_BLOCK) % 8 == 0
    assert s % max(NSA_TK, FOX_TK, FOX_TQ) == 0
    depth = ffn1_norm.shape[0]
    t = b * s
    rope_tab = _rope_tables(s)
    ov_t = _overlap_t(s)
    tile4 = lambda g: jnp.concatenate([g] * 4).reshape(1, 2 * LANES).astype(F32)
    h = x.reshape(t, d)
    for i in range(depth):
        h = _ffn(h, _row(ffn1_norm[i]), ffn1_wg[i].astype(BF16), ffn1_wu[i].astype(BF16),
                 ffn1_wd[i].astype(BF16))

        gains = jnp.concatenate([tile4(q_norm_nsa[i]), tile4(k_norm_slc[i]), tile4(k_norm_win[i]),
                                 tile4(q_norm_fox[i]), tile4(k_norm_fox[i]),
                                 jnp.zeros((3, 2 * LANES), F32)], axis=0)
        (qat, kc, vc, kslx, vslt, kwnx, vwnt, gates_t, fqt, fkx, fvt, cq) = _inproj(
            h.reshape(b, s, d), _row(mix_norm[i]), _layout_w_in(w_in[i]),
            _row(b_forget[i], LANES), gains, rope_tab)

        rows = s // CMP_STRIDE

        def shifted(tok):
            r0 = tok.reshape(b, rows, CMP_STRIDE * LANES)
            r1 = jnp.concatenate([r0[:, 1:], jnp.zeros_like(r0[:, :1])], axis=1)
            return r0, r1

        r0k, r1k = shifted(kc)
        r0v, r1v = shifted(vc)
        kcx, vcx = _compress(
            r0k, r1k, r0v, r1v, _layout_cmp_w1(cmp_k_w1[i]), _layout_cmp_w1(cmp_v_w1[i]),
            _layout_cmp_pos(cmp_pos_k[i]), _layout_cmp_pos(cmp_pos_v[i]),
            _pad_cols(cmp_k_w2[i], LANES).astype(BF16), _pad_cols(cmp_v_w2[i], LANES).astype(BF16),
            _row(k_norm_cmp[i], LANES))
        vct = jnp.transpose(vcx[..., :HEAD_DIM], (0, 1, 3, 2))

        o_a = _nsa(qat, kcx, vct, kslx, vslt, kwnx, vwnt, gates_t, ov_t)
        o_b = _fox(fqt, fkx, fvt, cq)

        w_o = w_out[i].astype(BF16)
        h = _outproj(h, o_a.reshape(t, NSA_Q_W), o_b.reshape(t, FOX_W),
                     _row(out_norm_nsa[i]), _row(out_norm_fox[i]), w_o[:NSA_Q_W], w_o[NSA_Q_W:])

        h = _ffn(h, _row(ffn2_norm[i]), ffn2_wg[i].astype(BF16), ffn2_wu[i].astype(BF16),
                 ffn2_wd[i].astype(BF16))

        h = _ple(h, p[i].reshape(t, -1), _row(ple_gate_norm[i]), _row(ple_norm[i]),
                 ple_w_gate[i].astype(BF16), ple_w_proj[i].astype(BF16))
    return h.reshape(b, s, d)
```

```python
import numpy as np
import jax
import jax.numpy as jnp
from jax import lax
from jax.experimental import pallas as pl
from jax.experimental.pallas import tpu as pltpu

F32 = jnp.float32
BF16 = jnp.bfloat16

D_MODEL = 1024
HEAD_DIM = 64
N_HEADS_NSA = 8
N_KV_NSA = 2
GROUP_NSA = N_HEADS_NSA // N_KV_NSA
N_HEADS_FOX = 8
NSA_Q_W = N_HEADS_NSA * HEAD_DIM
NSA_KV_W = N_KV_NSA * HEAD_DIM
FOX_W = N_HEADS_FOX * HEAD_DIM
PROJ_SIZES = (NSA_Q_W, NSA_KV_W, NSA_KV_W, NSA_KV_W, NSA_KV_W, NSA_KV_W, NSA_KV_W,
              3 * N_HEADS_NSA, FOX_W, FOX_W, FOX_W, N_HEADS_FOX)
D_FF = 2816
D_PLE = 256
ROPE_THETA = 500000.0
ROT_DIM = HEAD_DIM // 4
CMP_BLOCK = 32
CMP_STRIDE = 16
CMP_HIDDEN = 256
SEL_BLOCK = 64
SEL_SHIFT = 6
SEL_TOPK = 16
WINDOW = 512
FORCED_SCORE = 1e6
EPS = 1e-6
NEG = -1e30
LOG2E = 1.4426950408889634

LANES = 128
BF16_ROWS = 16
VMEM_LIMIT = 56 * 1024 * 1024
AUG_STRIDE = 8

C_QA = 0
C_KCVC = C_QA + NSA_Q_W
C_KSL = C_KCVC + 2 * LANES
C_KWN = C_KSL + N_KV_NSA * LANES
C_VSLWN = C_KWN + N_KV_NSA * LANES
C_GATE = C_VSLWN + 2 * LANES
C_FQ = C_GATE + N_KV_NSA * LANES
C_FK = C_FQ + FOX_W
C_FV = C_FK + FOX_W
C_FL = C_FV + FOX_W
C_END = C_FL + 2 * LANES

FFN_TM = 512
FFN_TF = 256
PROJ_TS = 512
NSA_TQ = 256
NSA_TK = 256
FOX_TQ = 512
FOX_TK = 256


def _dot(a, b):
    return jnp.dot(a, b, preferred_element_type=F32)


def _dot_nt(a, b):
    return lax.dot_general(a, b, (((1,), (1,)), ((), ())), preferred_element_type=F32)


def _params(*sem):
    return pltpu.CompilerParams(dimension_semantics=sem, vmem_limit_bytes=VMEM_LIMIT)


def _rms(x):
    return lax.rsqrt(jnp.mean(x * x, axis=-1, keepdims=True) + EPS)


def _split3(x):
    hi = x.astype(BF16)
    r1 = x - hi.astype(F32)
    mid = r1.astype(BF16)
    lo = (r1 - mid.astype(F32)).astype(BF16)
    return hi, mid, lo


def _swiglu_half_step(x, g_ref, wg_ref, wu_ref, wd_ref):
    xn = (x * _rms(x) * g_ref[...]).astype(BF16)
    acc = None
    for j in range(wg_ref.shape[1] // FFN_TF):
        sl = slice(j * FFN_TF, (j + 1) * FFN_TF)
        gate = _dot(xn, wg_ref[:, sl])
        up = _dot(xn, wu_ref[:, sl])
        act = (gate * jax.nn.sigmoid(gate) * up).astype(BF16)
        part = _dot(act, wd_ref[sl, :])
        acc = part if acc is None else acc + part
    return x + 0.5 * acc


def _ffn_kernel(x_ref, g_ref, wg_ref, wu_ref, wd_ref, o_ref):
    o_ref[...] = _swiglu_half_step(x_ref[...], g_ref, wg_ref, wu_ref, wd_ref)


def _resident(a):
    return pl.BlockSpec(a.shape, lambda *_: (0,) * a.ndim, pipeline_mode=pl.Buffered(1))


def _ffn(x, g, wg, wu, wd):
    t, d = x.shape
    tm = min(FFN_TM, t)
    rows = pl.BlockSpec((tm, d), lambda i: (i, 0))
    return pl.pallas_call(
        _ffn_kernel,
        grid=(t // tm,),
        in_specs=[rows, _resident(g), _resident(wg), _resident(wu), _resident(wd)],
        out_specs=rows,
        out_shape=jax.ShapeDtypeStruct((t, d), F32),
        compiler_params=_params("parallel"),
        name="ffn",
    )(x, g, wg, wu, wd)


def _log_sigmoid(x):
    return -(jnp.maximum(-x, 0.0) + jnp.log(1.0 + jnp.exp(-jnp.abs(x))))


def _inproj_kernel(h_ref, g_ref, w_ref, bf_ref, gains_ref, rope_ref,
                   qa_ref, kc_ref, vc_ref, ksl_ref, vsl_ref, kwn_ref, vwn_ref,
                   gate_ref, fq_ref, fk_ref, fv_ref, cum_ref, carry_ref):
    si = pl.program_id(1)
    ts = h_ref.shape[1]
    wide = 2 * LANES
    x = h_ref[0]
    a = (x * _rms(x) * g_ref[...]).astype(BF16)

    row = lax.broadcasted_iota(jnp.int32, (2 * wide, wide), 0) & (wide - 1)
    col = lax.broadcasted_iota(jnp.int32, (2 * wide, wide), 1)
    ones_head = jnp.where(jnp.right_shift(row, 6) == jnp.right_shift(col, 6), 1.0, 0.0).astype(BF16)
    ones_chunk = jnp.where(jnp.right_shift(row, 7) == jnp.right_shift(col, 7), 1.0, 0.0).astype(BF16)
    cos_t, sin_lo, sin_hi = rope_ref[0], rope_ref[1], rope_ref[2]

    def proj(c0):
        return _dot(a, w_ref[:, c0:c0 + wide])

    def norm(u, ones_mat, gain_row):
        u2 = u * u
        hi = u2.astype(BF16)
        lo = (u2 - hi.astype(F32)).astype(BF16)
        ms = _dot(jnp.concatenate([hi, lo], axis=1), ones_mat) * (1.0 / HEAD_DIM)
        return u * lax.rsqrt(ms + EPS) * gains_ref[gain_row:gain_row + 1, :]

    def rope_half(u):
        return (u * cos_t + pltpu.roll(u, LANES - ROT_DIM // 2, 1) * sin_lo
                + pltpu.roll(u, ROT_DIM // 2, 1) * sin_hi)

    def rope(u):
        return jnp.concatenate([rope_half(u[:, :LANES]), rope_half(u[:, LANES:])], axis=1)

    def feature_major(u):
        return jnp.transpose(u.astype(BF16))

    scale = HEAD_DIM ** -0.5 * LOG2E
    for c in range(NSA_Q_W // wide):
        sl = slice(c * wide, (c + 1) * wide)
        qa_ref[0, sl, :] = feature_major(rope(norm(proj(C_QA + c * wide), ones_head, 0)) * scale)

    u = proj(C_KCVC)
    kc_ref[0] = rope_half(u[:, :LANES]).astype(BF16)
    vc_ref[0] = u[:, LANES:].astype(BF16)
    u = proj(C_VSLWN)
    u = feature_major(u)
    vsl_ref[0] = u[:LANES]
    vwn_ref[0] = u[LANES:]

    pos = si * ts + lax.broadcasted_iota(jnp.int32, (ts, wide), 0)
    lane = lax.broadcasted_iota(jnp.int32, (ts, wide), 1) & (LANES - 1)
    sel_onehot = jnp.where(lane == HEAD_DIM + jnp.right_shift(pos, SEL_SHIFT), 1.0, 0.0)
    ksl_ref[0] = (rope(norm(proj(C_KSL), ones_chunk, 1)) + sel_onehot).astype(BF16)
    kwn_ref[0] = rope(norm(proj(C_KWN), ones_chunk, 2)).astype(BF16)
    gate_ref[0] = jnp.transpose(jax.nn.sigmoid(proj(C_GATE)))

    @pl.when(si == 0)
    def _():
        carry_ref[...] = jnp.zeros_like(carry_ref)

    lf = _log_sigmoid(proj(C_FL)[:, :LANES] + bf_ref[...])
    r_i = lax.broadcasted_iota(jnp.int32, (ts, ts), 0)
    c_i = lax.broadcasted_iota(jnp.int32, (ts, ts), 1)
    tri = jnp.where(r_i >= c_i, 1.0, 0.0).astype(BF16)
    hi, mid, lo = _split3(lf)
    c = _dot(tri, hi) + _dot(tri, mid) + _dot(tri, lo) + carry_ref[0:1, :]
    carry_ref[...] = jnp.broadcast_to(c[ts - 1:ts, :], carry_ref.shape)
    c = c * LOG2E
    cum_ref[0] = jnp.transpose(c)[:N_HEADS_FOX]

    n_pairs = N_HEADS_FOX // 2
    p_row = lax.broadcasted_iota(jnp.int32, (3 * LANES, n_pairs * LANES), 0)
    p_col = lax.broadcasted_iota(jnp.int32, (3 * LANES, n_pairs * LANES), 1)
    head = p_row & (LANES - 1)
    part = jnp.right_shift(p_row, 7)
    target = jnp.right_shift(head, 1) * LANES + (head & 1) * AUG_STRIDE + part
    place = jnp.where((p_col == target) & (head < N_HEADS_FOX), 1.0, 0.0).astype(BF16)
    aug = _dot(jnp.concatenate(_split3(-c), axis=1), place)

    for c2 in range(FOX_W // wide):
        sl = slice(c2 * wide, (c2 + 1) * wide)
        fq_ref[0, sl, :] = feature_major(norm(proj(C_FQ + c2 * wide), ones_head, 3) * scale)
        fv_ref[0, sl, :] = feature_major(proj(C_FV + c2 * wide))
        fk = norm(proj(C_FK + c2 * wide), ones_head, 4).astype(BF16)
        for e in range(2):
            pair = 2 * c2 + e
            fk_ref[0, :, 2 * pair * LANES:(2 * pair + 1) * LANES] = fk[:, e * LANES:(e + 1) * LANES]
            fk_ref[0, :, (2 * pair + 1) * LANES:(2 * pair + 2) * LANES] = (
                aug[:, pair * LANES:(pair + 1) * LANES].astype(BF16))


def _inproj(h, g, w_all, bf_row, gains, rope_tab):
    b, s, d = h.shape
    ts = min(PROJ_TS, s)
    tok = lambda c: pl.BlockSpec((1, ts, c), lambda i, j: (i, j, 0))
    const = lambda shape: pl.BlockSpec(shape, lambda i, j: (0,) * len(shape))
    feat = lambda c: pl.BlockSpec((1, c, ts), lambda i, j: (i, 0, j))
    kvx = N_KV_NSA * LANES
    specs = [(NSA_Q_W, BF16, True), (LANES, BF16, False), (LANES, BF16, False), (kvx, BF16, False),
             (LANES, BF16, True), (kvx, BF16, False), (LANES, BF16, True), (kvx, F32, True),
             (FOX_W, BF16, True), (2 * FOX_W, BF16, False), (FOX_W, BF16, True),
             (N_HEADS_FOX, F32, True)]
    out_shape = [jax.ShapeDtypeStruct((b, c, s) if fm else (b, s, c), dt) for c, dt, fm in specs]
    return pl.pallas_call(
        _inproj_kernel,
        grid=(b, s // ts),
        in_specs=[
            tok(d),
            const((1, d)),
            const((d, C_END)),
            const((1, LANES)),
            const((8, 2 * LANES)),
            pl.BlockSpec((3, ts, LANES), lambda i, j: (0, j, 0)),
        ],
        out_specs=[feat(c) if fm else tok(c) for c, _, fm in specs],
        out_shape=out_shape,
        scratch_shapes=[pltpu.VMEM((8, LANES), F32)],
        compiler_params=_params("parallel", "arbitrary"),
        name="inproj",
    )(h, g, w_all, bf_row, gains, rope_tab)


def _compress_kernel(r0k_ref, r1k_ref, r0v_ref, r1v_ref, w1k_ref, w1v_ref,
                     pk_ref, pv_ref, w2k_ref, w2v_ref, gain_ref, kc_ref, vc_ref):
    def mlp(r0_ref, r1_ref, w1_ref, p_ref, w2_ref, kh):
        top, bot = w1_ref[kh, 0], w1_ref[kh, 1]
        bias = _dot(p_ref[0], top) + _dot(p_ref[1], bot)
        hid = _dot(r0_ref[0], top) + _dot(r1_ref[0], bot) + bias[0:1, :]
        hid = hid * jax.nn.sigmoid(hid)
        return _dot(hid.astype(BF16), w2_ref[...])

    for kh in range(N_KV_NSA):
        kc = mlp(r0k_ref, r1k_ref, w1k_ref, pk_ref, w2k_ref, kh)
        ms = jnp.sum(kc * kc, axis=-1, keepdims=True) * (1.0 / HEAD_DIM)
        kc_ref[0, kh] = (kc * lax.rsqrt(ms + EPS) * gain_ref[...]).astype(BF16)
        vc_ref[0, kh] = mlp(r0v_ref, r1v_ref, w1v_ref, pv_ref, w2v_ref, kh).astype(BF16)


def _compress(r0k, r1k, r0v, r1v, w1k, w1v, pk, pv, w2k, w2v, gain):
    b, r, w = r0k.shape
    rows = pl.BlockSpec((1, r, w), lambda i: (i, 0, 0))
    const = lambda a: pl.BlockSpec(a.shape, lambda i: (0,) * a.ndim)
    out = jax.ShapeDtypeStruct((b, N_KV_NSA, r, LANES), BF16)
    out_spec = pl.BlockSpec((1, N_KV_NSA, r, LANES), lambda i: (i, 0, 0, 0))
    return pl.pallas_call(
        _compress_kernel,
        grid=(b,),
        in_specs=[rows, rows, rows, rows, const(w1k), const(w1v), const(pk), const(pv),
                  const(w2k), const(w2v), const(gain)],
        out_specs=[out_spec, out_spec],
        out_shape=[out, out],
        compiler_params=_params("parallel"),
        name="compress",
    )(r0k, r1k, r0v, r1v, w1k, w1v, pk, pv, w2k, w2v, gain)


def _ones_rows(width):
    return jnp.ones((BF16_ROWS, width), BF16)


def _pipelined_sweep(n_full, n_tail, tk, scores, masked, consume, s_ref, mt_ref):
    def stash(s_list):
        for e, s in enumerate(s_list):
            s_ref[e] = s
            mt_ref[e] = jnp.max(s, axis=0, keepdims=True)

    def body(j, carry):
        k0 = pl.multiple_of(j * tk, tk)
        s_new = scores(k0)
        consume(pl.multiple_of(k0 - tk, tk))
        stash(s_new)
        return carry

    def tail_step(k0):
        s_new = scores(k0)
        consume(pl.multiple_of(k0 - tk, tk))
        stash(masked(s_new, k0))

    k_tail = pl.multiple_of(n_full * tk, tk)

    @pl.when(n_full == 0)
    def _():
        stash(masked(scores(k_tail), k_tail))

    @pl.when(n_full > 0)
    def _():
        stash(scores(0))
        lax.fori_loop(1, n_full, body, 0)
        tail_step(k_tail)

    for i in range(1, n_tail):
        tail_step(pl.multiple_of(k_tail + i * tk, tk))
    consume(pl.multiple_of(k_tail + (n_tail - 1) * tk, tk))


def _nsa_kernel(qt_ref, kc_ref, vct_ref, ksl_ref, vslt_ref, kwn_ref, vwnt_ref, gt_ref, ov_ref,
                o_ref, qaug_ref, m_ref, acc_ref, s_ref, mt_ref):
    qi = pl.program_id(2)
    tq = qt_ref.shape[2]
    s_len = ksl_ref.shape[1]
    n_cmp_rows = kc_ref.shape[2]
    n_sel = ov_ref.shape[0]
    top = min(SEL_TOPK, n_sel)
    grp = GROUP_NSA
    cols = grp * tq
    tk = min(NSA_TK, s_len)
    t0 = qi * tq

    def tok_of_col(rows):
        return t0 + (lax.broadcasted_iota(jnp.int32, (rows, cols), 1) & (tq - 1))

    def tok_of_tile(rows):
        return t0 + lax.broadcasted_iota(jnp.int32, (rows, tq), 1)

    def softmax_cols(s, keep):
        s = s + jnp.concatenate([jnp.where(keep, 0.0, NEG)] * grp, axis=1)
        mx = jnp.max(s, axis=0, keepdims=True)
        e = jnp.exp2(s - mx)
        den = jnp.sum(e, axis=0, keepdims=True)
        return e, jnp.where(mx > 0.5 * NEG, 1.0 / den, 0.0)

    qt = qt_ref[0]
    qaug_ref[0:HEAD_DIM, :] = jnp.concatenate(
        [qt[g * HEAD_DIM:(g + 1) * HEAD_DIM] for g in range(grp)], axis=1)
    qaug_ref[HEAD_DIM:, :] = jnp.zeros((HEAD_DIM, cols), BF16)

    n_c = lax.broadcasted_iota(jnp.int32, (n_cmp_rows, tq), 0)
    keep_c = n_c * CMP_STRIDE + (CMP_BLOCK - 1) <= tok_of_tile(n_cmp_rows)
    e_c, inv_c = softmax_cols(_dot(kc_ref[0, 0], qaug_ref[...]), keep_c)
    o_c = _dot(vct_ref[0, 0], e_c.astype(BF16)) * inv_c

    p_sum = e_c[:, 0:tq] * inv_c[:, 0:tq]
    for g in range(1, grp):
        p_sum = p_sum + e_c[:, g * tq:(g + 1) * tq] * inv_c[:, g * tq:(g + 1) * tq]
    p_hi = p_sum.astype(BF16)
    p_lo = (p_sum - p_hi.astype(F32)).astype(BF16)
    ov = ov_ref[...]
    imp = _dot(ov, p_hi) + _dot(ov, p_lo)
    j_blk = lax.broadcasted_iota(jnp.int32, (n_sel, tq), 0)
    cur = jnp.right_shift(t0 + lax.broadcasted_iota(jnp.int32, (n_sel, tq), 1), SEL_SHIFT)
    forced = (j_blk == 0) | (j_blk == cur) | (j_blk == cur - 1)
    imp = jnp.where(j_blk <= cur, jnp.where(forced, FORCED_SCORE, imp), -1.0)
    sub = lax.broadcasted_iota(jnp.int32, (8, tq), 0)
    groups = [imp[8 * v:8 * v + 8] for v in range(n_sel // 8)]
    ranks = [jnp.zeros((8, tq), F32) for _ in groups]
    for i in range(n_sel):
        r_i = jnp.broadcast_to(imp[i:i + 1, :], (8, tq))
        for v, g_v in enumerate(groups):
            if i < 8 * v:
                beats = r_i >= g_v
            elif i >= 8 * v + 8:
                beats = r_i > g_v
            else:
                beats = (r_i > g_v) | ((r_i == g_v) & (sub > i - 8 * v))
            ranks[v] = ranks[v] + jnp.where(beats, 1.0, 0.0)
    chosen = (jnp.concatenate(ranks, axis=0) < top) & (imp >= 0.0)
    bias_t = jnp.where(chosen, 0.0, NEG)
    if n_sel < HEAD_DIM:
        bias_t = jnp.concatenate([bias_t, jnp.zeros((HEAD_DIM - n_sel, tq), F32)], axis=0)
    qaug_ref[HEAD_DIM:, :] = jnp.concatenate([bias_t.astype(BF16)] * grp, axis=1)

    m_ref[...] = jnp.full_like(m_ref, NEG)
    acc_ref[...] = jnp.zeros_like(acc_ref)

    def slc_scores(k0):
        return [_dot(ksl_ref[0, pl.ds(k0, tk), :], qaug_ref[...])]

    def slc_masked(s_list, k0):
        key = k0 + lax.broadcasted_iota(jnp.int32, (tk, tq), 0)
        bias = jnp.where(key <= tok_of_tile(tk), 0.0, NEG)
        return [s_list[0] + jnp.concatenate([bias] * grp, axis=1)]

    def slc_consume(k0):
        m_old = m_ref[...]
        m_new = jnp.maximum(m_old, mt_ref[0])
        p = jnp.exp2(s_ref[0] - m_new).astype(BF16)
        vt = jnp.concatenate([vslt_ref[0, :, pl.ds(k0, tk)], _ones_rows(tk)], axis=0)
        acc_ref[...] = jnp.exp2(m_old - m_new) * acc_ref[...] + _dot(vt, p)
        m_ref[...] = m_new

    _pipelined_sweep(t0 // tk, max(1, tq // tk), tk, slc_scores, slc_masked, slc_consume, s_ref, mt_ref)
    o_s = acc_ref[0:HEAD_DIM, :] * (1.0 / acc_ref[HEAD_DIM:HEAD_DIM + 1, :])

    span = min(WINDOW + tq, s_len)
    start = pl.multiple_of(jnp.maximum(t0 + tq - span, 0), LANES)
    t_w = tok_of_tile(span)
    k_w = start + lax.broadcasted_iota(jnp.int32, (span, tq), 0)
    keep_w = (k_w <= t_w) & (t_w - k_w < WINDOW)
    e_w, inv_w = softmax_cols(_dot(kwn_ref[0, pl.ds(start, span), :], qaug_ref[...]), keep_w)
    o_w = _dot(vwnt_ref[0, :, pl.ds(start, span)], e_w.astype(BF16)) * inv_w

    gt = gt_ref[0]
    outs = []
    for g in range(grp):
        sl = slice(g * tq, (g + 1) * tq)
        outs.append(gt[3 * g:3 * g + 1] * o_c[:, sl] + gt[3 * g + 1:3 * g + 2] * o_s[:, sl]
                    + gt[3 * g + 2:3 * g + 3] * o_w[:, sl])
    o_ref[0] = jnp.transpose(jnp.concatenate(outs, axis=0))


def _nsa(qat, kcx, vct, kslx, vslt, kwnx, vwnt, gates_t, ov_t):
    b, _, s = qat.shape
    tq = min(NSA_TQ, s)
    assert tq & (tq - 1) == 0 and tq % LANES == 0
    r = kcx.shape[2]
    gw = GROUP_NSA * HEAD_DIM
    cols = GROUP_NSA * tq
    gate_rows = 16
    key_major = pl.BlockSpec((1, s, LANES), lambda i, k, j: (i, 0, k))
    val_major = pl.BlockSpec((1, HEAD_DIM, s), lambda i, k, j: (i, k, 0))
    return pl.pallas_call(
        _nsa_kernel,
        grid=(b, N_KV_NSA, s // tq),
        in_specs=[
            pl.BlockSpec((1, gw, tq), lambda i, k, j: (i, k, j)),
            pl.BlockSpec((1, 1, r, LANES), lambda i, k, j: (i, k, 0, 0)),
            pl.BlockSpec((1, 1, HEAD_DIM, r), lambda i, k, j: (i, k, 0, 0)),
            key_major, val_major, key_major, val_major,
            pl.BlockSpec((1, gate_rows, tq), lambda i, k, j: (i, k * (LANES // gate_rows), j)),
            pl.BlockSpec(ov_t.shape, lambda i, k, j: (0, 0)),
        ],
        out_specs=pl.BlockSpec((1, tq, gw), lambda i, k, j: (i, j, k)),
        out_shape=jax.ShapeDtypeStruct((b, s, NSA_Q_W), F32),
        scratch_shapes=[
            pltpu.VMEM((LANES, cols), BF16),
            pltpu.VMEM((1, cols), F32),
            pltpu.VMEM((HEAD_DIM + BF16_ROWS, cols), F32),
            pltpu.VMEM((1, min(NSA_TK, s), cols), F32),
            pltpu.VMEM((1, 1, cols), F32),
        ],
        compiler_params=_params("parallel", "parallel", "arbitrary"),
        name="nsa",
    )(qat, kcx, vct, kslx, vslt, kwnx, vwnt, gates_t, ov_t)


def _fox_kernel(qt_ref, k_ref, vt_ref, cq_ref, o_ref, qaug_ref, m_ref, acc_ref, s_ref, mt_ref):
    pair = pl.program_id(1)
    qi = pl.program_id(2)
    tq = qt_ref.shape[2]
    tk = min(FOX_TK, k_ref.shape[1])
    t0 = qi * tq
    qt = qt_ref[0]
    row = lax.broadcasted_iota(jnp.int32, (LANES, tq), 0)
    zero = jnp.zeros_like(qt)
    cq = []
    for e in range(2):
        qaug_ref[e, 0:LANES, :] = jnp.where((row >= e * HEAD_DIM) & (row < (e + 1) * HEAD_DIM), qt, zero)
        pick = (row >= e * AUG_STRIDE) & (row < e * AUG_STRIDE + 3)
        qaug_ref[e, LANES:, :] = jnp.where(pick, 1.0, 0.0).astype(BF16)
        cq.append(cq_ref[0, pl.ds(2 * pair + e, 1), :])

    m_ref[...] = jnp.full_like(m_ref, NEG)
    acc_ref[...] = jnp.zeros_like(acc_ref)

    def scores(k0):
        kt = k_ref[0, pl.ds(k0, tk), :]
        return [_dot(kt, qaug_ref[e]) for e in range(2)]

    def masked(s_list, k0):
        key = k0 + lax.broadcasted_iota(jnp.int32, (tk, tq), 0)
        tok = t0 + lax.broadcasted_iota(jnp.int32, (tk, tq), 1)
        return [jnp.where(key <= tok, sv, NEG) for sv in s_list]

    def consume(k0):
        for e in range(2):
            m_old = m_ref[e]
            m_new = jnp.maximum(m_old, mt_ref[e] + cq[e])
            p = jnp.exp2(s_ref[e] + (cq[e] - m_new)).astype(BF16)
            vt = jnp.concatenate([vt_ref[0, e * HEAD_DIM:(e + 1) * HEAD_DIM, pl.ds(k0, tk)],
                                  _ones_rows(tk)], axis=0)
            acc_ref[e] = jnp.exp2(m_old - m_new) * acc_ref[e] + _dot(vt, p)
            m_ref[e] = m_new

    _pipelined_sweep(t0 // tk, max(1, tq // tk), tk, scores, masked, consume, s_ref, mt_ref)
    o_ref[0] = jnp.transpose(jnp.concatenate(
        [acc_ref[e, 0:HEAD_DIM, :] * (1.0 / acc_ref[e, HEAD_DIM:HEAD_DIM + 1, :]) for e in range(2)], axis=0))


def _fox(fqt, fkx, fvt, cq):
    b, _, s = fqt.shape
    tq = min(FOX_TQ, s)
    return pl.pallas_call(
        _fox_kernel,
        grid=(b, N_HEADS_FOX // 2, s // tq),
        in_specs=[
            pl.BlockSpec((1, LANES, tq), lambda i, p, j: (i, p, j)),
            pl.BlockSpec((1, s, 2 * LANES), lambda i, p, j: (i, 0, p)),
            pl.BlockSpec((1, LANES, s), lambda i, p, j: (i, p, 0)),
            pl.BlockSpec((1, N_HEADS_FOX, tq), lambda i, p, j: (i, 0, j)),
        ],
        out_specs=pl.BlockSpec((1, tq, LANES), lambda i, p, j: (i, j, p)),
        out_shape=jax.ShapeDtypeStruct((b, s, FOX_W), F32),
        scratch_shapes=[
            pltpu.VMEM((2, 2 * LANES, tq), BF16),
            pltpu.VMEM((2, 1, tq), F32),
            pltpu.VMEM((2, HEAD_DIM + BF16_ROWS, tq), F32),
            pltpu.VMEM((2, min(FOX_TK, s), tq), F32),
            pltpu.VMEM((2, 1, tq), F32),
        ],
        compiler_params=_params("parallel", "parallel", "arbitrary"),
        name="fox",
    )(fqt, fkx, fvt, cq)


def _tail_kernel(h_ref, oa_ref, ob_ref, p_ref, ga_ref, gb_ref, wa_ref, wb_ref,
                 g_ref, wg_ref, wu_ref, wd_ref, gg_ref, gp_ref, wgate_ref, wproj_ref, o_ref):
    oa = oa_ref[...]
    ob = ob_ref[...]
    na = (oa * _rms(oa) * ga_ref[...]).astype(BF16)
    nb = (ob * _rms(ob) * gb_ref[...]).astype(BF16)
    h = h_ref[...] + _dot(na, wa_ref[...]) + _dot(nb, wb_ref[...])
    h = _swiglu_half_step(h, g_ref, wg_ref, wu_ref, wd_ref)
    hn = (h * _rms(h) * gg_ref[...]).astype(BF16)
    gate = jax.nn.sigmoid(_dot(hn, wgate_ref[...]))
    e = _dot(p_ref[...].astype(BF16), wproj_ref[...])
    o_ref[...] = h + gate * (e * _rms(e) * gp_ref[...])


def _tail(h, oa, ob, p, ga, gb, wa, wb, g, wg, wu, wd, gg, gp, wgate, wproj):
    t, d = h.shape
    tm = min(FFN_TM, t)
    rows = lambda c: pl.BlockSpec((tm, c), lambda i: (i, 0))
    consts = (ga, gb, wa, wb, g, wg, wu, wd, gg, gp, wgate, wproj)
    return pl.pallas_call(
        _tail_kernel,
        grid=(t // tm,),
        in_specs=[rows(d), rows(oa.shape[1]), rows(ob.shape[1]), rows(p.shape[1])]
        + [_resident(c) for c in consts],
        out_specs=rows(d),
        out_shape=jax.ShapeDtypeStruct((t, d), F32),
        compiler_params=_params("parallel"),
        name="tail",
    )(h, oa, ob, p, *consts)


def _expand_heads(w, n):
    w = w.reshape(w.shape[0], n, HEAD_DIM)
    return jnp.concatenate([w, jnp.zeros_like(w)], axis=-1).reshape(w.shape[0], n * LANES)


def _pad_cols(w, width):
    return jnp.pad(w, ((0, 0), (0, width - w.shape[1])))


def _layout_w_in(w_in):
    splits = [int(v) for v in np.cumsum(PROJ_SIZES)[:-1]]
    qa, kc, vc, ksl, vsl, kwn, vwn, ga, qf, kf, vf, fl = jnp.split(w_in, splits, axis=-1)
    per_group = 3 * GROUP_NSA
    ga_x = jnp.concatenate([_pad_cols(ga[:, k * per_group:(k + 1) * per_group], LANES)
                            for k in range(N_KV_NSA)], axis=-1)
    cols = [qa, kc, vc, _expand_heads(ksl, N_KV_NSA), _expand_heads(kwn, N_KV_NSA), vsl, vwn,
            ga_x, qf, kf, vf, _pad_cols(fl, 2 * LANES)]
    return jnp.concatenate(cols, axis=-1).astype(BF16)


def _rope_tables(seq):
    pos = jnp.arange(seq, dtype=F32)
    inv = ROPE_THETA ** (-jnp.arange(0, ROT_DIM, 2, dtype=F32) / ROT_DIM)
    ang = pos[:, None] * inv[None, :]
    cos, sin = jnp.cos(ang), jnp.sin(ang)
    half = ROT_DIM // 2
    rest = HEAD_DIM - ROT_DIM
    ones = jnp.ones((seq, rest), F32)
    zeros = jnp.zeros((seq, rest), F32)
    zh = jnp.zeros((seq, half), F32)
    cos_t = jnp.concatenate([cos, cos, ones], axis=-1)
    sin_lo = jnp.concatenate([-sin, zh, zeros], axis=-1)
    sin_hi = jnp.concatenate([zh, sin, zeros], axis=-1)
    tile2 = lambda t: jnp.concatenate([t, t], axis=-1)
    return jnp.stack([tile2(cos_t), tile2(sin_lo), tile2(sin_hi)])


def _layout_cmp_w1(w1):
    hid = w1.shape[1]
    w = w1.reshape(2, CMP_STRIDE, HEAD_DIM, hid)
    z = jnp.zeros_like(w)
    per_head = [jnp.concatenate([w, z], axis=2), jnp.concatenate([z, w], axis=2)]
    return jnp.stack(per_head).reshape(N_KV_NSA, 2, CMP_STRIDE * LANES, hid).astype(BF16)


def _layout_cmp_pos(pos):
    p = pos.reshape(2, CMP_STRIDE, HEAD_DIM)
    p = jnp.concatenate([p, p], axis=-1).reshape(2, 1, CMP_STRIDE * LANES)
    return jnp.broadcast_to(p, (2, 8, CMP_STRIDE * LANES)).astype(BF16)


def _overlap_t(seq):
    rows = seq // CMP_STRIDE
    n_sel = seq // SEL_BLOCK
    cmp_start = np.arange(rows) * CMP_STRIDE
    sel_start = np.arange(n_sel) * SEL_BLOCK
    ov = ((cmp_start[None, :] <= sel_start[:, None] + SEL_BLOCK - 1)
          & (cmp_start[None, :] + CMP_BLOCK - 1 >= sel_start[:, None]))
    ov[:, (seq - CMP_BLOCK) // CMP_STRIDE + 1:] = False
    return jnp.asarray(ov.astype(np.float32), BF16)


def _row(v, width=None):
    v = v.reshape(1, -1).astype(F32)
    return v if width is None else _pad_cols(v, width)


def kernel(x, p, ffn1_norm, ffn1_wg, ffn1_wu, ffn1_wd, mix_norm, w_in, b_forget, q_norm_nsa, k_norm_cmp, k_norm_slc, k_norm_win, cmp_pos_k, cmp_pos_v, cmp_k_w1, cmp_k_w2, cmp_v_w1, cmp_v_w2, q_norm_fox, k_norm_fox, out_norm_nsa, out_norm_fox, w_out, ffn2_norm, ffn2_wg, ffn2_wu, ffn2_wd, ple_gate_norm, ple_w_gate, ple_w_proj, ple_norm):
    b, s, d = x.shape
    assert s // SEL_BLOCK <= HEAD_DIM and (s // SEL_BLOCK) % 8 == 0
    assert s % max(NSA_TK, FOX_TK, FOX_TQ) == 0
    depth = ffn1_norm.shape[0]
    t = b * s
    rope_tab = _rope_tables(s)
    ov_t = _overlap_t(s)
    tile4 = lambda g: jnp.concatenate([g] * 4).reshape(1, 2 * LANES).astype(F32)
    h = x.reshape(t, d)
    for i in range(depth):
        h = _ffn(h, _row(ffn1_norm[i]), ffn1_wg[i].astype(BF16), ffn1_wu[i].astype(BF16),
                 ffn1_wd[i].astype(BF16))

        gains = jnp.concatenate([tile4(q_norm_nsa[i]), tile4(k_norm_slc[i]), tile4(k_norm_win[i]),
                                 tile4(q_norm_fox[i]), tile4(k_norm_fox[i]),
                                 jnp.zeros((3, 2 * LANES), F32)], axis=0)
        (qat, kc, vc, kslx, vslt, kwnx, vwnt, gates_t, fqt, fkx, fvt, cq) = _inproj(
            h.reshape(b, s, d), _row(mix_norm[i]), _layout_w_in(w_in[i]),
            _row(b_forget[i], LANES), gains, rope_tab)

        rows = s // CMP_STRIDE

        def shifted(tok):
            r0 = tok.reshape(b, rows, CMP_STRIDE * LANES)
            r1 = jnp.concatenate([r0[:, 1:], jnp.zeros_like(r0[:, :1])], axis=1)
            return r0, r1

        r0k, r1k = shifted(kc)
        r0v, r1v = shifted(vc)
        kcx, vcx = _compress(
            r0k, r1k, r0v, r1v, _layout_cmp_w1(cmp_k_w1[i]), _layout_cmp_w1(cmp_v_w1[i]),
            _layout_cmp_pos(cmp_pos_k[i]), _layout_cmp_pos(cmp_pos_v[i]),
            _pad_cols(cmp_k_w2[i], LANES).astype(BF16), _pad_cols(cmp_v_w2[i], LANES).astype(BF16),
            _row(k_norm_cmp[i], LANES))
        vct = jnp.transpose(vcx[..., :HEAD_DIM], (0, 1, 3, 2))

        o_a = _nsa(qat, kcx, vct, kslx, vslt, kwnx, vwnt, gates_t, ov_t)
        o_b = _fox(fqt, fkx, fvt, cq)

        w_o = w_out[i].astype(BF16)
        h = _tail(h, o_a.reshape(t, NSA_Q_W), o_b.reshape(t, FOX_W), p[i].reshape(t, -1),
                  _row(out_norm_nsa[i]), _row(out_norm_fox[i]), w_o[:NSA_Q_W], w_o[NSA_Q_W:],
                  _row(ffn2_norm[i]), ffn2_wg[i].astype(BF16), ffn2_wu[i].astype(BF16),
                  ffn2_wd[i].astype(BF16), _row(ple_gate_norm[i]), _row(ple_norm[i]),
                  ple_w_gate[i].astype(BF16), ple_w_proj[i].astype(BF16))
    return h.reshape(b, s, d)
```

```python
import functools

import numpy as np
import jax
import jax.numpy as jnp
from jax import lax
from jax.experimental import pallas as pl
from jax.experimental.pallas import tpu as pltpu

F32 = jnp.float32
BF16 = jnp.bfloat16

D_MODEL = 1024
HEAD_DIM = 64
N_HEADS_NSA = 8
N_KV_NSA = 2
GROUP_NSA = N_HEADS_NSA // N_KV_NSA
N_HEADS_FOX = 8
NSA_Q_W = N_HEADS_NSA * HEAD_DIM
NSA_KV_W = N_KV_NSA * HEAD_DIM
FOX_W = N_HEADS_FOX * HEAD_DIM
PROJ_SIZES = (NSA_Q_W, NSA_KV_W, NSA_KV_W, NSA_KV_W, NSA_KV_W, NSA_KV_W, NSA_KV_W,
              3 * N_HEADS_NSA, FOX_W, FOX_W, FOX_W, N_HEADS_FOX)
D_FF = 2816
D_PLE = 256
ROPE_THETA = 500000.0
ROT_DIM = HEAD_DIM // 4
CMP_BLOCK = 32
CMP_STRIDE = 16
CMP_HIDDEN = 256
SEL_BLOCK = 64
SEL_SHIFT = 6
SEL_TOPK = 16
WINDOW = 512
FORCED_SCORE = 1e6
EPS = 1e-6
NEG = -1e30
LOG2E = 1.4426950408889634

LANES = 128
BF16_ROWS = 16
VMEM_LIMIT = 56 * 1024 * 1024
AUG_STRIDE = 8

C_QA = 0
C_KCVC = C_QA + NSA_Q_W
C_KSL = C_KCVC + 2 * LANES
C_KWN = C_KSL + N_KV_NSA * LANES
C_VSLWN = C_KWN + N_KV_NSA * LANES
C_GATE = C_VSLWN + 2 * LANES
C_FQ = C_GATE + N_KV_NSA * LANES
C_FK = C_FQ + FOX_W
C_FV = C_FK + FOX_W
C_FL = C_FV + FOX_W
C_END = C_FL + 2 * LANES

FFN_TM = 512
FFN_TF = 256
PROJ_TS = 512
NSA_TQ = 256
NSA_TK = 256
FOX_TQ = 512
FOX_TK = 512


def _dot(a, b):
    return jnp.dot(a, b, preferred_element_type=F32)


def _dot_nt(a, b):
    return lax.dot_general(a, b, (((1,), (1,)), ((), ())), preferred_element_type=F32)


def _params(*sem):
    return pltpu.CompilerParams(dimension_semantics=sem, vmem_limit_bytes=VMEM_LIMIT)


def _rms(x):
    return lax.rsqrt(jnp.mean(x * x, axis=-1, keepdims=True) + EPS)


def _split3(x):
    hi = x.astype(BF16)
    r1 = x - hi.astype(F32)
    mid = r1.astype(BF16)
    lo = (r1 - mid.astype(F32)).astype(BF16)
    return hi, mid, lo


def _swiglu_half_step(x, g_ref, wg_ref, wu_ref, wd_ref):
    xn = (x * _rms(x) * g_ref[...]).astype(BF16)
    acc = None
    for j in range(wg_ref.shape[1] // FFN_TF):
        sl = slice(j * FFN_TF, (j + 1) * FFN_TF)
        gate = _dot(xn, wg_ref[:, sl])
        up = _dot(xn, wu_ref[:, sl])
        act = (gate * jax.nn.sigmoid(gate) * up).astype(BF16)
        part = _dot(act, wd_ref[sl, :])
        acc = part if acc is None else acc + part
    return x + 0.5 * acc


def _ffn_kernel(x_ref, g_ref, wg_ref, wu_ref, wd_ref, o_ref):
    o_ref[...] = _swiglu_half_step(x_ref[...], g_ref, wg_ref, wu_ref, wd_ref)


def _resident(a):
    return pl.BlockSpec(a.shape, lambda *_: (0,) * a.ndim, pipeline_mode=pl.Buffered(1))


def _ffn(x, g, wg, wu, wd):
    t, d = x.shape
    tm = min(FFN_TM, t)
    rows = pl.BlockSpec((tm, d), lambda i: (i, 0))
    return pl.pallas_call(
        _ffn_kernel,
        grid=(t // tm,),
        in_specs=[rows, _resident(g), _resident(wg), _resident(wu), _resident(wd)],
        out_specs=rows,
        out_shape=jax.ShapeDtypeStruct((t, d), F32),
        compiler_params=_params("parallel"),
        name="ffn",
    )(x, g, wg, wu, wd)


def _log_sigmoid(x):
    return -(jnp.maximum(-x, 0.0) + jnp.log(1.0 + jnp.exp(-jnp.abs(x))))


def _inproj_kernel(h_ref, g_ref, w_ref, bf_ref, gains_ref, rope_ref,
                   qa_ref, kc_ref, vc_ref, ksl_ref, vsl_ref, kwn_ref, vwn_ref,
                   gate_ref, fq_ref, fk_ref, fv_ref, cum_ref, carry_ref):
    si = pl.program_id(1)
    ts = h_ref.shape[1]
    wide = 2 * LANES
    x = h_ref[0]
    a = (x * _rms(x) * g_ref[...]).astype(BF16)

    row = lax.broadcasted_iota(jnp.int32, (2 * wide, wide), 0) & (wide - 1)
    col = lax.broadcasted_iota(jnp.int32, (2 * wide, wide), 1)
    ones_head = jnp.where(jnp.right_shift(row, 6) == jnp.right_shift(col, 6), 1.0, 0.0).astype(BF16)
    ones_chunk = jnp.where(jnp.right_shift(row, 7) == jnp.right_shift(col, 7), 1.0, 0.0).astype(BF16)
    cos_t, sin_lo, sin_hi = rope_ref[0], rope_ref[1], rope_ref[2]

    def proj(c0):
        return _dot(a, w_ref[:, c0:c0 + wide])

    def norm(u, ones_mat, gain_row):
        u2 = u * u
        hi = u2.astype(BF16)
        lo = (u2 - hi.astype(F32)).astype(BF16)
        ms = _dot(jnp.concatenate([hi, lo], axis=1), ones_mat) * (1.0 / HEAD_DIM)
        return u * lax.rsqrt(ms + EPS) * gains_ref[gain_row:gain_row + 1, :]

    def rope_half(u):
        return (u * cos_t + pltpu.roll(u, LANES - ROT_DIM // 2, 1) * sin_lo
                + pltpu.roll(u, ROT_DIM // 2, 1) * sin_hi)

    def rope(u):
        return jnp.concatenate([rope_half(u[:, :LANES]), rope_half(u[:, LANES:])], axis=1)

    def feature_major(u):
        return jnp.transpose(u.astype(BF16))

    scale = HEAD_DIM ** -0.5 * LOG2E
    for c in range(NSA_Q_W // wide):
        sl = slice(c * wide, (c + 1) * wide)
        qa_ref[0, sl, :] = feature_major(rope(norm(proj(C_QA + c * wide), ones_head, 0)) * scale)

    u = proj(C_KCVC)
    kc_ref[0] = rope_half(u[:, :LANES]).astype(BF16)
    vc_ref[0] = u[:, LANES:].astype(BF16)
    u = proj(C_VSLWN)
    u = feature_major(u)
    vsl_ref[0] = u[:LANES]
    vwn_ref[0] = u[LANES:]

    pos = si * ts + lax.broadcasted_iota(jnp.int32, (ts, wide), 0)
    lane = lax.broadcasted_iota(jnp.int32, (ts, wide), 1) & (LANES - 1)
    sel_onehot = jnp.where(lane == HEAD_DIM + jnp.right_shift(pos, SEL_SHIFT), 1.0, 0.0)
    ksl_ref[0] = (rope(norm(proj(C_KSL), ones_chunk, 1)) + sel_onehot).astype(BF16)
    kwn_ref[0] = rope(norm(proj(C_KWN), ones_chunk, 2)).astype(BF16)
    gate_ref[0] = jnp.transpose(jax.nn.sigmoid(proj(C_GATE)))

    @pl.when(si == 0)
    def _():
        carry_ref[...] = jnp.zeros_like(carry_ref)

    lf = _log_sigmoid(proj(C_FL)[:, :LANES] + bf_ref[...])
    r_i = lax.broadcasted_iota(jnp.int32, (ts, ts), 0)
    c_i = lax.broadcasted_iota(jnp.int32, (ts, ts), 1)
    tri = jnp.where(r_i >= c_i, 1.0, 0.0).astype(BF16)
    hi, mid, lo = _split3(lf)
    c = _dot(tri, hi) + _dot(tri, mid) + _dot(tri, lo) + carry_ref[0:1, :]
    carry_ref[...] = jnp.broadcast_to(c[ts - 1:ts, :], carry_ref.shape)
    c = c * LOG2E
    cum_ref[0] = jnp.transpose(c)[:N_HEADS_FOX]

    n_pairs = N_HEADS_FOX // 2
    p_row = lax.broadcasted_iota(jnp.int32, (3 * LANES, n_pairs * LANES), 0)
    p_col = lax.broadcasted_iota(jnp.int32, (3 * LANES, n_pairs * LANES), 1)
    head = p_row & (LANES - 1)
    part = jnp.right_shift(p_row, 7)
    target = jnp.right_shift(head, 1) * LANES + (head & 1) * AUG_STRIDE + part
    place = jnp.where((p_col == target) & (head < N_HEADS_FOX), 1.0, 0.0).astype(BF16)
    aug = _dot(jnp.concatenate(_split3(-c), axis=1), place)

    for c2 in range(FOX_W // wide):
        sl = slice(c2 * wide, (c2 + 1) * wide)
        fq_ref[0, sl, :] = feature_major(norm(proj(C_FQ + c2 * wide), ones_head, 3) * scale)
        fv_ref[0, sl, :] = feature_major(proj(C_FV + c2 * wide))
        fk = norm(proj(C_FK + c2 * wide), ones_head, 4).astype(BF16)
        for e in range(2):
            pair = 2 * c2 + e
            fk_ref[0, :, 2 * pair * LANES:(2 * pair + 1) * LANES] = fk[:, e * LANES:(e + 1) * LANES]
            fk_ref[0, :, (2 * pair + 1) * LANES:(2 * pair + 2) * LANES] = (
                aug[:, pair * LANES:(pair + 1) * LANES].astype(BF16))


def _inproj(h, g, w_all, bf_row, gains, rope_tab):
    b, s, d = h.shape
    ts = min(PROJ_TS, s)
    tok = lambda c: pl.BlockSpec((1, ts, c), lambda i, j: (i, j, 0))
    const = lambda shape: pl.BlockSpec(shape, lambda i, j: (0,) * len(shape))
    feat = lambda c: pl.BlockSpec((1, c, ts), lambda i, j: (i, 0, j))
    kvx = N_KV_NSA * LANES
    specs = [(NSA_Q_W, BF16, True), (LANES, BF16, False), (LANES, BF16, False), (kvx, BF16, False),
             (LANES, BF16, True), (kvx, BF16, False), (LANES, BF16, True), (kvx, F32, True),
             (FOX_W, BF16, True), (2 * FOX_W, BF16, False), (FOX_W, BF16, True),
             (N_HEADS_FOX, F32, True)]
    out_shape = [jax.ShapeDtypeStruct((b, c, s) if fm else (b, s, c), dt) for c, dt, fm in specs]
    return pl.pallas_call(
        _inproj_kernel,
        grid=(b, s // ts),
        in_specs=[
            tok(d),
            const((1, d)),
            const((d, C_END)),
            const((1, LANES)),
            const((8, 2 * LANES)),
            pl.BlockSpec((3, ts, LANES), lambda i, j: (0, j, 0)),
        ],
        out_specs=[feat(c) if fm else tok(c) for c, _, fm in specs],
        out_shape=out_shape,
        scratch_shapes=[pltpu.VMEM((8, LANES), F32)],
        compiler_params=_params("parallel", "arbitrary"),
        name="inproj",
    )(h, g, w_all, bf_row, gains, rope_tab)


def _compress_kernel(r0k_ref, r1k_ref, r0v_ref, r1v_ref, w1k_ref, w1v_ref,
                     pk_ref, pv_ref, w2k_ref, w2v_ref, gain_ref, kc_ref, vc_ref):
    def mlp(r0_ref, r1_ref, w1_ref, p_ref, w2_ref, kh):
        top, bot = w1_ref[kh, 0], w1_ref[kh, 1]
        bias = _dot(p_ref[0], top) + _dot(p_ref[1], bot)
        hid = _dot(r0_ref[0], top) + _dot(r1_ref[0], bot) + bias[0:1, :]
        hid = hid * jax.nn.sigmoid(hid)
        return _dot(hid.astype(BF16), w2_ref[...])

    for kh in range(N_KV_NSA):
        kc = mlp(r0k_ref, r1k_ref, w1k_ref, pk_ref, w2k_ref, kh)
        ms = jnp.sum(kc * kc, axis=-1, keepdims=True) * (1.0 / HEAD_DIM)
        kc_ref[0, kh] = (kc * lax.rsqrt(ms + EPS) * gain_ref[...]).astype(BF16)
        vc_ref[0, kh] = mlp(r0v_ref, r1v_ref, w1v_ref, pv_ref, w2v_ref, kh).astype(BF16)


def _compress(r0k, r1k, r0v, r1v, w1k, w1v, pk, pv, w2k, w2v, gain):
    b, r, w = r0k.shape
    rows = pl.BlockSpec((1, r, w), lambda i: (i, 0, 0))
    const = lambda a: pl.BlockSpec(a.shape, lambda i: (0,) * a.ndim)
    out = jax.ShapeDtypeStruct((b, N_KV_NSA, r, LANES), BF16)
    out_spec = pl.BlockSpec((1, N_KV_NSA, r, LANES), lambda i: (i, 0, 0, 0))
    return pl.pallas_call(
        _compress_kernel,
        grid=(b,),
        in_specs=[rows, rows, rows, rows, const(w1k), const(w1v), const(pk), const(pv),
                  const(w2k), const(w2v), const(gain)],
        out_specs=[out_spec, out_spec],
        out_shape=[out, out],
        compiler_params=_params("parallel"),
        name="compress",
    )(r0k, r1k, r0v, r1v, w1k, w1v, pk, pv, w2k, w2v, gain)


def _ones_rows(width):
    return jnp.ones((BF16_ROWS, width), BF16)


def _pipelined_sweep(n_full, n_tail, tk, scores, masked, consume, s_ref, mt_ref):
    def stash(s_list):
        for e, s in enumerate(s_list):
            s_ref[e] = s
            mt_ref[e] = jnp.max(s, axis=0, keepdims=True)

    def body(j, carry):
        k0 = pl.multiple_of(j * tk, tk)
        s_new = scores(k0)
        consume(pl.multiple_of(k0 - tk, tk))
        stash(s_new)
        return carry

    def tail_step(k0):
        s_new = scores(k0)
        consume(pl.multiple_of(k0 - tk, tk))
        stash(masked(s_new, k0))

    k_tail = pl.multiple_of(n_full * tk, tk)

    @pl.when(n_full == 0)
    def _():
        stash(masked(scores(k_tail), k_tail))

    @pl.when(n_full > 0)
    def _():
        stash(scores(0))
        lax.fori_loop(1, n_full, body, 0)
        tail_step(k_tail)

    for i in range(1, n_tail):
        tail_step(pl.multiple_of(k_tail + i * tk, tk))
    consume(pl.multiple_of(k_tail + (n_tail - 1) * tk, tk))


def _nsa_kernel(qt_ref, kc_ref, vct_ref, ksl_ref, vslt_ref, kwn_ref, vwnt_ref, gt_ref, ov_ref,
                o_ref, qaug_ref, m_ref, acc_ref, s_ref, mt_ref, part_ref):
    qi = pl.program_id(2)
    tq = qt_ref.shape[2]
    s_len = ksl_ref.shape[1]
    n_cmp_rows = kc_ref.shape[2]
    n_sel = ov_ref.shape[0]
    top = min(SEL_TOPK, n_sel)
    grp = GROUP_NSA
    cols = grp * tq
    tk = min(NSA_TK, s_len)
    t0 = qi * tq

    def tok_of_col(rows):
        return t0 + (lax.broadcasted_iota(jnp.int32, (rows, cols), 1) & (tq - 1))

    def tok_of_tile(rows):
        return t0 + lax.broadcasted_iota(jnp.int32, (rows, tq), 1)

    def softmax_cols(s, keep):
        s = s + jnp.concatenate([jnp.where(keep, 0.0, NEG)] * grp, axis=1)
        mx = jnp.max(s, axis=0, keepdims=True)
        e = jnp.exp2(s - mx)
        den = jnp.sum(e, axis=0, keepdims=True)
        return e, jnp.where(mx > 0.5 * NEG, 1.0 / den, 0.0)

    qt = qt_ref[0]
    qaug_ref[0:HEAD_DIM, :] = jnp.concatenate(
        [qt[g * HEAD_DIM:(g + 1) * HEAD_DIM] for g in range(grp)], axis=1)
    qaug_ref[HEAD_DIM:, :] = jnp.zeros((HEAD_DIM, cols), BF16)

    n_c = lax.broadcasted_iota(jnp.int32, (n_cmp_rows, tq), 0)
    keep_c = n_c * CMP_STRIDE + (CMP_BLOCK - 1) <= tok_of_tile(n_cmp_rows)
    e_c, inv_c = softmax_cols(_dot(kc_ref[0, 0], qaug_ref[...]), keep_c)
    o_c = _dot(vct_ref[0, 0], e_c.astype(BF16)) * inv_c

    span = min(WINDOW + tq, s_len)
    start = pl.multiple_of(jnp.maximum(t0 + tq - span, 0), LANES)
    t_w = tok_of_tile(span)
    k_w = start + lax.broadcasted_iota(jnp.int32, (span, tq), 0)
    keep_w = (k_w <= t_w) & (t_w - k_w < WINDOW)
    e_w, inv_w = softmax_cols(_dot(kwn_ref[0, pl.ds(start, span), :], qaug_ref[...]), keep_w)
    o_w = _dot(vwnt_ref[0, :, pl.ds(start, span)], e_w.astype(BF16)) * inv_w

    gt = gt_ref[0]
    for g in range(grp):
        sl = slice(g * tq, (g + 1) * tq)
        part_ref[g * HEAD_DIM:(g + 1) * HEAD_DIM, :] = (
            gt[3 * g:3 * g + 1] * o_c[:, sl] + gt[3 * g + 2:3 * g + 3] * o_w[:, sl])

    p_sum = e_c[:, 0:tq] * inv_c[:, 0:tq]
    for g in range(1, grp):
        p_sum = p_sum + e_c[:, g * tq:(g + 1) * tq] * inv_c[:, g * tq:(g + 1) * tq]
    p_hi = p_sum.astype(BF16)
    p_lo = (p_sum - p_hi.astype(F32)).astype(BF16)
    ov = ov_ref[...]
    imp = _dot(ov, p_hi) + _dot(ov, p_lo)
    j_blk = lax.broadcasted_iota(jnp.int32, (n_sel, tq), 0)
    cur = jnp.right_shift(t0 + lax.broadcasted_iota(jnp.int32, (n_sel, tq), 1), SEL_SHIFT)
    forced = (j_blk == 0) | (j_blk == cur) | (j_blk == cur - 1)
    imp = jnp.where(j_blk <= cur, jnp.where(forced, FORCED_SCORE, imp), -1.0)
    sub = lax.broadcasted_iota(jnp.int32, (8, tq), 0)
    n_grp = n_sel // 8

    def write_selection_bias(n_active):
        groups = [imp[8 * v:8 * v + 8] for v in range(n_active)]
        ranks = [jnp.zeros((8, tq), F32) for _ in groups]
        for i in range(8 * n_active):
            r_i = jnp.broadcast_to(imp[i:i + 1, :], (8, tq))
            for v, g_v in enumerate(groups):
                if i < 8 * v:
                    beats = r_i >= g_v
                elif i >= 8 * v + 8:
                    beats = r_i > g_v
                else:
                    beats = (r_i > g_v) | ((r_i == g_v) & (sub > i - 8 * v))
                ranks[v] = ranks[v] + jnp.where(beats, 1.0, 0.0)
        rows = [jnp.where((r_v < top) & (g_v >= 0.0), 0.0, NEG) for r_v, g_v in zip(ranks, groups)]
        rows += [jnp.full((8, tq), NEG, F32)] * (n_grp - n_active)
        if n_sel < HEAD_DIM:
            rows.append(jnp.zeros((HEAD_DIM - n_sel, tq), F32))
        bias_t = jnp.concatenate(rows, axis=0).astype(BF16)
        qaug_ref[HEAD_DIM:, :] = jnp.concatenate([bias_t] * grp, axis=1)

    groups_needed = jnp.right_shift(t0 + tq - 1, SEL_SHIFT + 3) + 1
    for n_active in range(1, n_grp + 1):
        pl.when(groups_needed == n_active)(functools.partial(write_selection_bias, n_active))

    m_ref[...] = jnp.full_like(m_ref, NEG)
    acc_ref[...] = jnp.zeros_like(acc_ref)

    def slc_scores(k0):
        return [_dot(ksl_ref[0, pl.ds(k0, tk), :], qaug_ref[...])]

    def slc_masked(s_list, k0):
        key = k0 + lax.broadcasted_iota(jnp.int32, (tk, tq), 0)
        bias = jnp.where(key <= tok_of_tile(tk), 0.0, NEG)
        return [s_list[0] + jnp.concatenate([bias] * grp, axis=1)]

    def slc_consume(k0):
        m_old = m_ref[...]
        m_new = jnp.maximum(m_old, mt_ref[0])
        p = jnp.exp2(s_ref[0] - m_new).astype(BF16)
        vt = jnp.concatenate([vslt_ref[0, :, pl.ds(k0, tk)], _ones_rows(tk)], axis=0)
        acc_ref[...] = jnp.exp2(m_old - m_new) * acc_ref[...] + _dot(vt, p)
        m_ref[...] = m_new

    _pipelined_sweep(t0 // tk, max(1, tq // tk), tk, slc_scores, slc_masked, slc_consume, s_ref, mt_ref)
    o_s = acc_ref[0:HEAD_DIM, :] * (1.0 / acc_ref[HEAD_DIM:HEAD_DIM + 1, :])

    outs = [part_ref[g * HEAD_DIM:(g + 1) * HEAD_DIM, :]
            + gt[3 * g + 1:3 * g + 2] * o_s[:, g * tq:(g + 1) * tq] for g in range(grp)]
    o_ref[0] = jnp.transpose(jnp.concatenate(outs, axis=0))


def _nsa(qat, kcx, vct, kslx, vslt, kwnx, vwnt, gates_t, ov_t):
    b, _, s = qat.shape
    tq = min(NSA_TQ, s)
    assert tq & (tq - 1) == 0 and tq % LANES == 0
    r = kcx.shape[2]
    gw = GROUP_NSA * HEAD_DIM
    cols = GROUP_NSA * tq
    gate_rows = 16
    key_major = pl.BlockSpec((1, s, LANES), lambda i, k, j: (i, 0, k))
    val_major = pl.BlockSpec((1, HEAD_DIM, s), lambda i, k, j: (i, k, 0))
    return pl.pallas_call(
        _nsa_kernel,
        grid=(b, N_KV_NSA, s // tq),
        in_specs=[
            pl.BlockSpec((1, gw, tq), lambda i, k, j: (i, k, j)),
            pl.BlockSpec((1, 1, r, LANES), lambda i, k, j: (i, k, 0, 0)),
            pl.BlockSpec((1, 1, HEAD_DIM, r), lambda i, k, j: (i, k, 0, 0)),
            key_major, val_major, key_major, val_major,
            pl.BlockSpec((1, gate_rows, tq), lambda i, k, j: (i, k * (LANES // gate_rows), j)),
            pl.BlockSpec(ov_t.shape, lambda i, k, j: (0, 0)),
        ],
        out_specs=pl.BlockSpec((1, tq, gw), lambda i, k, j: (i, j, k)),
        out_shape=jax.ShapeDtypeStruct((b, s, NSA_Q_W), F32),
        scratch_shapes=[
            pltpu.VMEM((LANES, cols), BF16),
            pltpu.VMEM((1, cols), F32),
            pltpu.VMEM((HEAD_DIM + BF16_ROWS, cols), F32),
            pltpu.VMEM((1, min(NSA_TK, s), cols), F32),
            pltpu.VMEM((1, 1, cols), F32),
            pltpu.VMEM((gw, tq), F32),
        ],
        compiler_params=_params("parallel", "parallel", "arbitrary"),
        name="nsa",
    )(qat, kcx, vct, kslx, vslt, kwnx, vwnt, gates_t, ov_t)


def _fox_kernel(qt_ref, k_ref, vt_ref, cq_ref, o_ref, qaug_ref, m_ref, acc_ref, s_ref, mt_ref):
    pair = pl.program_id(1)
    qi = pl.program_id(2)
    tq = qt_ref.shape[2]
    tk = min(FOX_TK, k_ref.shape[1])
    t0 = qi * tq
    qt = qt_ref[0]
    row = lax.broadcasted_iota(jnp.int32, (LANES, tq), 0)
    zero = jnp.zeros_like(qt)
    cq = []
    for e in range(2):
        qaug_ref[e, 0:LANES, :] = jnp.where((row >= e * HEAD_DIM) & (row < (e + 1) * HEAD_DIM), qt, zero)
        pick = (row >= e * AUG_STRIDE) & (row < e * AUG_STRIDE + 3)
        qaug_ref[e, LANES:, :] = jnp.where(pick, 1.0, 0.0).astype(BF16)
        cq.append(cq_ref[0, pl.ds(2 * pair + e, 1), :])

    m_ref[...] = jnp.full_like(m_ref, NEG)
    acc_ref[...] = jnp.zeros_like(acc_ref)

    def scores(k0):
        kt = k_ref[0, pl.ds(k0, tk), :]
        return [_dot(kt, qaug_ref[e]) for e in range(2)]

    def masked(s_list, k0):
        key = k0 + lax.broadcasted_iota(jnp.int32, (tk, tq), 0)
        tok = t0 + lax.broadcasted_iota(jnp.int32, (tk, tq), 1)
        return [jnp.where(key <= tok, sv, NEG) for sv in s_list]

    def consume(k0):
        for e in range(2):
            m_old = m_ref[e]
            m_new = jnp.maximum(m_old, mt_ref[e] + cq[e])
            p = jnp.exp2(s_ref[e] + (cq[e] - m_new)).astype(BF16)
            vt = jnp.concatenate([vt_ref[0, e * HEAD_DIM:(e + 1) * HEAD_DIM, pl.ds(k0, tk)],
                                  _ones_rows(tk)], axis=0)
            acc_ref[e] = jnp.exp2(m_old - m_new) * acc_ref[e] + _dot(vt, p)
            m_ref[e] = m_new

    _pipelined_sweep(t0 // tk, max(1, tq // tk), tk, scores, masked, consume, s_ref, mt_ref)
    o_ref[0] = jnp.transpose(jnp.concatenate(
        [acc_ref[e, 0:HEAD_DIM, :] * (1.0 / acc_ref[e, HEAD_DIM:HEAD_DIM + 1, :]) for e in range(2)], axis=0))


def _fox(fqt, fkx, fvt, cq):
    b, _, s = fqt.shape
    tq = min(FOX_TQ, s)
    return pl.pallas_call(
        _fox_kernel,
        grid=(b, N_HEADS_FOX // 2, s // tq),
        in_specs=[
            pl.BlockSpec((1, LANES, tq), lambda i, p, j: (i, p, j)),
            pl.BlockSpec((1, s, 2 * LANES), lambda i, p, j: (i, 0, p)),
            pl.BlockSpec((1, LANES, s), lambda i, p, j: (i, p, 0)),
            pl.BlockSpec((1, N_HEADS_FOX, tq), lambda i, p, j: (i, 0, j)),
        ],
        out_specs=pl.BlockSpec((1, tq, LANES), lambda i, p, j: (i, j, p)),
        out_shape=jax.ShapeDtypeStruct((b, s, FOX_W), F32),
        scratch_shapes=[
            pltpu.VMEM((2, 2 * LANES, tq), BF16),
            pltpu.VMEM((2, 1, tq), F32),
            pltpu.VMEM((2, HEAD_DIM + BF16_ROWS, tq), F32),
            pltpu.VMEM((2, min(FOX_TK, s), tq), F32),
            pltpu.VMEM((2, 1, tq), F32),
        ],
        compiler_params=_params("parallel", "parallel", "arbitrary"),
        name="fox",
    )(fqt, fkx, fvt, cq)


def _tail_kernel(h_ref, oa_ref, ob_ref, p_ref, ga_ref, gb_ref, wa_ref, wb_ref,
                 g_ref, wg_ref, wu_ref, wd_ref, gg_ref, gp_ref, wgate_ref, wproj_ref, o_ref):
    oa = oa_ref[...]
    ob = ob_ref[...]
    na = (oa * _rms(oa) * ga_ref[...]).astype(BF16)
    nb = (ob * _rms(ob) * gb_ref[...]).astype(BF16)
    h = h_ref[...] + _dot(na, wa_ref[...]) + _dot(nb, wb_ref[...])
    h = _swiglu_half_step(h, g_ref, wg_ref, wu_ref, wd_ref)
    hn = (h * _rms(h) * gg_ref[...]).astype(BF16)
    gate = jax.nn.sigmoid(_dot(hn, wgate_ref[...]))
    e = _dot(p_ref[...].astype(BF16), wproj_ref[...])
    o_ref[...] = h + gate * (e * _rms(e) * gp_ref[...])


def _tail(h, oa, ob, p, ga, gb, wa, wb, g, wg, wu, wd, gg, gp, wgate, wproj):
    t, d = h.shape
    tm = min(FFN_TM, t)
    rows = lambda c: pl.BlockSpec((tm, c), lambda i: (i, 0))
    consts = (ga, gb, wa, wb, g, wg, wu, wd, gg, gp, wgate, wproj)
    return pl.pallas_call(
        _tail_kernel,
        grid=(t // tm,),
        in_specs=[rows(d), rows(oa.shape[1]), rows(ob.shape[1]), rows(p.shape[1])]
        + [_resident(c) for c in consts],
        out_specs=rows(d),
        out_shape=jax.ShapeDtypeStruct((t, d), F32),
        compiler_params=_params("parallel"),
        name="tail",
    )(h, oa, ob, p, *consts)


def _expand_heads(w, n):
    w = w.reshape(w.shape[0], n, HEAD_DIM)
    return jnp.concatenate([w, jnp.zeros_like(w)], axis=-1).reshape(w.shape[0], n * LANES)


def _pad_cols(w, width):
    return jnp.pad(w, ((0, 0), (0, width - w.shape[1])))


def _layout_w_in(w_in):
    splits = [int(v) for v in np.cumsum(PROJ_SIZES)[:-1]]
    qa, kc, vc, ksl, vsl, kwn, vwn, ga, qf, kf, vf, fl = jnp.split(w_in, splits, axis=-1)
    per_group = 3 * GROUP_NSA
    ga_x = jnp.concatenate([_pad_cols(ga[:, k * per_group:(k + 1) * per_group], LANES)
                            for k in range(N_KV_NSA)], axis=-1)
    cols = [qa, kc, vc, _expand_heads(ksl, N_KV_NSA), _expand_heads(kwn, N_KV_NSA), vsl, vwn,
            ga_x, qf, kf, vf, _pad_cols(fl, 2 * LANES)]
    return jnp.concatenate(cols, axis=-1).astype(BF16)


def _rope_tables(seq):
    pos = jnp.arange(seq, dtype=F32)
    inv = ROPE_THETA ** (-jnp.arange(0, ROT_DIM, 2, dtype=F32) / ROT_DIM)
    ang = pos[:, None] * inv[None, :]
    cos, sin = jnp.cos(ang), jnp.sin(ang)
    half = ROT_DIM // 2
    rest = HEAD_DIM - ROT_DIM
    ones = jnp.ones((seq, rest), F32)
    zeros = jnp.zeros((seq, rest), F32)
    zh = jnp.zeros((seq, half), F32)
    cos_t = jnp.concatenate([cos, cos, ones], axis=-1)
    sin_lo = jnp.concatenate([-sin, zh, zeros], axis=-1)
    sin_hi = jnp.concatenate([zh, sin, zeros], axis=-1)
    tile2 = lambda t: jnp.concatenate([t, t], axis=-1)
    return jnp.stack([tile2(cos_t), tile2(sin_lo), tile2(sin_hi)])


def _layout_cmp_w1(w1):
    hid = w1.shape[1]
    w = w1.reshape(2, CMP_STRIDE, HEAD_DIM, hid)
    z = jnp.zeros_like(w)
    per_head = [jnp.concatenate([w, z], axis=2), jnp.concatenate([z, w], axis=2)]
    return jnp.stack(per_head).reshape(N_KV_NSA, 2, CMP_STRIDE * LANES, hid).astype(BF16)


def _layout_cmp_pos(pos):
    p = pos.reshape(2, CMP_STRIDE, HEAD_DIM)
    p = jnp.concatenate([p, p], axis=-1).reshape(2, 1, CMP_STRIDE * LANES)
    return jnp.broadcast_to(p, (2, 8, CMP_STRIDE * LANES)).astype(BF16)


def _overlap_t(seq):
    rows = seq // CMP_STRIDE
    n_sel = seq // SEL_BLOCK
    cmp_start = np.arange(rows) * CMP_STRIDE
    sel_start = np.arange(n_sel) * SEL_BLOCK
    ov = ((cmp_start[None, :] <= sel_start[:, None] + SEL_BLOCK - 1)
          & (cmp_start[None, :] + CMP_BLOCK - 1 >= sel_start[:, None]))
    ov[:, (seq - CMP_BLOCK) // CMP_STRIDE + 1:] = False
    return jnp.asarray(ov.astype(np.float32), BF16)


def _row(v, width=None):
    v = v.reshape(1, -1).astype(F32)
    return v if width is None else _pad_cols(v, width)


def kernel(x, p, ffn1_norm, ffn1_wg, ffn1_wu, ffn1_wd, mix_norm, w_in, b_forget, q_norm_nsa, k_norm_cmp, k_norm_slc, k_norm_win, cmp_pos_k, cmp_pos_v, cmp_k_w1, cmp_k_w2, cmp_v_w1, cmp_v_w2, q_norm_fox, k_norm_fox, out_norm_nsa, out_norm_fox, w_out, ffn2_norm, ffn2_wg, ffn2_wu, ffn2_wd, ple_gate_norm, ple_w_gate, ple_w_proj, ple_norm):
    b, s, d = x.shape
    assert s // SEL_BLOCK <= HEAD_DIM and (s // SEL_BLOCK) % 8 == 0
    assert s % max(NSA_TK, FOX_TK, FOX_TQ) == 0
    depth = ffn1_norm.shape[0]
    t = b * s
    rope_tab = _rope_tables(s)
    ov_t = _overlap_t(s)
    tile4 = lambda g: jnp.concatenate([g] * 4).reshape(1, 2 * LANES).astype(F32)
    h = x.reshape(t, d)
    for i in range(depth):
        h = _ffn(h, _row(ffn1_norm[i]), ffn1_wg[i].astype(BF16), ffn1_wu[i].astype(BF16),
                 ffn1_wd[i].astype(BF16))

        gains = jnp.concatenate([tile4(q_norm_nsa[i]), tile4(k_norm_slc[i]), tile4(k_norm_win[i]),
                                 tile4(q_norm_fox[i]), tile4(k_norm_fox[i]),
                                 jnp.zeros((3, 2 * LANES), F32)], axis=0)
        (qat, kc, vc, kslx, vslt, kwnx, vwnt, gates_t, fqt, fkx, fvt, cq) = _inproj(
            h.reshape(b, s, d), _row(mix_norm[i]), _layout_w_in(w_in[i]),
            _row(b_forget[i], LANES), gains, rope_tab)

        rows = s // CMP_STRIDE

        def shifted(tok):
            r0 = tok.reshape(b, rows, CMP_STRIDE * LANES)
            r1 = jnp.concatenate([r0[:, 1:], jnp.zeros_like(r0[:, :1])], axis=1)
            return r0, r1

        r0k, r1k = shifted(kc)
        r0v, r1v = shifted(vc)
        kcx, vcx = _compress(
            r0k, r1k, r0v, r1v, _layout_cmp_w1(cmp_k_w1[i]), _layout_cmp_w1(cmp_v_w1[i]),
            _layout_cmp_pos(cmp_pos_k[i]), _layout_cmp_pos(cmp_pos_v[i]),
            _pad_cols(cmp_k_w2[i], LANES).astype(BF16), _pad_cols(cmp_v_w2[i], LANES).astype(BF16),
            _row(k_norm_cmp[i], LANES))
        vct = jnp.transpose(vcx[..., :HEAD_DIM], (0, 1, 3, 2))

        o_a = _nsa(qat, kcx, vct, kslx, vslt, kwnx, vwnt, gates_t, ov_t)
        o_b = _fox(fqt, fkx, fvt, cq)

        w_o = w_out[i].astype(BF16)
        h = _tail(h, o_a.reshape(t, NSA_Q_W), o_b.reshape(t, FOX_W), p[i].reshape(t, -1),
                  _row(out_norm_nsa[i]), _row(out_norm_fox[i]), w_o[:NSA_Q_W], w_o[NSA_Q_W:],
                  _row(ffn2_norm[i]), ffn2_wg[i].astype(BF16), ffn2_wu[i].astype(BF16),
                  ffn2_wd[i].astype(BF16), _row(ple_gate_norm[i]), _row(ple_norm[i]),
                  ple_w_gate[i].astype(BF16), ple_w_proj[i].astype(BF16))
    return h.reshape(b, s, d)
```

```python
import functools

import numpy as np
import jax
import jax.numpy as jnp
from jax import lax
from jax.experimental import pallas as pl
from jax.experimental.pallas import tpu as pltpu

F32 = jnp.float32
BF16 = jnp.bfloat16

D_MODEL = 1024
HEAD_DIM = 64
N_HEADS_NSA = 8
N_KV_NSA = 2
GROUP_NSA = N_HEADS_NSA // N_KV_NSA
N_HEADS_FOX = 8
NSA_Q_W = N_HEADS_NSA * HEAD_DIM
NSA_KV_W = N_KV_NSA * HEAD_DIM
FOX_W = N_HEADS_FOX * HEAD_DIM
PROJ_SIZES = (NSA_Q_W, NSA_KV_W, NSA_KV_W, NSA_KV_W, NSA_KV_W, NSA_KV_W, NSA_KV_W,
              3 * N_HEADS_NSA, FOX_W, FOX_W, FOX_W, N_HEADS_FOX)
D_FF = 2816
D_PLE = 256
ROPE_THETA = 500000.0
ROT_DIM = HEAD_DIM // 4
CMP_BLOCK = 32
CMP_STRIDE = 16
CMP_HIDDEN = 256
SEL_BLOCK = 64
SEL_SHIFT = 6
SEL_TOPK = 16
WINDOW = 512
FORCED_SCORE = 1e6
EPS = 1e-6
NEG = -1e30
LOG2E = 1.4426950408889634

LANES = 128
BF16_ROWS = 16
VMEM_LIMIT = 56 * 1024 * 1024
AUG_STRIDE = 8

C_QA = 0
C_KCVC = C_QA + NSA_Q_W
C_KSL = C_KCVC + 2 * LANES
C_KWN = C_KSL + N_KV_NSA * LANES
C_VSLWN = C_KWN + N_KV_NSA * LANES
C_GATE = C_VSLWN + 2 * LANES
C_FQ = C_GATE + N_KV_NSA * LANES
C_FK = C_FQ + FOX_W
C_FV = C_FK + FOX_W
C_FL = C_FV + FOX_W
C_END = C_FL + 2 * LANES

FFN_TM = 512
FFN_TF = 256
PROJ_TS = 512
NSA_TQ = 256
NSA_TK = 256
FOX_TQ = 512
FOX_TK = 512


def _dot(a, b):
    return jnp.dot(a, b, preferred_element_type=F32)


def _dot_nt(a, b):
    return lax.dot_general(a, b, (((1,), (1,)), ((), ())), preferred_element_type=F32)


def _params(*sem):
    return pltpu.CompilerParams(dimension_semantics=sem, vmem_limit_bytes=VMEM_LIMIT)


def _rms(x):
    return lax.rsqrt(jnp.mean(x * x, axis=-1, keepdims=True) + EPS)


def _split3(x):
    hi = x.astype(BF16)
    r1 = x - hi.astype(F32)
    mid = r1.astype(BF16)
    lo = (r1 - mid.astype(F32)).astype(BF16)
    return hi, mid, lo


def _swiglu_half_step(x, g_ref, wg_ref, wu_ref, wd_ref):
    xn = (x * _rms(x) * g_ref[...]).astype(BF16)
    acc = None
    for j in range(wg_ref.shape[1] // FFN_TF):
        sl = slice(j * FFN_TF, (j + 1) * FFN_TF)
        gate = _dot(xn, wg_ref[:, sl])
        up = _dot(xn, wu_ref[:, sl])
        act = (gate * jax.nn.sigmoid(gate) * up).astype(BF16)
        part = _dot(act, wd_ref[sl, :])
        acc = part if acc is None else acc + part
    return x + 0.5 * acc


def _ffn_kernel(x_ref, g_ref, wg_ref, wu_ref, wd_ref, o_ref):
    o_ref[...] = _swiglu_half_step(x_ref[...], g_ref, wg_ref, wu_ref, wd_ref)


def _resident(a):
    return pl.BlockSpec(a.shape, lambda *_: (0,) * a.ndim, pipeline_mode=pl.Buffered(1))


def _ffn(x, g, wg, wu, wd):
    t, d = x.shape
    tm = min(FFN_TM, t)
    rows = pl.BlockSpec((tm, d), lambda i: (i, 0))
    return pl.pallas_call(
        _ffn_kernel,
        grid=(t // tm,),
        in_specs=[rows, _resident(g), _resident(wg), _resident(wu), _resident(wd)],
        out_specs=rows,
        out_shape=jax.ShapeDtypeStruct((t, d), F32),
        compiler_params=_params("parallel"),
        name="ffn",
    )(x, g, wg, wu, wd)


def _log_sigmoid(x):
    return -(jnp.maximum(-x, 0.0) + jnp.log(1.0 + jnp.exp(-jnp.abs(x))))


def _inproj_kernel(h_ref, g_ref, w_ref, bf_ref, gains_ref, rope_ref,
                   qa_ref, kc_ref, vc_ref, ksl_ref, vsl_ref, kwn_ref, vwn_ref,
                   gate_ref, fq_ref, fk_ref, fv_ref, cum_ref, carry_ref):
    si = pl.program_id(1)
    ts = h_ref.shape[1]
    wide = 2 * LANES
    x = h_ref[0]
    a = (x * _rms(x) * g_ref[...]).astype(BF16)

    row = lax.broadcasted_iota(jnp.int32, (2 * wide, wide), 0) & (wide - 1)
    col = lax.broadcasted_iota(jnp.int32, (2 * wide, wide), 1)
    ones_head = jnp.where(jnp.right_shift(row, 6) == jnp.right_shift(col, 6), 1.0, 0.0).astype(BF16)
    ones_chunk = jnp.where(jnp.right_shift(row, 7) == jnp.right_shift(col, 7), 1.0, 0.0).astype(BF16)
    cos_t, sin_lo, sin_hi = rope_ref[0], rope_ref[1], rope_ref[2]

    def proj(c0):
        return _dot(a, w_ref[:, c0:c0 + wide])

    def norm(u, ones_mat, gain_row):
        u2 = u * u
        hi = u2.astype(BF16)
        lo = (u2 - hi.astype(F32)).astype(BF16)
        ms = _dot(jnp.concatenate([hi, lo], axis=1), ones_mat) * (1.0 / HEAD_DIM)
        return u * lax.rsqrt(ms + EPS) * gains_ref[gain_row:gain_row + 1, :]

    def rope_half(u):
        return (u * cos_t + pltpu.roll(u, LANES - ROT_DIM // 2, 1) * sin_lo
                + pltpu.roll(u, ROT_DIM // 2, 1) * sin_hi)

    def rope(u):
        return jnp.concatenate([rope_half(u[:, :LANES]), rope_half(u[:, LANES:])], axis=1)

    def feature_major(u):
        return jnp.transpose(u.astype(BF16))

    scale = HEAD_DIM ** -0.5 * LOG2E
    pos = si * ts + lax.broadcasted_iota(jnp.int32, (ts, wide), 0)
    lane = lax.broadcasted_iota(jnp.int32, (ts, wide), 1) & (LANES - 1)
    sel_onehot = jnp.where(lane == HEAD_DIM + jnp.right_shift(pos, SEL_SHIFT), 1.0, 0.0)

    @pl.when(si == 0)
    def _():
        carry_ref[...] = jnp.zeros_like(carry_ref)

    aug = []

    def forget_stage(u):
        lf = _log_sigmoid(u[:, :LANES] + bf_ref[...])
        r_i = lax.broadcasted_iota(jnp.int32, (ts, ts), 0)
        c_i = lax.broadcasted_iota(jnp.int32, (ts, ts), 1)
        tri = jnp.where(r_i >= c_i, 1.0, 0.0).astype(BF16)
        hi, mid, lo = _split3(lf)
        c = _dot(tri, hi) + _dot(tri, mid) + _dot(tri, lo) + carry_ref[0:1, :]
        carry_ref[...] = jnp.broadcast_to(c[ts - 1:ts, :], carry_ref.shape)
        c = c * LOG2E
        cum_ref[0] = jnp.transpose(c)[:N_HEADS_FOX]
        n_pairs = N_HEADS_FOX // 2
        p_row = lax.broadcasted_iota(jnp.int32, (3 * LANES, n_pairs * LANES), 0)
        p_col = lax.broadcasted_iota(jnp.int32, (3 * LANES, n_pairs * LANES), 1)
        head = p_row & (LANES - 1)
        part = jnp.right_shift(p_row, 7)
        target = jnp.right_shift(head, 1) * LANES + (head & 1) * AUG_STRIDE + part
        place = jnp.where((p_col == target) & (head < N_HEADS_FOX), 1.0, 0.0).astype(BF16)
        aug.append(_dot(jnp.concatenate(_split3(-c), axis=1), place).astype(BF16))

    def nsa_q_stage(c, u):
        qa_ref[0, c * wide:(c + 1) * wide, :] = feature_major(rope(norm(u, ones_head, 0)) * scale)

    def kcvc_stage(u):
        kc_ref[0] = rope_half(u[:, :LANES]).astype(BF16)
        vc_ref[0] = u[:, LANES:].astype(BF16)

    def values_stage(u):
        u = feature_major(u)
        vsl_ref[0] = u[:LANES]
        vwn_ref[0] = u[LANES:]

    def ksl_stage(u):
        ksl_ref[0] = (rope(norm(u, ones_chunk, 1)) + sel_onehot).astype(BF16)

    def kwn_stage(u):
        kwn_ref[0] = rope(norm(u, ones_chunk, 2)).astype(BF16)

    def gate_stage(u):
        gate_ref[0] = jnp.transpose(jax.nn.sigmoid(u))

    def fox_q_stage(c, u):
        fq_ref[0, c * wide:(c + 1) * wide, :] = feature_major(norm(u, ones_head, 3) * scale)

    def fox_v_stage(c, u):
        fv_ref[0, c * wide:(c + 1) * wide, :] = feature_major(u)

    def fox_k_stage(c, u):
        fk = norm(u, ones_head, 4).astype(BF16)
        for e in range(2):
            pair = 2 * c + e
            fk_ref[0, :, 2 * pair * LANES:(2 * pair + 1) * LANES] = fk[:, e * LANES:(e + 1) * LANES]
            fk_ref[0, :, (2 * pair + 1) * LANES:(2 * pair + 2) * LANES] = (
                aug[0][:, pair * LANES:(pair + 1) * LANES])

    stages = [(C_FL, forget_stage), (C_KCVC, kcvc_stage), (C_VSLWN, values_stage),
              (C_KSL, ksl_stage), (C_KWN, kwn_stage), (C_GATE, gate_stage)]
    for c in range(NSA_Q_W // wide):
        stages.append((C_QA + c * wide, functools.partial(nsa_q_stage, c)))
    for c in range(FOX_W // wide):
        stages.append((C_FQ + c * wide, functools.partial(fox_q_stage, c)))
        stages.append((C_FV + c * wide, functools.partial(fox_v_stage, c)))
        stages.append((C_FK + c * wide, functools.partial(fox_k_stage, c)))

    u_next = proj(stages[0][0])
    for idx, (_, epilogue) in enumerate(stages):
        u = u_next
        if idx + 1 < len(stages):
            u_next = proj(stages[idx + 1][0])
        epilogue(u)


def _inproj(h, g, w_all, bf_row, gains, rope_tab):
    b, s, d = h.shape
    ts = min(PROJ_TS, s)
    tok = lambda c: pl.BlockSpec((1, ts, c), lambda i, j: (i, j, 0))
    const = lambda shape: pl.BlockSpec(shape, lambda i, j: (0,) * len(shape))
    feat = lambda c: pl.BlockSpec((1, c, ts), lambda i, j: (i, 0, j))
    kvx = N_KV_NSA * LANES
    specs = [(NSA_Q_W, BF16, True), (LANES, BF16, False), (LANES, BF16, False), (kvx, BF16, False),
             (LANES, BF16, True), (kvx, BF16, False), (LANES, BF16, True), (kvx, F32, True),
             (FOX_W, BF16, True), (2 * FOX_W, BF16, False), (FOX_W, BF16, True),
             (N_HEADS_FOX, F32, True)]
    out_shape = [jax.ShapeDtypeStruct((b, c, s) if fm else (b, s, c), dt) for c, dt, fm in specs]
    return pl.pallas_call(
        _inproj_kernel,
        grid=(b, s // ts),
        in_specs=[
            tok(d),
            const((1, d)),
            const((d, C_END)),
            const((1, LANES)),
            const((8, 2 * LANES)),
            pl.BlockSpec((3, ts, LANES), lambda i, j: (0, j, 0)),
        ],
        out_specs=[feat(c) if fm else tok(c) for c, _, fm in specs],
        out_shape=out_shape,
        scratch_shapes=[pltpu.VMEM((8, LANES), F32)],
        compiler_params=_params("parallel", "arbitrary"),
        name="inproj",
    )(h, g, w_all, bf_row, gains, rope_tab)


def _compress_kernel(r0k_ref, r1k_ref, r0v_ref, r1v_ref, w1k_ref, w1v_ref,
                     pk_ref, pv_ref, w2k_ref, w2v_ref, gain_ref, kc_ref, vc_ref):
    def mlp(r0_ref, r1_ref, w1_ref, p_ref, w2_ref, kh):
        top, bot = w1_ref[kh, 0], w1_ref[kh, 1]
        bias = _dot(p_ref[0], top) + _dot(p_ref[1], bot)
        hid = _dot(r0_ref[0], top) + _dot(r1_ref[0], bot) + bias[0:1, :]
        hid = hid * jax.nn.sigmoid(hid)
        return _dot(hid.astype(BF16), w2_ref[...])

    for kh in range(N_KV_NSA):
        kc = mlp(r0k_ref, r1k_ref, w1k_ref, pk_ref, w2k_ref, kh)
        ms = jnp.sum(kc * kc, axis=-1, keepdims=True) * (1.0 / HEAD_DIM)
        kc_ref[0, kh] = (kc * lax.rsqrt(ms + EPS) * gain_ref[...]).astype(BF16)
        vc_ref[0, kh] = mlp(r0v_ref, r1v_ref, w1v_ref, pv_ref, w2v_ref, kh).astype(BF16)


def _compress(r0k, r1k, r0v, r1v, w1k, w1v, pk, pv, w2k, w2v, gain):
    b, r, w = r0k.shape
    rows = pl.BlockSpec((1, r, w), lambda i: (i, 0, 0))
    const = lambda a: pl.BlockSpec(a.shape, lambda i: (0,) * a.ndim)
    out = jax.ShapeDtypeStruct((b, N_KV_NSA, r, LANES), BF16)
    out_spec = pl.BlockSpec((1, N_KV_NSA, r, LANES), lambda i: (i, 0, 0, 0))
    return pl.pallas_call(
        _compress_kernel,
        grid=(b,),
        in_specs=[rows, rows, rows, rows, const(w1k), const(w1v), const(pk), const(pv),
                  const(w2k), const(w2v), const(gain)],
        out_specs=[out_spec, out_spec],
        out_shape=[out, out],
        compiler_params=_params("parallel"),
        name="compress",
    )(r0k, r1k, r0v, r1v, w1k, w1v, pk, pv, w2k, w2v, gain)


def _ones_rows(width):
    return jnp.ones((BF16_ROWS, width), BF16)


def _pipelined_sweep(n_full, n_tail, tk, scores, masked, consume, s_ref, mt_ref):
    def stash(s_list):
        for e, s in enumerate(s_list):
            s_ref[e] = s
            mt_ref[e] = jnp.max(s, axis=0, keepdims=True)

    def body(j, carry):
        k0 = pl.multiple_of(j * tk, tk)
        s_new = scores(k0)
        consume(pl.multiple_of(k0 - tk, tk))
        stash(s_new)
        return carry

    def tail_step(k0):
        s_new = scores(k0)
        consume(pl.multiple_of(k0 - tk, tk))
        stash(masked(s_new, k0))

    k_tail = pl.multiple_of(n_full * tk, tk)

    @pl.when(n_full == 0)
    def _():
        stash(masked(scores(k_tail), k_tail))

    @pl.when(n_full > 0)
    def _():
        stash(scores(0))
        lax.fori_loop(1, n_full, body, 0)
        tail_step(k_tail)

    for i in range(1, n_tail):
        tail_step(pl.multiple_of(k_tail + i * tk, tk))
    consume(pl.multiple_of(k_tail + (n_tail - 1) * tk, tk))


def _nsa_kernel(qt_ref, kc_ref, vct_ref, ksl_ref, vslt_ref, kwn_ref, vwnt_ref, gt_ref, ov_ref,
                o_ref, qaug_ref, m_ref, acc_ref, s_ref, mt_ref, part_ref):
    qi = pl.program_id(2)
    tq = qt_ref.shape[2]
    s_len = ksl_ref.shape[1]
    n_cmp_rows = kc_ref.shape[2]
    n_sel = ov_ref.shape[0]
    top = min(SEL_TOPK, n_sel)
    grp = GROUP_NSA
    cols = grp * tq
    tk = min(NSA_TK, s_len)
    t0 = qi * tq

    def tok_of_col(rows):
        return t0 + (lax.broadcasted_iota(jnp.int32, (rows, cols), 1) & (tq - 1))

    def tok_of_tile(rows):
        return t0 + lax.broadcasted_iota(jnp.int32, (rows, tq), 1)

    def softmax_cols(s, keep):
        s = s + jnp.concatenate([jnp.where(keep, 0.0, NEG)] * grp, axis=1)
        mx = jnp.max(s, axis=0, keepdims=True)
        e = jnp.exp2(s - mx)
        den = jnp.sum(e, axis=0, keepdims=True)
        return e, jnp.where(mx > 0.5 * NEG, 1.0 / den, 0.0)

    qt = qt_ref[0]
    qaug_ref[0:HEAD_DIM, :] = jnp.concatenate(
        [qt[g * HEAD_DIM:(g + 1) * HEAD_DIM] for g in range(grp)], axis=1)
    qaug_ref[HEAD_DIM:, :] = jnp.zeros((HEAD_DIM, cols), BF16)

    span = min(WINDOW + tq, s_len)
    start = pl.multiple_of(jnp.maximum(t0 + tq - span, 0), LANES)
    s_c = _dot(kc_ref[0, 0], qaug_ref[...])
    s_w = _dot(kwn_ref[0, pl.ds(start, span), :], qaug_ref[...])

    n_c = lax.broadcasted_iota(jnp.int32, (n_cmp_rows, tq), 0)
    keep_c = n_c * CMP_STRIDE + (CMP_BLOCK - 1) <= tok_of_tile(n_cmp_rows)
    e_c, inv_c = softmax_cols(s_c, keep_c)
    o_c = _dot(vct_ref[0, 0], e_c.astype(BF16)) * inv_c

    t_w = tok_of_tile(span)
    k_w = start + lax.broadcasted_iota(jnp.int32, (span, tq), 0)
    keep_w = (k_w <= t_w) & (t_w - k_w < WINDOW)
    e_w, inv_w = softmax_cols(s_w, keep_w)
    o_w = _dot(vwnt_ref[0, :, pl.ds(start, span)], e_w.astype(BF16)) * inv_w

    gt = gt_ref[0]
    for g in range(grp):
        sl = slice(g * tq, (g + 1) * tq)
        part_ref[g * HEAD_DIM:(g + 1) * HEAD_DIM, :] = (
            gt[3 * g:3 * g + 1] * o_c[:, sl] + gt[3 * g + 2:3 * g + 3] * o_w[:, sl])

    p_sum = e_c[:, 0:tq] * inv_c[:, 0:tq]
    for g in range(1, grp):
        p_sum = p_sum + e_c[:, g * tq:(g + 1) * tq] * inv_c[:, g * tq:(g + 1) * tq]
    p_hi = p_sum.astype(BF16)
    p_lo = (p_sum - p_hi.astype(F32)).astype(BF16)
    ov = ov_ref[...]
    imp = _dot(ov, p_hi) + _dot(ov, p_lo)
    j_blk = lax.broadcasted_iota(jnp.int32, (n_sel, tq), 0)
    cur = jnp.right_shift(t0 + lax.broadcasted_iota(jnp.int32, (n_sel, tq), 1), SEL_SHIFT)
    forced = (j_blk == 0) | (j_blk == cur) | (j_blk == cur - 1)
    imp = jnp.where(j_blk <= cur, jnp.where(forced, FORCED_SCORE, imp), -1.0)
    sub = lax.broadcasted_iota(jnp.int32, (8, tq), 0)
    n_grp = n_sel // 8

    def write_selection_bias(n_active):
        groups = [imp[8 * v:8 * v + 8] for v in range(n_active)]
        ranks = [jnp.zeros((8, tq), F32) for _ in groups]
        for i in range(8 * n_active):
            r_i = jnp.broadcast_to(imp[i:i + 1, :], (8, tq))
            for v, g_v in enumerate(groups):
                if i < 8 * v:
                    beats = r_i >= g_v
                elif i >= 8 * v + 8:
                    beats = r_i > g_v
                else:
                    beats = (r_i > g_v) | ((r_i == g_v) & (sub > i - 8 * v))
                ranks[v] = ranks[v] + jnp.where(beats, 1.0, 0.0)
        rows = [jnp.where((r_v < top) & (g_v >= 0.0), 0.0, NEG) for r_v, g_v in zip(ranks, groups)]
        rows += [jnp.full((8, tq), NEG, F32)] * (n_grp - n_active)
        if n_sel < HEAD_DIM:
            rows.append(jnp.zeros((HEAD_DIM - n_sel, tq), F32))
        bias_t = jnp.concatenate(rows, axis=0).astype(BF16)
        qaug_ref[HEAD_DIM:, :] = jnp.concatenate([bias_t] * grp, axis=1)

    groups_needed = jnp.right_shift(t0 + tq - 1, SEL_SHIFT + 3) + 1
    for n_active in range(1, n_grp + 1):
        pl.when(groups_needed == n_active)(functools.partial(write_selection_bias, n_active))

    m_ref[...] = jnp.full_like(m_ref, NEG)
    acc_ref[...] = jnp.zeros_like(acc_ref)

    def slc_scores(k0):
        return [_dot(ksl_ref[0, pl.ds(k0, tk), :], qaug_ref[...])]

    def slc_masked(s_list, k0):
        key = k0 + lax.broadcasted_iota(jnp.int32, (tk, tq), 0)
        bias = jnp.where(key <= tok_of_tile(tk), 0.0, NEG)
        return [s_list[0] + jnp.concatenate([bias] * grp, axis=1)]

    def slc_consume(k0):
        m_old = m_ref[...]
        m_new = jnp.maximum(m_old, mt_ref[0])
        p = jnp.exp2(s_ref[0] - m_new).astype(BF16)
        vt = jnp.concatenate([vslt_ref[0, :, pl.ds(k0, tk)], _ones_rows(tk)], axis=0)
        acc_ref[...] = jnp.exp2(m_old - m_new) * acc_ref[...] + _dot(vt, p)
        m_ref[...] = m_new

    _pipelined_sweep(t0 // tk, max(1, tq // tk), tk, slc_scores, slc_masked, slc_consume, s_ref, mt_ref)
    o_s = acc_ref[0:HEAD_DIM, :] * (1.0 / acc_ref[HEAD_DIM:HEAD_DIM + 1, :])

    outs = [part_ref[g * HEAD_DIM:(g + 1) * HEAD_DIM, :]
            + gt[3 * g + 1:3 * g + 2] * o_s[:, g * tq:(g + 1) * tq] for g in range(grp)]
    o_ref[0] = jnp.transpose(jnp.concatenate(outs, axis=0))


def _nsa(qat, kcx, vct, kslx, vslt, kwnx, vwnt, gates_t, ov_t):
    b, _, s = qat.shape
    tq = min(NSA_TQ, s)
    assert tq & (tq - 1) == 0 and tq % LANES == 0
    r = kcx.shape[2]
    gw = GROUP_NSA * HEAD_DIM
    cols = GROUP_NSA * tq
    gate_rows = 16
    key_major = pl.BlockSpec((1, s, LANES), lambda i, k, j: (i, 0, k))
    val_major = pl.BlockSpec((1, HEAD_DIM, s), lambda i, k, j: (i, k, 0))
    return pl.pallas_call(
        _nsa_kernel,
        grid=(b, N_KV_NSA, s // tq),
        in_specs=[
            pl.BlockSpec((1, gw, tq), lambda i, k, j: (i, k, j)),
            pl.BlockSpec((1, 1, r, LANES), lambda i, k, j: (i, k, 0, 0)),
            pl.BlockSpec((1, 1, HEAD_DIM, r), lambda i, k, j: (i, k, 0, 0)),
            key_major, val_major, key_major, val_major,
            pl.BlockSpec((1, gate_rows, tq), lambda i, k, j: (i, k * (LANES // gate_rows), j)),
            pl.BlockSpec(ov_t.shape, lambda i, k, j: (0, 0)),
        ],
        out_specs=pl.BlockSpec((1, tq, gw), lambda i, k, j: (i, j, k)),
        out_shape=jax.ShapeDtypeStruct((b, s, NSA_Q_W), F32),
        scratch_shapes=[
            pltpu.VMEM((LANES, cols), BF16),
            pltpu.VMEM((1, cols), F32),
            pltpu.VMEM((HEAD_DIM + BF16_ROWS, cols), F32),
            pltpu.VMEM((1, min(NSA_TK, s), cols), F32),
            pltpu.VMEM((1, 1, cols), F32),
            pltpu.VMEM((gw, tq), F32),
        ],
        compiler_params=_params("parallel", "parallel", "arbitrary"),
        name="nsa",
    )(qat, kcx, vct, kslx, vslt, kwnx, vwnt, gates_t, ov_t)


def _fox_kernel(qt_ref, k_ref, vt_ref, cq_ref, o_ref, qaug_ref, m_ref, acc_ref, s_ref, mt_ref):
    pair = pl.program_id(1)
    qi = pl.program_id(2)
    tq = qt_ref.shape[2]
    tk = min(FOX_TK, k_ref.shape[1])
    t0 = qi * tq
    qt = qt_ref[0]
    row = lax.broadcasted_iota(jnp.int32, (LANES, tq), 0)
    zero = jnp.zeros_like(qt)
    cq = []
    for e in range(2):
        qaug_ref[e, 0:LANES, :] = jnp.where((row >= e * HEAD_DIM) & (row < (e + 1) * HEAD_DIM), qt, zero)
        pick = (row >= e * AUG_STRIDE) & (row < e * AUG_STRIDE + 3)
        qaug_ref[e, LANES:, :] = jnp.where(pick, 1.0, 0.0).astype(BF16)
        cq.append(cq_ref[0, pl.ds(2 * pair + e, 1), :])

    m_ref[...] = jnp.full_like(m_ref, NEG)
    acc_ref[...] = jnp.zeros_like(acc_ref)

    def scores(k0):
        kt = k_ref[0, pl.ds(k0, tk), :]
        return [_dot(kt, qaug_ref[e]) for e in range(2)]

    def masked(s_list, k0):
        key = k0 + lax.broadcasted_iota(jnp.int32, (tk, tq), 0)
        tok = t0 + lax.broadcasted_iota(jnp.int32, (tk, tq), 1)
        return [jnp.where(key <= tok, sv, NEG) for sv in s_list]

    def consume(k0):
        for e in range(2):
            m_old = m_ref[e]
            m_new = jnp.maximum(m_old, mt_ref[e] + cq[e])
            p = jnp.exp2(s_ref[e] + (cq[e] - m_new)).astype(BF16)
            vt = jnp.concatenate([vt_ref[0, e * HEAD_DIM:(e + 1) * HEAD_DIM, pl.ds(k0, tk)],
                                  _ones_rows(tk)], axis=0)
            acc_ref[e] = jnp.exp2(m_old - m_new) * acc_ref[e] + _dot(vt, p)
            m_ref[e] = m_new

    _pipelined_sweep(t0 // tk, max(1, tq // tk), tk, scores, masked, consume, s_ref, mt_ref)
    o_ref[0] = jnp.transpose(jnp.concatenate(
        [acc_ref[e, 0:HEAD_DIM, :] * (1.0 / acc_ref[e, HEAD_DIM:HEAD_DIM + 1, :]) for e in range(2)], axis=0))


def _fox(fqt, fkx, fvt, cq):
    b, _, s = fqt.shape
    tq = min(FOX_TQ, s)
    return pl.pallas_call(
        _fox_kernel,
        grid=(b, N_HEADS_FOX // 2, s // tq),
        in_specs=[
            pl.BlockSpec((1, LANES, tq), lambda i, p, j: (i, p, j)),
            pl.BlockSpec((1, s, 2 * LANES), lambda i, p, j: (i, 0, p)),
            pl.BlockSpec((1, LANES, s), lambda i, p, j: (i, p, 0)),
            pl.BlockSpec((1, N_HEADS_FOX, tq), lambda i, p, j: (i, 0, j)),
        ],
        out_specs=pl.BlockSpec((1, tq, LANES), lambda i, p, j: (i, j, p)),
        out_shape=jax.ShapeDtypeStruct((b, s, FOX_W), F32),
        scratch_shapes=[
            pltpu.VMEM((2, 2 * LANES, tq), BF16),
            pltpu.VMEM((2, 1, tq), F32),
            pltpu.VMEM((2, HEAD_DIM + BF16_ROWS, tq), F32),
            pltpu.VMEM((2, min(FOX_TK, s), tq), F32),
            pltpu.VMEM((2, 1, tq), F32),
        ],
        compiler_params=_params("parallel", "parallel", "arbitrary"),
        name="fox",
    )(fqt, fkx, fvt, cq)


def _tail_kernel(h_ref, oa_ref, ob_ref, p_ref, ga_ref, gb_ref, wa_ref, wb_ref,
                 g_ref, wg_ref, wu_ref, wd_ref, gg_ref, gp_ref, wgate_ref, wproj_ref, o_ref):
    oa = oa_ref[...]
    ob = ob_ref[...]
    na = (oa * _rms(oa) * ga_ref[...]).astype(BF16)
    nb = (ob * _rms(ob) * gb_ref[...]).astype(BF16)
    h = h_ref[...] + _dot(na, wa_ref[...]) + _dot(nb, wb_ref[...])
    h = _swiglu_half_step(h, g_ref, wg_ref, wu_ref, wd_ref)
    hn = (h * _rms(h) * gg_ref[...]).astype(BF16)
    gate = jax.nn.sigmoid(_dot(hn, wgate_ref[...]))
    e = _dot(p_ref[...].astype(BF16), wproj_ref[...])
    o_ref[...] = h + gate * (e * _rms(e) * gp_ref[...])


def _tail(h, oa, ob, p, ga, gb, wa, wb, g, wg, wu, wd, gg, gp, wgate, wproj):
    t, d = h.shape
    tm = min(FFN_TM, t)
    rows = lambda c: pl.BlockSpec((tm, c), lambda i: (i, 0))
    consts = (ga, gb, wa, wb, g, wg, wu, wd, gg, gp, wgate, wproj)
    return pl.pallas_call(
        _tail_kernel,
        grid=(t // tm,),
        in_specs=[rows(d), rows(oa.shape[1]), rows(ob.shape[1]), rows(p.shape[1])]
        + [_resident(c) for c in consts],
        out_specs=rows(d),
        out_shape=jax.ShapeDtypeStruct((t, d), F32),
        compiler_params=_params("parallel"),
        name="tail",
    )(h, oa, ob, p, *consts)


def _expand_heads(w, n):
    w = w.reshape(w.shape[0], n, HEAD_DIM)
    return jnp.concatenate([w, jnp.zeros_like(w)], axis=-1).reshape(w.shape[0], n * LANES)


def _pad_cols(w, width):
    return jnp.pad(w, ((0, 0), (0, width - w.shape[1])))


def _layout_w_in(w_in):
    splits = [int(v) for v in np.cumsum(PROJ_SIZES)[:-1]]
    qa, kc, vc, ksl, vsl, kwn, vwn, ga, qf, kf, vf, fl = jnp.split(w_in, splits, axis=-1)
    per_group = 3 * GROUP_NSA
    ga_x = jnp.concatenate([_pad_cols(ga[:, k * per_group:(k + 1) * per_group], LANES)
                            for k in range(N_KV_NSA)], axis=-1)
    cols = [qa, kc, vc, _expand_heads(ksl, N_KV_NSA), _expand_heads(kwn, N_KV_NSA), vsl, vwn,
            ga_x, qf, kf, vf, _pad_cols(fl, 2 * LANES)]
    return jnp.concatenate(cols, axis=-1).astype(BF16)


def _rope_tables(seq):
    pos = jnp.arange(seq, dtype=F32)
    inv = ROPE_THETA ** (-jnp.arange(0, ROT_DIM, 2, dtype=F32) / ROT_DIM)
    ang = pos[:, None] * inv[None, :]
    cos, sin = jnp.cos(ang), jnp.sin(ang)
    half = ROT_DIM // 2
    rest = HEAD_DIM - ROT_DIM
    ones = jnp.ones((seq, rest), F32)
    zeros = jnp.zeros((seq, rest), F32)
    zh = jnp.zeros((seq, half), F32)
    cos_t = jnp.concatenate([cos, cos, ones], axis=-1)
    sin_lo = jnp.concatenate([-sin, zh, zeros], axis=-1)
    sin_hi = jnp.concatenate([zh, sin, zeros], axis=-1)
    tile2 = lambda t: jnp.concatenate([t, t], axis=-1)
    return jnp.stack([tile2(cos_t), tile2(sin_lo), tile2(sin_hi)])


def _layout_cmp_w1(w1):
    hid = w1.shape[1]
    w = w1.reshape(2, CMP_STRIDE, HEAD_DIM, hid)
    z = jnp.zeros_like(w)
    per_head = [jnp.concatenate([w, z], axis=2), jnp.concatenate([z, w], axis=2)]
    return jnp.stack(per_head).reshape(N_KV_NSA, 2, CMP_STRIDE * LANES, hid).astype(BF16)


def _layout_cmp_pos(pos):
    p = pos.reshape(2, CMP_STRIDE, HEAD_DIM)
    p = jnp.concatenate([p, p], axis=-1).reshape(2, 1, CMP_STRIDE * LANES)
    return jnp.broadcast_to(p, (2, 8, CMP_STRIDE * LANES)).astype(BF16)


def _overlap_t(seq):
    rows = seq // CMP_STRIDE
    n_sel = seq // SEL_BLOCK
    cmp_start = np.arange(rows) * CMP_STRIDE
    sel_start = np.arange(n_sel) * SEL_BLOCK
    ov = ((cmp_start[None, :] <= sel_start[:, None] + SEL_BLOCK - 1)
          & (cmp_start[None, :] + CMP_BLOCK - 1 >= sel_start[:, None]))
    ov[:, (seq - CMP_BLOCK) // CMP_STRIDE + 1:] = False
    return jnp.asarray(ov.astype(np.float32), BF16)


def _row(v, width=None):
    v = v.reshape(1, -1).astype(F32)
    return v if width is None else _pad_cols(v, width)


def kernel(x, p, ffn1_norm, ffn1_wg, ffn1_wu, ffn1_wd, mix_norm, w_in, b_forget, q_norm_nsa, k_norm_cmp, k_norm_slc, k_norm_win, cmp_pos_k, cmp_pos_v, cmp_k_w1, cmp_k_w2, cmp_v_w1, cmp_v_w2, q_norm_fox, k_norm_fox, out_norm_nsa, out_norm_fox, w_out, ffn2_norm, ffn2_wg, ffn2_wu, ffn2_wd, ple_gate_norm, ple_w_gate, ple_w_proj, ple_norm):
    b, s, d = x.shape
    assert s // SEL_BLOCK <= HEAD_DIM and (s // SEL_BLOCK) % 8 == 0
    assert s % max(NSA_TK, FOX_TK, FOX_TQ) == 0
    depth = ffn1_norm.shape[0]
    t = b * s
    rope_tab = _rope_tables(s)
    ov_t = _overlap_t(s)
    tile4 = lambda g: jnp.concatenate([g] * 4).reshape(1, 2 * LANES).astype(F32)
    h = x.reshape(t, d)
    for i in range(depth):
        h = _ffn(h, _row(ffn1_norm[i]), ffn1_wg[i].astype(BF16), ffn1_wu[i].astype(BF16),
                 ffn1_wd[i].astype(BF16))

        gains = jnp.concatenate([tile4(q_norm_nsa[i]), tile4(k_norm_slc[i]), tile4(k_norm_win[i]),
                                 tile4(q_norm_fox[i]), tile4(k_norm_fox[i]),
                                 jnp.zeros((3, 2 * LANES), F32)], axis=0)
        (qat, kc, vc, kslx, vslt, kwnx, vwnt, gates_t, fqt, fkx, fvt, cq) = _inproj(
            h.reshape(b, s, d), _row(mix_norm[i]), _layout_w_in(w_in[i]),
            _row(b_forget[i], LANES), gains, rope_tab)

        rows = s // CMP_STRIDE

        def shifted(tok):
            r0 = tok.reshape(b, rows, CMP_STRIDE * LANES)
            r1 = jnp.concatenate([r0[:, 1:], jnp.zeros_like(r0[:, :1])], axis=1)
            return r0, r1

        r0k, r1k = shifted(kc)
        r0v, r1v = shifted(vc)
        kcx, vcx = _compress(
            r0k, r1k, r0v, r1v, _layout_cmp_w1(cmp_k_w1[i]), _layout_cmp_w1(cmp_v_w1[i]),
            _layout_cmp_pos(cmp_pos_k[i]), _layout_cmp_pos(cmp_pos_v[i]),
            _pad_cols(cmp_k_w2[i], LANES).astype(BF16), _pad_cols(cmp_v_w2[i], LANES).astype(BF16),
            _row(k_norm_cmp[i], LANES))
        vct = jnp.transpose(vcx[..., :HEAD_DIM], (0, 1, 3, 2))

        o_a = _nsa(qat, kcx, vct, kslx, vslt, kwnx, vwnt, gates_t, ov_t)
        o_b = _fox(fqt, fkx, fvt, cq)

        w_o = w_out[i].astype(BF16)
        h = _tail(h, o_a.reshape(t, NSA_Q_W), o_b.reshape(t, FOX_W), p[i].reshape(t, -1),
                  _row(out_norm_nsa[i]), _row(out_norm_fox[i]), w_o[:NSA_Q_W], w_o[NSA_Q_W:],
                  _row(ffn2_norm[i]), ffn2_wg[i].astype(BF16), ffn2_wu[i].astype(BF16),
                  ffn2_wd[i].astype(BF16), _row(ple_gate_norm[i]), _row(ple_norm[i]),
                  ple_w_gate[i].astype(BF16), ple_w_proj[i].astype(BF16))
    return h.reshape(b, s, d)
```

```python
import functools

import numpy as np
import jax
import jax.numpy as jnp
from jax import lax
from jax.experimental import pallas as pl
from jax.experimental.pallas import tpu as pltpu

F32 = jnp.float32
BF16 = jnp.bfloat16

D_MODEL = 1024
HEAD_DIM = 64
N_HEADS_NSA = 8
N_KV_NSA = 2
GROUP_NSA = N_HEADS_NSA // N_KV_NSA
N_HEADS_FOX = 8
NSA_Q_W = N_HEADS_NSA * HEAD_DIM
NSA_KV_W = N_KV_NSA * HEAD_DIM
FOX_W = N_HEADS_FOX * HEAD_DIM
PROJ_SIZES = (NSA_Q_W, NSA_KV_W, NSA_KV_W, NSA_KV_W, NSA_KV_W, NSA_KV_W, NSA_KV_W,
              3 * N_HEADS_NSA, FOX_W, FOX_W, FOX_W, N_HEADS_FOX)
D_FF = 2816
D_PLE = 256
ROPE_THETA = 500000.0
ROT_DIM = HEAD_DIM // 4
CMP_BLOCK = 32
CMP_STRIDE = 16
CMP_HIDDEN = 256
SEL_BLOCK = 64
SEL_SHIFT = 6
SEL_TOPK = 16
WINDOW = 512
FORCED_SCORE = 1e6
EPS = 1e-6
NEG = -1e30
LOG2E = 1.4426950408889634

LANES = 128
BF16_ROWS = 16
VMEM_LIMIT = 56 * 1024 * 1024
AUG_STRIDE = 8

C_QA = 0
C_KCVC = C_QA + NSA_Q_W
C_KSL = C_KCVC + 2 * LANES
C_KWN = C_KSL + N_KV_NSA * LANES
C_VSLWN = C_KWN + N_KV_NSA * LANES
C_GATE = C_VSLWN + 2 * LANES
C_FQ = C_GATE + N_KV_NSA * LANES
C_FK = C_FQ + FOX_W
C_FV = C_FK + FOX_W
C_FL = C_FV + FOX_W
C_END = C_FL + 2 * LANES

FFN_TM = 512
FFN_TF = 256
PROJ_TS = 512
NSA_TQ = 256
NSA_TK = 512
FOX_TQ = 512
FOX_TK = 512


def _dot(a, b):
    return jnp.dot(a, b, preferred_element_type=F32)


def _dot_nt(a, b):
    return lax.dot_general(a, b, (((1,), (1,)), ((), ())), preferred_element_type=F32)


def _params(*sem):
    return pltpu.CompilerParams(dimension_semantics=sem, vmem_limit_bytes=VMEM_LIMIT)


def _rms(x):
    return lax.rsqrt(jnp.mean(x * x, axis=-1, keepdims=True) + EPS)


def _split3(x):
    hi = x.astype(BF16)
    r1 = x - hi.astype(F32)
    mid = r1.astype(BF16)
    lo = (r1 - mid.astype(F32)).astype(BF16)
    return hi, mid, lo


def _swiglu_half_step(x, g_ref, wg_ref, wu_ref, wd_ref):
    xn = (x * _rms(x) * g_ref[...]).astype(BF16)
    acc = None
    for j in range(wg_ref.shape[1] // FFN_TF):
        sl = slice(j * FFN_TF, (j + 1) * FFN_TF)
        gate = _dot(xn, wg_ref[:, sl])
        up = _dot(xn, wu_ref[:, sl])
        act = (gate * jax.nn.sigmoid(gate) * up).astype(BF16)
        part = _dot(act, wd_ref[sl, :])
        acc = part if acc is None else acc + part
    return x + 0.5 * acc


def _ffn_kernel(x_ref, g_ref, wg_ref, wu_ref, wd_ref, o_ref):
    o_ref[...] = _swiglu_half_step(x_ref[...], g_ref, wg_ref, wu_ref, wd_ref)


def _resident(a):
    return pl.BlockSpec(a.shape, lambda *_: (0,) * a.ndim, pipeline_mode=pl.Buffered(1))


def _ffn(x, g, wg, wu, wd):
    t, d = x.shape
    tm = min(FFN_TM, t)
    rows = pl.BlockSpec((tm, d), lambda i: (i, 0))
    return pl.pallas_call(
        _ffn_kernel,
        grid=(t // tm,),
        in_specs=[rows, _resident(g), _resident(wg), _resident(wu), _resident(wd)],
        out_specs=rows,
        out_shape=jax.ShapeDtypeStruct((t, d), F32),
        compiler_params=_params("parallel"),
        name="ffn",
    )(x, g, wg, wu, wd)


def _log_sigmoid(x):
    return -(jnp.maximum(-x, 0.0) + jnp.log(1.0 + jnp.exp(-jnp.abs(x))))


def _inproj_kernel(h_ref, g_ref, w_ref, bf_ref, gains_ref, rope_ref,
                   qa_ref, kc_ref, vc_ref, ksl_ref, vsl_ref, kwn_ref, vwn_ref,
                   gate_ref, fq_ref, fk_ref, fv_ref, cum_ref, carry_ref):
    si = pl.program_id(1)
    ts = h_ref.shape[1]
    wide = 2 * LANES
    x = h_ref[0]
    a = (x * _rms(x) * g_ref[...]).astype(BF16)

    row = lax.broadcasted_iota(jnp.int32, (2 * wide, wide), 0) & (wide - 1)
    col = lax.broadcasted_iota(jnp.int32, (2 * wide, wide), 1)
    ones_head = jnp.where(jnp.right_shift(row, 6) == jnp.right_shift(col, 6), 1.0, 0.0).astype(BF16)
    ones_chunk = jnp.where(jnp.right_shift(row, 7) == jnp.right_shift(col, 7), 1.0, 0.0).astype(BF16)
    cos_t, sin_lo, sin_hi = rope_ref[0], rope_ref[1], rope_ref[2]

    def proj(c0):
        return _dot(a, w_ref[:, c0:c0 + wide])

    def norm(u, ones_mat, gain_row):
        u2 = u * u
        hi = u2.astype(BF16)
        lo = (u2 - hi.astype(F32)).astype(BF16)
        ms = _dot(jnp.concatenate([hi, lo], axis=1), ones_mat) * (1.0 / HEAD_DIM)
        return u * lax.rsqrt(ms + EPS) * gains_ref[gain_row:gain_row + 1, :]

    def rope_half(u):
        return (u * cos_t + pltpu.roll(u, LANES - ROT_DIM // 2, 1) * sin_lo
                + pltpu.roll(u, ROT_DIM // 2, 1) * sin_hi)

    def rope(u):
        return jnp.concatenate([rope_half(u[:, :LANES]), rope_half(u[:, LANES:])], axis=1)

    def feature_major(u):
        return jnp.transpose(u.astype(BF16))

    scale = HEAD_DIM ** -0.5 * LOG2E
    pos = si * ts + lax.broadcasted_iota(jnp.int32, (ts, wide), 0)
    lane = lax.broadcasted_iota(jnp.int32, (ts, wide), 1) & (LANES - 1)
    sel_onehot = jnp.where(lane == HEAD_DIM + jnp.right_shift(pos, SEL_SHIFT), 1.0, 0.0)

    @pl.when(si == 0)
    def _():
        carry_ref[...] = jnp.zeros_like(carry_ref)

    aug = []

    def forget_stage(u):
        lf = _log_sigmoid(u[:, :LANES] + bf_ref[...])
        r_i = lax.broadcasted_iota(jnp.int32, (ts, ts), 0)
        c_i = lax.broadcasted_iota(jnp.int32, (ts, ts), 1)
        tri = jnp.where(r_i >= c_i, 1.0, 0.0).astype(BF16)
        hi, mid, lo = _split3(lf)
        c = _dot(tri, hi) + _dot(tri, mid) + _dot(tri, lo) + carry_ref[0:1, :]
        carry_ref[...] = jnp.broadcast_to(c[ts - 1:ts, :], carry_ref.shape)
        c = c * LOG2E
        cum_ref[0] = jnp.transpose(c)[:N_HEADS_FOX]
        n_pairs = N_HEADS_FOX // 2
        p_row = lax.broadcasted_iota(jnp.int32, (3 * LANES, n_pairs * LANES), 0)
        p_col = lax.broadcasted_iota(jnp.int32, (3 * LANES, n_pairs * LANES), 1)
        head = p_row & (LANES - 1)
        part = jnp.right_shift(p_row, 7)
        target = jnp.right_shift(head, 1) * LANES + (head & 1) * AUG_STRIDE + part
        place = jnp.where((p_col == target) & (head < N_HEADS_FOX), 1.0, 0.0).astype(BF16)
        aug.append(_dot(jnp.concatenate(_split3(-c), axis=1), place).astype(BF16))

    def nsa_q_stage(c, u):
        qa_ref[0, c * wide:(c + 1) * wide, :] = feature_major(rope(norm(u, ones_head, 0)) * scale)

    def kcvc_stage(u):
        kc_ref[0] = rope_half(u[:, :LANES]).astype(BF16)
        vc_ref[0] = u[:, LANES:].astype(BF16)

    def values_stage(u):
        u = feature_major(u)
        vsl_ref[0] = u[:LANES]
        vwn_ref[0] = u[LANES:]

    def ksl_stage(u):
        ksl_ref[0] = (rope(norm(u, ones_chunk, 1)) + sel_onehot).astype(BF16)

    def kwn_stage(u):
        kwn_ref[0] = rope(norm(u, ones_chunk, 2)).astype(BF16)

    def gate_stage(u):
        gate_ref[0] = jnp.transpose(jax.nn.sigmoid(u))

    def fox_q_stage(c, u):
        fq_ref[0, c * wide:(c + 1) * wide, :] = feature_major(norm(u, ones_head, 3) * scale)

    def fox_v_stage(c, u):
        fv_ref[0, c * wide:(c + 1) * wide, :] = feature_major(u)

    def fox_k_stage(c, u):
        fk = norm(u, ones_head, 4).astype(BF16)
        for e in range(2):
            pair = 2 * c + e
            fk_ref[0, :, 2 * pair * LANES:(2 * pair + 1) * LANES] = fk[:, e * LANES:(e + 1) * LANES]
            fk_ref[0, :, (2 * pair + 1) * LANES:(2 * pair + 2) * LANES] = (
                aug[0][:, pair * LANES:(pair + 1) * LANES])

    stages = [(C_FL, forget_stage), (C_KCVC, kcvc_stage), (C_VSLWN, values_stage),
              (C_KSL, ksl_stage), (C_KWN, kwn_stage), (C_GATE, gate_stage)]
    for c in range(NSA_Q_W // wide):
        stages.append((C_QA + c * wide, functools.partial(nsa_q_stage, c)))
    for c in range(FOX_W // wide):
        stages.append((C_FQ + c * wide, functools.partial(fox_q_stage, c)))
        stages.append((C_FV + c * wide, functools.partial(fox_v_stage, c)))
        stages.append((C_FK + c * wide, functools.partial(fox_k_stage, c)))

    u_next = proj(stages[0][0])
    for idx, (_, epilogue) in enumerate(stages):
        u = u_next
        if idx + 1 < len(stages):
            u_next = proj(stages[idx + 1][0])
        epilogue(u)


def _inproj(h, g, w_all, bf_row, gains, rope_tab):
    b, s, d = h.shape
    ts = min(PROJ_TS, s)
    tok = lambda c: pl.BlockSpec((1, ts, c), lambda i, j: (i, j, 0))
    const = lambda shape: pl.BlockSpec(shape, lambda i, j: (0,) * len(shape))
    feat = lambda c: pl.BlockSpec((1, c, ts), lambda i, j: (i, 0, j))
    kvx = N_KV_NSA * LANES
    specs = [(NSA_Q_W, BF16, True), (LANES, BF16, False), (LANES, BF16, False), (kvx, BF16, False),
             (LANES, BF16, True), (kvx, BF16, False), (LANES, BF16, True), (kvx, F32, True),
             (FOX_W, BF16, True), (2 * FOX_W, BF16, False), (FOX_W, BF16, True),
             (N_HEADS_FOX, F32, True)]
    out_shape = [jax.ShapeDtypeStruct((b, c, s) if fm else (b, s, c), dt) for c, dt, fm in specs]
    return pl.pallas_call(
        _inproj_kernel,
        grid=(b, s // ts),
        in_specs=[
            tok(d),
            const((1, d)),
            const((d, C_END)),
            const((1, LANES)),
            const((8, 2 * LANES)),
            pl.BlockSpec((3, ts, LANES), lambda i, j: (0, j, 0)),
        ],
        out_specs=[feat(c) if fm else tok(c) for c, _, fm in specs],
        out_shape=out_shape,
        scratch_shapes=[pltpu.VMEM((8, LANES), F32)],
        compiler_params=_params("parallel", "arbitrary"),
        name="inproj",
    )(h, g, w_all, bf_row, gains, rope_tab)


def _compress_kernel(rk_ref, rv_ref, w1k_ref, w1v_ref,
                     pk_ref, pv_ref, w2k_ref, w2v_ref, gain_ref, kc_ref, vc_ref):
    def mlp(r_ref, w1_ref, p_ref, w2_ref, kh):
        top, bot = w1_ref[kh, 0], w1_ref[kh, 1]
        rows = r_ref[0]
        bias = _dot(p_ref[0], top) + _dot(p_ref[1], bot)
        second = pltpu.roll(_dot(rows, bot), rows.shape[0] - 1, 0)
        hid = _dot(rows, top) + second + bias[0:1, :]
        hid = hid * jax.nn.sigmoid(hid)
        return _dot(hid.astype(BF16), w2_ref[...])

    for kh in range(N_KV_NSA):
        kc = mlp(rk_ref, w1k_ref, pk_ref, w2k_ref, kh)
        ms = jnp.sum(kc * kc, axis=-1, keepdims=True) * (1.0 / HEAD_DIM)
        kc_ref[0, kh] = (kc * lax.rsqrt(ms + EPS) * gain_ref[...]).astype(BF16)
        vc_ref[0, kh] = mlp(rv_ref, w1v_ref, pv_ref, w2v_ref, kh).astype(BF16)


def _compress(rk, rv, w1k, w1v, pk, pv, w2k, w2v, gain):
    b, r, w = rk.shape
    rows = pl.BlockSpec((1, r, w), lambda i: (i, 0, 0))
    const = lambda a: pl.BlockSpec(a.shape, lambda i: (0,) * a.ndim)
    out = jax.ShapeDtypeStruct((b, N_KV_NSA, r, LANES), BF16)
    out_spec = pl.BlockSpec((1, N_KV_NSA, r, LANES), lambda i: (i, 0, 0, 0))
    return pl.pallas_call(
        _compress_kernel,
        grid=(b,),
        in_specs=[rows, rows, const(w1k), const(w1v), const(pk), const(pv),
                  const(w2k), const(w2v), const(gain)],
        out_specs=[out_spec, out_spec],
        out_shape=[out, out],
        compiler_params=_params("parallel"),
        name="compress",
    )(rk, rv, w1k, w1v, pk, pv, w2k, w2v, gain)


def _ones_rows(width):
    return jnp.ones((BF16_ROWS, width), BF16)


def _pipelined_sweep(n_full, n_tail, tk, scores, masked, consume, s_ref, mt_ref):
    def stash(s_list):
        for e, s in enumerate(s_list):
            s_ref[e] = s
            mt_ref[e] = jnp.max(s, axis=0, keepdims=True)

    def body(j, carry):
        k0 = pl.multiple_of(j * tk, tk)
        s_new = scores(k0)
        consume(pl.multiple_of(k0 - tk, tk))
        stash(s_new)
        return carry

    def tail_step(k0):
        s_new = scores(k0)
        consume(pl.multiple_of(k0 - tk, tk))
        stash(masked(s_new, k0))

    k_tail = pl.multiple_of(n_full * tk, tk)

    @pl.when(n_full == 0)
    def _():
        stash(masked(scores(k_tail), k_tail))

    @pl.when(n_full > 0)
    def _():
        stash(scores(0))
        lax.fori_loop(1, n_full, body, 0)
        tail_step(k_tail)

    for i in range(1, n_tail):
        tail_step(pl.multiple_of(k_tail + i * tk, tk))
    consume(pl.multiple_of(k_tail + (n_tail - 1) * tk, tk))


def _nsa_kernel(qt_ref, kc_ref, vct_ref, ksl_ref, vslt_ref, kwn_ref, vwnt_ref, gt_ref, ov_ref,
                o_ref, qaug_ref, m_ref, acc_ref, s_ref, mt_ref, part_ref):
    qi = pl.program_id(2)
    tq = qt_ref.shape[2]
    s_len = ksl_ref.shape[1]
    n_cmp_rows = kc_ref.shape[2]
    n_sel = ov_ref.shape[0]
    top = min(SEL_TOPK, n_sel)
    grp = GROUP_NSA
    cols = grp * tq
    tk = min(NSA_TK, s_len)
    t0 = qi * tq

    def tok_of_col(rows):
        return t0 + (lax.broadcasted_iota(jnp.int32, (rows, cols), 1) & (tq - 1))

    def tok_of_tile(rows):
        return t0 + lax.broadcasted_iota(jnp.int32, (rows, tq), 1)

    def softmax_cols(s, keep):
        s = s + jnp.concatenate([jnp.where(keep, 0.0, NEG)] * grp, axis=1)
        mx = jnp.max(s, axis=0, keepdims=True)
        e = jnp.exp2(s - mx)
        den = jnp.sum(e, axis=0, keepdims=True)
        return e, jnp.where(mx > 0.5 * NEG, 1.0 / den, 0.0)

    qt = qt_ref[0]
    qaug_ref[0:HEAD_DIM, :] = jnp.concatenate(
        [qt[g * HEAD_DIM:(g + 1) * HEAD_DIM] for g in range(grp)], axis=1)
    qaug_ref[HEAD_DIM:, :] = jnp.zeros((HEAD_DIM, cols), BF16)

    span = min(WINDOW + tq, s_len)
    start = pl.multiple_of(jnp.maximum(t0 + tq - span, 0), LANES)
    s_c = _dot(kc_ref[0, 0], qaug_ref[...])
    s_w = _dot(kwn_ref[0, pl.ds(start, span), :], qaug_ref[...])

    n_c = lax.broadcasted_iota(jnp.int32, (n_cmp_rows, tq), 0)
    keep_c = n_c * CMP_STRIDE + (CMP_BLOCK - 1) <= tok_of_tile(n_cmp_rows)
    e_c, inv_c = softmax_cols(s_c, keep_c)
    o_c = _dot(vct_ref[0, 0], e_c.astype(BF16)) * inv_c

    t_w = tok_of_tile(span)
    k_w = start + lax.broadcasted_iota(jnp.int32, (span, tq), 0)
    keep_w = (k_w <= t_w) & (t_w - k_w < WINDOW)
    e_w, inv_w = softmax_cols(s_w, keep_w)
    o_w = _dot(vwnt_ref[0, :, pl.ds(start, span)], e_w.astype(BF16)) * inv_w

    gt = gt_ref[0]
    for g in range(grp):
        sl = slice(g * tq, (g + 1) * tq)
        part_ref[g * HEAD_DIM:(g + 1) * HEAD_DIM, :] = (
            gt[3 * g:3 * g + 1] * o_c[:, sl] + gt[3 * g + 2:3 * g + 3] * o_w[:, sl])

    p_sum = e_c[:, 0:tq] * inv_c[:, 0:tq]
    for g in range(1, grp):
        p_sum = p_sum + e_c[:, g * tq:(g + 1) * tq] * inv_c[:, g * tq:(g + 1) * tq]
    p_hi = p_sum.astype(BF16)
    p_lo = (p_sum - p_hi.astype(F32)).astype(BF16)
    ov = ov_ref[...]
    imp = _dot(ov, p_hi) + _dot(ov, p_lo)
    j_blk = lax.broadcasted_iota(jnp.int32, (n_sel, tq), 0)
    cur = jnp.right_shift(t0 + lax.broadcasted_iota(jnp.int32, (n_sel, tq), 1), SEL_SHIFT)
    forced = (j_blk == 0) | (j_blk == cur) | (j_blk == cur - 1)
    imp = jnp.where(j_blk <= cur, jnp.where(forced, FORCED_SCORE, imp), -1.0)
    sub = lax.broadcasted_iota(jnp.int32, (8, tq), 0)
    n_grp = n_sel // 8

    def write_selection_bias(n_active):
        groups = [imp[8 * v:8 * v + 8] for v in range(n_active)]
        ranks = [jnp.zeros((8, tq), F32) for _ in groups]
        for i in range(8 * n_active):
            r_i = jnp.broadcast_to(imp[i:i + 1, :], (8, tq))
            for v, g_v in enumerate(groups):
                if i < 8 * v:
                    beats = r_i >= g_v
                elif i >= 8 * v + 8:
                    beats = r_i > g_v
                else:
                    beats = (r_i > g_v) | ((r_i == g_v) & (sub > i - 8 * v))
                ranks[v] = ranks[v] + jnp.where(beats, 1.0, 0.0)
        rows = [jnp.where((r_v < top) & (g_v >= 0.0), 0.0, NEG) for r_v, g_v in zip(ranks, groups)]
        rows += [jnp.full((8, tq), NEG, F32)] * (n_grp - n_active)
        if n_sel < HEAD_DIM:
            rows.append(jnp.zeros((HEAD_DIM - n_sel, tq), F32))
        bias_t = jnp.concatenate(rows, axis=0).astype(BF16)
        qaug_ref[HEAD_DIM:, :] = jnp.concatenate([bias_t] * grp, axis=1)

    groups_needed = jnp.right_shift(t0 + tq - 1, SEL_SHIFT + 3) + 1
    for n_active in range(1, n_grp + 1):
        pl.when(groups_needed == n_active)(functools.partial(write_selection_bias, n_active))

    m_ref[...] = jnp.full_like(m_ref, NEG)
    acc_ref[...] = jnp.zeros_like(acc_ref)

    def slc_scores(k0):
        return [_dot(ksl_ref[0, pl.ds(k0, tk), :], qaug_ref[...])]

    def slc_masked(s_list, k0):
        key = k0 + lax.broadcasted_iota(jnp.int32, (tk, tq), 0)
        bias = jnp.where(key <= tok_of_tile(tk), 0.0, NEG)
        return [s_list[0] + jnp.concatenate([bias] * grp, axis=1)]

    def slc_consume(k0):
        m_old = m_ref[...]
        m_new = jnp.maximum(m_old, mt_ref[0])
        p = jnp.exp2(s_ref[0] - m_new).astype(BF16)
        vt = jnp.concatenate([vslt_ref[0, :, pl.ds(k0, tk)], _ones_rows(tk)], axis=0)
        acc_ref[...] = jnp.exp2(m_old - m_new) * acc_ref[...] + _dot(vt, p)
        m_ref[...] = m_new

    _pipelined_sweep(t0 // tk, max(1, tq // tk), tk, slc_scores, slc_masked, slc_consume, s_ref, mt_ref)
    o_s = acc_ref[0:HEAD_DIM, :] * (1.0 / acc_ref[HEAD_DIM:HEAD_DIM + 1, :])

    outs = [part_ref[g * HEAD_DIM:(g + 1) * HEAD_DIM, :]
            + gt[3 * g + 1:3 * g + 2] * o_s[:, g * tq:(g + 1) * tq] for g in range(grp)]
    o_ref[0] = jnp.transpose(jnp.concatenate(outs, axis=0))


def _nsa(qat, kcx, vct, kslx, vslt, kwnx, vwnt, gates_t, ov_t):
    b, _, s = qat.shape
    tq = min(NSA_TQ, s)
    assert tq & (tq - 1) == 0 and tq % LANES == 0
    r = kcx.shape[2]
    gw = GROUP_NSA * HEAD_DIM
    cols = GROUP_NSA * tq
    gate_rows = 16
    key_major = pl.BlockSpec((1, s, LANES), lambda i, k, j: (i, 0, k))
    val_major = pl.BlockSpec((1, HEAD_DIM, s), lambda i, k, j: (i, k, 0))
    return pl.pallas_call(
        _nsa_kernel,
        grid=(b, N_KV_NSA, s // tq),
        in_specs=[
            pl.BlockSpec((1, gw, tq), lambda i, k, j: (i, k, j)),
            pl.BlockSpec((1, 1, r, LANES), lambda i, k, j: (i, k, 0, 0)),
            pl.BlockSpec((1, 1, HEAD_DIM, r), lambda i, k, j: (i, k, 0, 0)),
            key_major, val_major, key_major, val_major,
            pl.BlockSpec((1, gate_rows, tq), lambda i, k, j: (i, k * (LANES // gate_rows), j)),
            pl.BlockSpec(ov_t.shape, lambda i, k, j: (0, 0)),
        ],
        out_specs=pl.BlockSpec((1, tq, gw), lambda i, k, j: (i, j, k)),
        out_shape=jax.ShapeDtypeStruct((b, s, NSA_Q_W), F32),
        scratch_shapes=[
            pltpu.VMEM((LANES, cols), BF16),
            pltpu.VMEM((1, cols), F32),
            pltpu.VMEM((HEAD_DIM + BF16_ROWS, cols), F32),
            pltpu.VMEM((1, min(NSA_TK, s), cols), F32),
            pltpu.VMEM((1, 1, cols), F32),
            pltpu.VMEM((gw, tq), F32),
        ],
        compiler_params=_params("parallel", "parallel", "arbitrary"),
        name="nsa",
    )(qat, kcx, vct, kslx, vslt, kwnx, vwnt, gates_t, ov_t)


def _fox_kernel(qt_ref, k_ref, vt_ref, cq_ref, o_ref, qaug_ref, m_ref, acc_ref, s_ref, mt_ref):
    pair = pl.program_id(1)
    qi = pl.program_id(2)
    tq = qt_ref.shape[2]
    tk = min(FOX_TK, k_ref.shape[1])
    t0 = qi * tq
    qt = qt_ref[0]
    row = lax.broadcasted_iota(jnp.int32, (LANES, tq), 0)
    zero = jnp.zeros_like(qt)
    cq = []
    for e in range(2):
        qaug_ref[e, 0:LANES, :] = jnp.where((row >= e * HEAD_DIM) & (row < (e + 1) * HEAD_DIM), qt, zero)
        pick = (row >= e * AUG_STRIDE) & (row < e * AUG_STRIDE + 3)
        qaug_ref[e, LANES:, :] = jnp.where(pick, 1.0, 0.0).astype(BF16)
        cq.append(cq_ref[0, pl.ds(2 * pair + e, 1), :])

    m_ref[...] = jnp.full_like(m_ref, NEG)
    acc_ref[...] = jnp.zeros_like(acc_ref)

    def scores(k0):
        kt = k_ref[0, pl.ds(k0, tk), :]
        return [_dot(kt, qaug_ref[e]) for e in range(2)]

    def masked(s_list, k0):
        key = k0 + lax.broadcasted_iota(jnp.int32, (tk, tq), 0)
        tok = t0 + lax.broadcasted_iota(jnp.int32, (tk, tq), 1)
        return [jnp.where(key <= tok, sv, NEG) for sv in s_list]

    def consume(k0):
        for e in range(2):
            m_old = m_ref[e]
            m_new = jnp.maximum(m_old, mt_ref[e] + cq[e])
            p = jnp.exp2(s_ref[e] + (cq[e] - m_new)).astype(BF16)
            vt = jnp.concatenate([vt_ref[0, e * HEAD_DIM:(e + 1) * HEAD_DIM, pl.ds(k0, tk)],
                                  _ones_rows(tk)], axis=0)
            acc_ref[e] = jnp.exp2(m_old - m_new) * acc_ref[e] + _dot(vt, p)
            m_ref[e] = m_new

    _pipelined_sweep(t0 // tk, max(1, tq // tk), tk, scores, masked, consume, s_ref, mt_ref)
    o_ref[0] = jnp.transpose(jnp.concatenate(
        [acc_ref[e, 0:HEAD_DIM, :] * (1.0 / acc_ref[e, HEAD_DIM:HEAD_DIM + 1, :]) for e in range(2)], axis=0))


def _fox(fqt, fkx, fvt, cq):
    b, _, s = fqt.shape
    tq = min(FOX_TQ, s)
    return pl.pallas_call(
        _fox_kernel,
        grid=(b, N_HEADS_FOX // 2, s // tq),
        in_specs=[
            pl.BlockSpec((1, LANES, tq), lambda i, p, j: (i, p, j)),
            pl.BlockSpec((1, s, 2 * LANES), lambda i, p, j: (i, 0, p)),
            pl.BlockSpec((1, LANES, s), lambda i, p, j: (i, p, 0)),
            pl.BlockSpec((1, N_HEADS_FOX, tq), lambda i, p, j: (i, 0, j)),
        ],
        out_specs=pl.BlockSpec((1, tq, LANES), lambda i, p, j: (i, j, p)),
        out_shape=jax.ShapeDtypeStruct((b, s, FOX_W), F32),
        scratch_shapes=[
            pltpu.VMEM((2, 2 * LANES, tq), BF16),
            pltpu.VMEM((2, 1, tq), F32),
            pltpu.VMEM((2, HEAD_DIM + BF16_ROWS, tq), F32),
            pltpu.VMEM((2, min(FOX_TK, s), tq), F32),
            pltpu.VMEM((2, 1, tq), F32),
        ],
        compiler_params=_params("parallel", "parallel", "arbitrary"),
        name="fox",
    )(fqt, fkx, fvt, cq)


def _tail_kernel(h_ref, oa_ref, ob_ref, p_ref, ga_ref, gb_ref, wa_ref, wb_ref,
                 g_ref, wg_ref, wu_ref, wd_ref, gg_ref, gp_ref, wgate_ref, wproj_ref, o_ref):
    oa = oa_ref[...]
    ob = ob_ref[...]
    na = (oa * _rms(oa) * ga_ref[...]).astype(BF16)
    nb = (ob * _rms(ob) * gb_ref[...]).astype(BF16)
    h = h_ref[...] + _dot(na, wa_ref[...]) + _dot(nb, wb_ref[...])
    h = _swiglu_half_step(h, g_ref, wg_ref, wu_ref, wd_ref)
    hn = (h * _rms(h) * gg_ref[...]).astype(BF16)
    gate = jax.nn.sigmoid(_dot(hn, wgate_ref[...]))
    e = _dot(p_ref[...].astype(BF16), wproj_ref[...])
    o_ref[...] = h + gate * (e * _rms(e) * gp_ref[...])


def _tail(h, oa, ob, p, ga, gb, wa, wb, g, wg, wu, wd, gg, gp, wgate, wproj):
    t, d = h.shape
    tm = min(FFN_TM, t)
    rows = lambda c: pl.BlockSpec((tm, c), lambda i: (i, 0))
    consts = (ga, gb, wa, wb, g, wg, wu, wd, gg, gp, wgate, wproj)
    return pl.pallas_call(
        _tail_kernel,
        grid=(t // tm,),
        in_specs=[rows(d), rows(oa.shape[1]), rows(ob.shape[1]), rows(p.shape[1])]
        + [_resident(c) for c in consts],
        out_specs=rows(d),
        out_shape=jax.ShapeDtypeStruct((t, d), F32),
        compiler_params=_params("parallel"),
        name="tail",
    )(h, oa, ob, p, *consts)


def _expand_heads(w, n):
    w = w.reshape(w.shape[0], n, HEAD_DIM)
    return jnp.concatenate([w, jnp.zeros_like(w)], axis=-1).reshape(w.shape[0], n * LANES)


def _pad_cols(w, width):
    return jnp.pad(w, ((0, 0), (0, width - w.shape[1])))


def _layout_w_in(w_in):
    splits = [int(v) for v in np.cumsum(PROJ_SIZES)[:-1]]
    qa, kc, vc, ksl, vsl, kwn, vwn, ga, qf, kf, vf, fl = jnp.split(w_in, splits, axis=-1)
    per_group = 3 * GROUP_NSA
    ga_x = jnp.concatenate([_pad_cols(ga[:, k * per_group:(k + 1) * per_group], LANES)
                            for k in range(N_KV_NSA)], axis=-1)
    cols = [qa, kc, vc, _expand_heads(ksl, N_KV_NSA), _expand_heads(kwn, N_KV_NSA), vsl, vwn,
            ga_x, qf, kf, vf, _pad_cols(fl, 2 * LANES)]
    return jnp.concatenate(cols, axis=-1).astype(BF16)


def _rope_tables(seq):
    pos = np.arange(seq, dtype=np.float64)
    inv = ROPE_THETA ** (-np.arange(0, ROT_DIM, 2, dtype=np.float64) / ROT_DIM)
    ang = pos[:, None] * inv[None, :]
    cos, sin = np.cos(ang), np.sin(ang)
    half = ROT_DIM // 2
    rest = HEAD_DIM - ROT_DIM
    ones = np.ones((seq, rest))
    zeros = np.zeros((seq, rest))
    zh = np.zeros((seq, half))
    cos_t = np.concatenate([cos, cos, ones], axis=-1)
    sin_lo = np.concatenate([-sin, zh, zeros], axis=-1)
    sin_hi = np.concatenate([zh, sin, zeros], axis=-1)
    tile2 = lambda t: np.concatenate([t, t], axis=-1)
    return jnp.asarray(np.stack([tile2(cos_t), tile2(sin_lo), tile2(sin_hi)]), F32)


def _layout_cmp_w1(w1):
    hid = w1.shape[1]
    w = w1.reshape(2, CMP_STRIDE, HEAD_DIM, hid)
    z = jnp.zeros_like(w)
    per_head = [jnp.concatenate([w, z], axis=2), jnp.concatenate([z, w], axis=2)]
    return jnp.stack(per_head).reshape(N_KV_NSA, 2, CMP_STRIDE * LANES, hid).astype(BF16)


def _layout_cmp_pos(pos):
    p = pos.reshape(2, CMP_STRIDE, HEAD_DIM)
    p = jnp.concatenate([p, p], axis=-1).reshape(2, 1, CMP_STRIDE * LANES)
    return jnp.broadcast_to(p, (2, 8, CMP_STRIDE * LANES)).astype(BF16)


def _overlap_t(seq):
    rows = seq // CMP_STRIDE
    n_sel = seq // SEL_BLOCK
    cmp_start = np.arange(rows) * CMP_STRIDE
    sel_start = np.arange(n_sel) * SEL_BLOCK
    ov = ((cmp_start[None, :] <= sel_start[:, None] + SEL_BLOCK - 1)
          & (cmp_start[None, :] + CMP_BLOCK - 1 >= sel_start[:, None]))
    ov[:, (seq - CMP_BLOCK) // CMP_STRIDE + 1:] = False
    return jnp.asarray(ov.astype(np.float32), BF16)


def _row(v, width=None):
    v = v.reshape(1, -1).astype(F32)
    return v if width is None else _pad_cols(v, width)


def kernel(x, p, ffn1_norm, ffn1_wg, ffn1_wu, ffn1_wd, mix_norm, w_in, b_forget, q_norm_nsa, k_norm_cmp, k_norm_slc, k_norm_win, cmp_pos_k, cmp_pos_v, cmp_k_w1, cmp_k_w2, cmp_v_w1, cmp_v_w2, q_norm_fox, k_norm_fox, out_norm_nsa, out_norm_fox, w_out, ffn2_norm, ffn2_wg, ffn2_wu, ffn2_wd, ple_gate_norm, ple_w_gate, ple_w_proj, ple_norm):
    b, s, d = x.shape
    assert s // SEL_BLOCK <= HEAD_DIM and (s // SEL_BLOCK) % 8 == 0
    assert s % max(NSA_TK, FOX_TK, FOX_TQ) == 0
    depth = ffn1_norm.shape[0]
    t = b * s
    rope_tab = _rope_tables(s)
    ov_t = _overlap_t(s)
    tile4 = lambda g: jnp.concatenate([g] * 4).reshape(1, 2 * LANES).astype(F32)
    h = x.reshape(t, d)
    for i in range(depth):
        h = _ffn(h, _row(ffn1_norm[i]), ffn1_wg[i].astype(BF16), ffn1_wu[i].astype(BF16),
                 ffn1_wd[i].astype(BF16))

        gains = jnp.concatenate([tile4(q_norm_nsa[i]), tile4(k_norm_slc[i]), tile4(k_norm_win[i]),
                                 tile4(q_norm_fox[i]), tile4(k_norm_fox[i]),
                                 jnp.zeros((3, 2 * LANES), F32)], axis=0)
        (qat, kc, vc, kslx, vslt, kwnx, vwnt, gates_t, fqt, fkx, fvt, cq) = _inproj(
            h.reshape(b, s, d), _row(mix_norm[i]), _layout_w_in(w_in[i]),
            _row(b_forget[i], LANES), gains, rope_tab)

        rows = s // CMP_STRIDE

        kcx, vcx = _compress(
            kc.reshape(b, rows, CMP_STRIDE * LANES), vc.reshape(b, rows, CMP_STRIDE * LANES),
            _layout_cmp_w1(cmp_k_w1[i]), _layout_cmp_w1(cmp_v_w1[i]),
            _layout_cmp_pos(cmp_pos_k[i]), _layout_cmp_pos(cmp_pos_v[i]),
            _pad_cols(cmp_k_w2[i], LANES).astype(BF16), _pad_cols(cmp_v_w2[i], LANES).astype(BF16),
            _row(k_norm_cmp[i], LANES))
        vct = jnp.transpose(vcx[..., :HEAD_DIM], (0, 1, 3, 2))

        o_a = _nsa(qat, kcx, vct, kslx, vslt, kwnx, vwnt, gates_t, ov_t)
        o_b = _fox(fqt, fkx, fvt, cq)

        w_o = w_out[i].astype(BF16)
        h = _tail(h, o_a.reshape(t, NSA_Q_W), o_b.reshape(t, FOX_W), p[i].reshape(t, -1),
                  _row(out_norm_nsa[i]), _row(out_norm_fox[i]), w_o[:NSA_Q_W], w_o[NSA_Q_W:],
                  _row(ffn2_norm[i]), ffn2_wg[i].astype(BF16), ffn2_wu[i].astype(BF16),
                  ffn2_wd[i].astype(BF16), _row(ple_gate_norm[i]), _row(ple_norm[i]),
                  ple_w_gate[i].astype(BF16), ple_w_proj[i].astype(BF16))
    return h.reshape(b, s, d)
```

```python
import functools

import numpy as np
import jax
import jax.numpy as jnp
from jax import lax
from jax.experimental import pallas as pl
from jax.experimental.pallas import tpu as pltpu

F32 = jnp.float32
BF16 = jnp.bfloat16

D_MODEL = 1024
HEAD_DIM = 64
N_HEADS_NSA = 8
N_KV_NSA = 2
GROUP_NSA = N_HEADS_NSA // N_KV_NSA
N_HEADS_FOX = 8
NSA_Q_W = N_HEADS_NSA * HEAD_DIM
NSA_KV_W = N_KV_NSA * HEAD_DIM
FOX_W = N_HEADS_FOX * HEAD_DIM
PROJ_SIZES = (NSA_Q_W, NSA_KV_W, NSA_KV_W, NSA_KV_W, NSA_KV_W, NSA_KV_W, NSA_KV_W,
              3 * N_HEADS_NSA, FOX_W, FOX_W, FOX_W, N_HEADS_FOX)
D_FF = 2816
D_PLE = 256
ROPE_THETA = 500000.0
ROT_DIM = HEAD_DIM // 4
CMP_BLOCK = 32
CMP_STRIDE = 16
CMP_HIDDEN = 256
SEL_BLOCK = 64
SEL_SHIFT = 6
SEL_TOPK = 16
WINDOW = 512
FORCED_SCORE = 1e6
EPS = 1e-6
NEG = -1e30
LOG2E = 1.4426950408889634

LANES = 128
LANE_SHIFT = 7
SUBLANES = 8
SUBLANE_SHIFT = 3
HEAD_SHIFT = 6
BF16_ROWS = 16
VMEM_LIMIT = 56 * 1024 * 1024
AUG_STRIDE = 8

C_QA = 0
C_KCVC = C_QA + NSA_Q_W
C_KSL = C_KCVC + 2 * LANES
C_KWN = C_KSL + N_KV_NSA * LANES
C_VSLWN = C_KWN + N_KV_NSA * LANES
C_GATE = C_VSLWN + 2 * LANES
C_FQ = C_GATE + N_KV_NSA * LANES
C_FK = C_FQ + FOX_W
C_FV = C_FK + FOX_W
C_FL = C_FV + FOX_W
C_END = C_FL + 2 * LANES

FFN_TM = 512
FFN_TF = 256
PROJ_TS = 512
NSA_TQ = 256
NSA_TK = 512
FOX_TQ = 512
FOX_TK = 512


def _dot(a, b):
    return jnp.dot(a, b, preferred_element_type=F32)


def _dot_nt(a, b):
    return lax.dot_general(a, b, (((1,), (1,)), ((), ())), preferred_element_type=F32)


def _params(*sem):
    return pltpu.CompilerParams(dimension_semantics=sem, vmem_limit_bytes=VMEM_LIMIT)


def _rms(x):
    return lax.rsqrt(jnp.mean(x * x, axis=-1, keepdims=True) + EPS)


def _split3(x):
    hi = x.astype(BF16)
    r1 = x - hi.astype(F32)
    mid = r1.astype(BF16)
    lo = (r1 - mid.astype(F32)).astype(BF16)
    return hi, mid, lo


def _swiglu_half_step(x, g_ref, wg_ref, wu_ref, wd_ref):
    xn = (x * _rms(x) * g_ref[...]).astype(BF16)
    acc = None
    for j in range(wg_ref.shape[1] // FFN_TF):
        sl = slice(j * FFN_TF, (j + 1) * FFN_TF)
        gate = _dot(xn, wg_ref[:, sl])
        up = _dot(xn, wu_ref[:, sl])
        act = (gate * jax.nn.sigmoid(gate) * up).astype(BF16)
        part = _dot(act, wd_ref[sl, :])
        acc = part if acc is None else acc + part
    return x + 0.5 * acc


def _ffn_kernel(x_ref, g_ref, wg_ref, wu_ref, wd_ref, o_ref):
    o_ref[...] = _swiglu_half_step(x_ref[...], g_ref, wg_ref, wu_ref, wd_ref)


def _resident(a):
    return pl.BlockSpec(a.shape, lambda *_: (0,) * a.ndim, pipeline_mode=pl.Buffered(1))


def _ffn(x, g, wg, wu, wd):
    t, d = x.shape
    tm = min(FFN_TM, t)
    rows = pl.BlockSpec((tm, d), lambda i: (i, 0))
    return pl.pallas_call(
        _ffn_kernel,
        grid=(t // tm,),
        in_specs=[rows, _resident(g), _resident(wg), _resident(wu), _resident(wd)],
        out_specs=rows,
        out_shape=jax.ShapeDtypeStruct((t, d), F32),
        compiler_params=_params("parallel"),
        name="ffn",
    )(x, g, wg, wu, wd)


def _log_sigmoid(x):
    return -(jnp.maximum(-x, 0.0) + jnp.log(1.0 + jnp.exp(-jnp.abs(x))))


def _inproj_kernel(h_ref, g_ref, w_ref, bf_ref, gains_ref, rope_ref,
                   qa_ref, kc_ref, vc_ref, ksl_ref, vsl_ref, kwn_ref, vwn_ref,
                   gate_ref, fq_ref, fk_ref, fv_ref, cum_ref, carry_ref):
    si = pl.program_id(1)
    ts = h_ref.shape[1]
    wide = 2 * LANES
    x = h_ref[0]
    a = (x * _rms(x) * g_ref[...]).astype(BF16)

    row = lax.broadcasted_iota(jnp.int32, (2 * wide, wide), 0) & (wide - 1)
    col = lax.broadcasted_iota(jnp.int32, (2 * wide, wide), 1)
    same = lambda shift: jnp.right_shift(row, shift) == jnp.right_shift(col, shift)
    ones_head = jnp.where(same(HEAD_SHIFT), 1.0, 0.0).astype(BF16)
    ones_chunk = jnp.where(same(LANE_SHIFT), 1.0, 0.0).astype(BF16)
    cos_t, sin_lo, sin_hi = rope_ref[0], rope_ref[1], rope_ref[2]

    def proj(c0):
        return _dot(a, w_ref[:, c0:c0 + wide])

    def norm(u, ones_mat, gain_row):
        u2 = u * u
        hi = u2.astype(BF16)
        lo = (u2 - hi.astype(F32)).astype(BF16)
        ms = _dot(jnp.concatenate([hi, lo], axis=1), ones_mat) * (1.0 / HEAD_DIM)
        return u * lax.rsqrt(ms + EPS) * gains_ref[gain_row:gain_row + 1, :]

    def rope_half(u):
        return (u * cos_t + pltpu.roll(u, LANES - ROT_DIM // 2, 1) * sin_lo
                + pltpu.roll(u, ROT_DIM // 2, 1) * sin_hi)

    def rope(u):
        return jnp.concatenate([rope_half(u[:, :LANES]), rope_half(u[:, LANES:])], axis=1)

    def feature_major(u):
        return jnp.transpose(u.astype(BF16))

    scale = HEAD_DIM ** -0.5 * LOG2E
    pos = si * ts + lax.broadcasted_iota(jnp.int32, (ts, wide), 0)
    lane = lax.broadcasted_iota(jnp.int32, (ts, wide), 1) & (LANES - 1)
    sel_onehot = jnp.where(lane == HEAD_DIM + jnp.right_shift(pos, SEL_SHIFT), 1.0, 0.0)

    @pl.when(si == 0)
    def _():
        carry_ref[...] = jnp.zeros_like(carry_ref)

    aug = []

    def forget_stage(u):
        lf = _log_sigmoid(u[:, :LANES] + bf_ref[...])
        r_i = lax.broadcasted_iota(jnp.int32, (ts, ts), 0)
        c_i = lax.broadcasted_iota(jnp.int32, (ts, ts), 1)
        tri = jnp.where(r_i >= c_i, 1.0, 0.0).astype(BF16)
        hi, mid, lo = _split3(lf)
        c = _dot(tri, hi) + _dot(tri, mid) + _dot(tri, lo) + carry_ref[0:1, :]
        carry_ref[...] = jnp.broadcast_to(c[ts - 1:ts, :], carry_ref.shape)
        c = c * LOG2E
        cum_ref[0] = jnp.transpose(c)[:N_HEADS_FOX]
        n_pairs = N_HEADS_FOX // 2
        p_row = lax.broadcasted_iota(jnp.int32, (3 * LANES, n_pairs * LANES), 0)
        p_col = lax.broadcasted_iota(jnp.int32, (3 * LANES, n_pairs * LANES), 1)
        head = p_row & (LANES - 1)
        part = jnp.right_shift(p_row, LANE_SHIFT)
        target = jnp.right_shift(head, 1) * LANES + (head & 1) * AUG_STRIDE + part
        place = jnp.where((p_col == target) & (head < N_HEADS_FOX), 1.0, 0.0).astype(BF16)
        aug.append(_dot(jnp.concatenate(_split3(-c), axis=1), place).astype(BF16))

    def nsa_q_stage(c, u):
        qa_ref[0, c * wide:(c + 1) * wide, :] = feature_major(rope(norm(u, ones_head, 0)) * scale)

    def kcvc_stage(u):
        kc_ref[0] = rope_half(u[:, :LANES]).astype(BF16)
        vc_ref[0] = u[:, LANES:].astype(BF16)

    def values_stage(u):
        u = feature_major(u)
        vsl_ref[0] = u[:LANES]
        vwn_ref[0] = u[LANES:]

    def ksl_stage(u):
        ksl_ref[0] = (rope(norm(u, ones_chunk, 1)) + sel_onehot).astype(BF16)

    def kwn_stage(u):
        kwn_ref[0] = rope(norm(u, ones_chunk, 2)).astype(BF16)

    def gate_stage(u):
        gate_ref[0] = jnp.transpose(jax.nn.sigmoid(u))

    def fox_q_stage(c, u):
        fq_ref[0, c * wide:(c + 1) * wide, :] = feature_major(norm(u, ones_head, 3) * scale)

    def fox_v_stage(c, u):
        fv_ref[0, c * wide:(c + 1) * wide, :] = feature_major(u)

    def fox_k_stage(c, u):
        fk = norm(u, ones_head, 4).astype(BF16)
        for e in range(2):
            pair = 2 * c + e
            fk_ref[0, :, 2 * pair * LANES:(2 * pair + 1) * LANES] = fk[:, e * LANES:(e + 1) * LANES]
            fk_ref[0, :, (2 * pair + 1) * LANES:(2 * pair + 2) * LANES] = (
                aug[0][:, pair * LANES:(pair + 1) * LANES])

    stages = [(C_FL, forget_stage), (C_KCVC, kcvc_stage), (C_VSLWN, values_stage),
              (C_KSL, ksl_stage), (C_KWN, kwn_stage), (C_GATE, gate_stage)]
    for c in range(NSA_Q_W // wide):
        stages.append((C_QA + c * wide, functools.partial(nsa_q_stage, c)))
    for c in range(FOX_W // wide):
        stages.append((C_FQ + c * wide, functools.partial(fox_q_stage, c)))
        stages.append((C_FV + c * wide, functools.partial(fox_v_stage, c)))
        stages.append((C_FK + c * wide, functools.partial(fox_k_stage, c)))

    u_next = proj(stages[0][0])
    for idx, (_, epilogue) in enumerate(stages):
        u = u_next
        if idx + 1 < len(stages):
            u_next = proj(stages[idx + 1][0])
        epilogue(u)


def _inproj(h, g, w_all, bf_row, gains, rope_tab):
    b, s, d = h.shape
    ts = min(PROJ_TS, s)
    tok = lambda c: pl.BlockSpec((1, ts, c), lambda i, j: (i, j, 0))
    const = lambda shape: pl.BlockSpec(shape, lambda i, j: (0,) * len(shape))
    feat = lambda c: pl.BlockSpec((1, c, ts), lambda i, j: (i, 0, j))
    kvx = N_KV_NSA * LANES
    specs = [(NSA_Q_W, BF16, True), (LANES, BF16, False), (LANES, BF16, False), (kvx, BF16, False),
             (LANES, BF16, True), (kvx, BF16, False), (LANES, BF16, True), (kvx, F32, True),
             (FOX_W, BF16, True), (2 * FOX_W, BF16, False), (FOX_W, BF16, True),
             (N_HEADS_FOX, F32, True)]
    out_shape = [jax.ShapeDtypeStruct((b, c, s) if fm else (b, s, c), dt) for c, dt, fm in specs]
    return pl.pallas_call(
        _inproj_kernel,
        grid=(b, s // ts),
        in_specs=[
            tok(d),
            const((1, d)),
            const((d, C_END)),
            const((1, LANES)),
            const((SUBLANES, 2 * LANES)),
            pl.BlockSpec((3, ts, LANES), lambda i, j: (0, j, 0)),
        ],
        out_specs=[feat(c) if fm else tok(c) for c, _, fm in specs],
        out_shape=out_shape,
        scratch_shapes=[pltpu.VMEM((SUBLANES, LANES), F32)],
        compiler_params=_params("parallel", "arbitrary"),
        name="inproj",
    )(h, g, w_all, bf_row, gains, rope_tab)


def _compress_kernel(rk_ref, rv_ref, w1k_ref, w1v_ref,
                     pk_ref, pv_ref, w2k_ref, w2v_ref, gain_ref, kc_ref, vc_ref):
    def mlp(r_ref, w1_ref, p_ref, w2_ref, kh):
        top, bot = w1_ref[kh, 0], w1_ref[kh, 1]
        rows = r_ref[0]
        bias = _dot(p_ref[0], top) + _dot(p_ref[1], bot)
        second = pltpu.roll(_dot(rows, bot), rows.shape[0] - 1, 0)
        hid = _dot(rows, top) + second + bias[0:1, :]
        hid = hid * jax.nn.sigmoid(hid)
        return _dot(hid.astype(BF16), w2_ref[...])

    for kh in range(N_KV_NSA):
        kc = mlp(rk_ref, w1k_ref, pk_ref, w2k_ref, kh)
        ms = jnp.sum(kc * kc, axis=-1, keepdims=True) * (1.0 / HEAD_DIM)
        kc_ref[0, kh] = (kc * lax.rsqrt(ms + EPS) * gain_ref[...]).astype(BF16)
        vc_ref[0, kh] = mlp(rv_ref, w1v_ref, pv_ref, w2v_ref, kh).astype(BF16)


def _compress(rk, rv, w1k, w1v, pk, pv, w2k, w2v, gain):
    b, r, w = rk.shape
    rows = pl.BlockSpec((1, r, w), lambda i: (i, 0, 0))
    const = lambda a: pl.BlockSpec(a.shape, lambda i: (0,) * a.ndim)
    out = jax.ShapeDtypeStruct((b, N_KV_NSA, r, LANES), BF16)
    out_spec = pl.BlockSpec((1, N_KV_NSA, r, LANES), lambda i: (i, 0, 0, 0))
    return pl.pallas_call(
        _compress_kernel,
        grid=(b,),
        in_specs=[rows, rows, const(w1k), const(w1v), const(pk), const(pv),
                  const(w2k), const(w2v), const(gain)],
        out_specs=[out_spec, out_spec],
        out_shape=[out, out],
        compiler_params=_params("parallel"),
        name="compress",
    )(rk, rv, w1k, w1v, pk, pv, w2k, w2v, gain)


def _ones_rows(width):
    return jnp.ones((BF16_ROWS, width), BF16)


def _pipelined_sweep(n_full, n_tail, tk, scores, masked, consume, s_ref, mt_ref):
    def stash(s_list):
        for e, s in enumerate(s_list):
            s_ref[e] = s
            mt_ref[e] = jnp.max(s, axis=0, keepdims=True)

    def body(j, carry):
        k0 = pl.multiple_of(j * tk, tk)
        s_new = scores(k0)
        consume(pl.multiple_of(k0 - tk, tk))
        stash(s_new)
        return carry

    def tail_step(k0):
        s_new = scores(k0)
        consume(pl.multiple_of(k0 - tk, tk))
        stash(masked(s_new, k0))

    k_tail = pl.multiple_of(n_full * tk, tk)

    @pl.when(n_full == 0)
    def _():
        stash(masked(scores(k_tail), k_tail))

    @pl.when(n_full > 0)
    def _():
        stash(scores(0))
        lax.fori_loop(1, n_full, body, 0)
        tail_step(k_tail)

    for i in range(1, n_tail):
        tail_step(pl.multiple_of(k_tail + i * tk, tk))
    consume(pl.multiple_of(k_tail + (n_tail - 1) * tk, tk))


def _nsa_kernel(qt_ref, kc_ref, vct_ref, ksl_ref, vslt_ref, kwn_ref, vwnt_ref, gt_ref, ov_ref,
                o_ref, qaug_ref, m_ref, acc_ref, s_ref, mt_ref, part_ref):
    qi = pl.program_id(2)
    tq = qt_ref.shape[2]
    s_len = ksl_ref.shape[1]
    n_cmp_rows = kc_ref.shape[2]
    n_sel = ov_ref.shape[0]
    top = min(SEL_TOPK, n_sel)
    grp = GROUP_NSA
    cols = grp * tq
    tk = min(NSA_TK, s_len)
    t0 = qi * tq

    def tok_of_col(rows):
        return t0 + (lax.broadcasted_iota(jnp.int32, (rows, cols), 1) & (tq - 1))

    def tok_of_tile(rows):
        return t0 + lax.broadcasted_iota(jnp.int32, (rows, tq), 1)

    def softmax_cols(s, keep):
        s = s + jnp.concatenate([jnp.where(keep, 0.0, NEG)] * grp, axis=1)
        mx = jnp.max(s, axis=0, keepdims=True)
        e = jnp.exp2(s - mx)
        den = jnp.sum(e, axis=0, keepdims=True)
        return e, jnp.where(mx > 0.5 * NEG, 1.0 / den, 0.0)

    qt = qt_ref[0]
    qaug_ref[0:HEAD_DIM, :] = jnp.concatenate(
        [qt[g * HEAD_DIM:(g + 1) * HEAD_DIM] for g in range(grp)], axis=1)
    qaug_ref[HEAD_DIM:, :] = jnp.zeros((HEAD_DIM, cols), BF16)

    span = min(WINDOW + tq, s_len)
    start = pl.multiple_of(jnp.maximum(t0 + tq - span, 0), LANES)
    s_c = _dot(kc_ref[0, 0], qaug_ref[...])
    s_w = _dot(kwn_ref[0, pl.ds(start, span), :], qaug_ref[...])

    n_c = lax.broadcasted_iota(jnp.int32, (n_cmp_rows, tq), 0)
    keep_c = n_c * CMP_STRIDE + (CMP_BLOCK - 1) <= tok_of_tile(n_cmp_rows)
    e_c, inv_c = softmax_cols(s_c, keep_c)
    o_c = _dot(vct_ref[0, 0], e_c.astype(BF16)) * inv_c

    t_w = tok_of_tile(span)
    k_w = start + lax.broadcasted_iota(jnp.int32, (span, tq), 0)
    keep_w = (k_w <= t_w) & (t_w - k_w < WINDOW)
    e_w, inv_w = softmax_cols(s_w, keep_w)
    o_w = _dot(vwnt_ref[0, :, pl.ds(start, span)], e_w.astype(BF16)) * inv_w

    gt = gt_ref[0]
    for g in range(grp):
        sl = slice(g * tq, (g + 1) * tq)
        part_ref[g * HEAD_DIM:(g + 1) * HEAD_DIM, :] = (
            gt[3 * g:3 * g + 1] * o_c[:, sl] + gt[3 * g + 2:3 * g + 3] * o_w[:, sl])

    p_sum = e_c[:, 0:tq] * inv_c[:, 0:tq]
    for g in range(1, grp):
        p_sum = p_sum + e_c[:, g * tq:(g + 1) * tq] * inv_c[:, g * tq:(g + 1) * tq]
    p_hi = p_sum.astype(BF16)
    p_lo = (p_sum - p_hi.astype(F32)).astype(BF16)
    ov = ov_ref[...]
    imp = _dot(ov, p_hi) + _dot(ov, p_lo)
    j_blk = lax.broadcasted_iota(jnp.int32, (n_sel, tq), 0)
    cur = jnp.right_shift(t0 + lax.broadcasted_iota(jnp.int32, (n_sel, tq), 1), SEL_SHIFT)
    forced = (j_blk == 0) | (j_blk == cur) | (j_blk == cur - 1)
    imp = jnp.where(j_blk <= cur, jnp.where(forced, FORCED_SCORE, imp), -1.0)
    rows_g = SUBLANES
    sub = lax.broadcasted_iota(jnp.int32, (rows_g, tq), 0)
    n_grp = n_sel // rows_g

    def write_selection_bias(n_active):
        groups = [imp[rows_g * v:rows_g * (v + 1)] for v in range(n_active)]
        ranks = [jnp.zeros((rows_g, tq), F32) for _ in groups]
        for i in range(rows_g * n_active):
            r_i = jnp.broadcast_to(imp[i:i + 1, :], (rows_g, tq))
            for v, g_v in enumerate(groups):
                if i < rows_g * v:
                    beats = r_i >= g_v
                elif i >= rows_g * (v + 1):
                    beats = r_i > g_v
                else:
                    beats = (r_i > g_v) | ((r_i == g_v) & (sub > i - rows_g * v))
                ranks[v] = ranks[v] + jnp.where(beats, 1.0, 0.0)
        rows = [jnp.where((r_v < top) & (g_v >= 0.0), 0.0, NEG) for r_v, g_v in zip(ranks, groups)]
        rows += [jnp.full((rows_g, tq), NEG, F32)] * (n_grp - n_active)
        if n_sel < HEAD_DIM:
            rows.append(jnp.zeros((HEAD_DIM - n_sel, tq), F32))
        bias_t = jnp.concatenate(rows, axis=0).astype(BF16)
        qaug_ref[HEAD_DIM:, :] = jnp.concatenate([bias_t] * grp, axis=1)

    groups_needed = jnp.right_shift(t0 + tq - 1, SEL_SHIFT + SUBLANE_SHIFT) + 1
    for n_active in range(1, n_grp + 1):
        pl.when(groups_needed == n_active)(functools.partial(write_selection_bias, n_active))

    m_ref[...] = jnp.full_like(m_ref, NEG)
    acc_ref[...] = jnp.zeros_like(acc_ref)

    def slc_scores(k0):
        return [_dot(ksl_ref[0, pl.ds(k0, tk), :], qaug_ref[...])]

    def slc_masked(s_list, k0):
        key = k0 + lax.broadcasted_iota(jnp.int32, (tk, tq), 0)
        bias = jnp.where(key <= tok_of_tile(tk), 0.0, NEG)
        return [s_list[0] + jnp.concatenate([bias] * grp, axis=1)]

    def slc_consume(k0):
        m_old = m_ref[...]
        m_new = jnp.maximum(m_old, mt_ref[0])
        p = jnp.exp2(s_ref[0] - m_new).astype(BF16)
        vt = jnp.concatenate([vslt_ref[0, :, pl.ds(k0, tk)], _ones_rows(tk)], axis=0)
        acc_ref[...] = jnp.exp2(m_old - m_new) * acc_ref[...] + _dot(vt, p)
        m_ref[...] = m_new

    _pipelined_sweep(t0 // tk, max(1, tq // tk), tk, slc_scores, slc_masked, slc_consume, s_ref, mt_ref)
    o_s = acc_ref[0:HEAD_DIM, :] * (1.0 / acc_ref[HEAD_DIM:HEAD_DIM + 1, :])

    outs = [part_ref[g * HEAD_DIM:(g + 1) * HEAD_DIM, :]
            + gt[3 * g + 1:3 * g + 2] * o_s[:, g * tq:(g + 1) * tq] for g in range(grp)]
    o_ref[0] = jnp.transpose(jnp.concatenate(outs, axis=0))


def _nsa(qat, kcx, vct, kslx, vslt, kwnx, vwnt, gates_t, ov_t):
    b, _, s = qat.shape
    tq = min(NSA_TQ, s)
    assert tq & (tq - 1) == 0 and tq % LANES == 0
    r = kcx.shape[2]
    gw = GROUP_NSA * HEAD_DIM
    cols = GROUP_NSA * tq
    gate_rows = -(-3 * GROUP_NSA // SUBLANES) * SUBLANES
    key_major = pl.BlockSpec((1, s, LANES), lambda i, k, j: (i, 0, k))
    val_major = pl.BlockSpec((1, HEAD_DIM, s), lambda i, k, j: (i, k, 0))
    return pl.pallas_call(
        _nsa_kernel,
        grid=(b, N_KV_NSA, s // tq),
        in_specs=[
            pl.BlockSpec((1, gw, tq), lambda i, k, j: (i, k, j)),
            pl.BlockSpec((1, 1, r, LANES), lambda i, k, j: (i, k, 0, 0)),
            pl.BlockSpec((1, 1, HEAD_DIM, r), lambda i, k, j: (i, k, 0, 0)),
            key_major, val_major, key_major, val_major,
            pl.BlockSpec((1, gate_rows, tq), lambda i, k, j: (i, k * (LANES // gate_rows), j)),
            pl.BlockSpec(ov_t.shape, lambda i, k, j: (0, 0)),
        ],
        out_specs=pl.BlockSpec((1, tq, gw), lambda i, k, j: (i, j, k)),
        out_shape=jax.ShapeDtypeStruct((b, s, NSA_Q_W), F32),
        scratch_shapes=[
            pltpu.VMEM((LANES, cols), BF16),
            pltpu.VMEM((1, cols), F32),
            pltpu.VMEM((HEAD_DIM + BF16_ROWS, cols), F32),
            pltpu.VMEM((1, min(NSA_TK, s), cols), F32),
            pltpu.VMEM((1, 1, cols), F32),
            pltpu.VMEM((gw, tq), F32),
        ],
        compiler_params=_params("parallel", "parallel", "arbitrary"),
        name="nsa",
    )(qat, kcx, vct, kslx, vslt, kwnx, vwnt, gates_t, ov_t)


def _fox_kernel(qt_ref, k_ref, vt_ref, cq_ref, o_ref, qaug_ref, m_ref, acc_ref, s_ref, mt_ref):
    pair = pl.program_id(1)
    qi = pl.program_id(2)
    tq = qt_ref.shape[2]
    tk = min(FOX_TK, k_ref.shape[1])
    t0 = qi * tq
    qt = qt_ref[0]
    row = lax.broadcasted_iota(jnp.int32, (LANES, tq), 0)
    zero = jnp.zeros_like(qt)
    cq = []
    for e in range(2):
        qaug_ref[e, 0:LANES, :] = jnp.where((row >= e * HEAD_DIM) & (row < (e + 1) * HEAD_DIM), qt, zero)
        pick = (row >= e * AUG_STRIDE) & (row < e * AUG_STRIDE + 3)
        qaug_ref[e, LANES:, :] = jnp.where(pick, 1.0, 0.0).astype(BF16)
        cq.append(cq_ref[0, pl.ds(2 * pair + e, 1), :])

    m_ref[...] = jnp.full_like(m_ref, NEG)
    acc_ref[...] = jnp.zeros_like(acc_ref)

    def scores(k0):
        kt = k_ref[0, pl.ds(k0, tk), :]
        return [_dot(kt, qaug_ref[e]) for e in range(2)]

    def masked(s_list, k0):
        key = k0 + lax.broadcasted_iota(jnp.int32, (tk, tq), 0)
        tok = t0 + lax.broadcasted_iota(jnp.int32, (tk, tq), 1)
        return [jnp.where(key <= tok, sv, NEG) for sv in s_list]

    def consume(k0):
        for e in range(2):
            m_old = m_ref[e]
            m_new = jnp.maximum(m_old, mt_ref[e] + cq[e])
            p = jnp.exp2(s_ref[e] + (cq[e] - m_new)).astype(BF16)
            vt = jnp.concatenate([vt_ref[0, e * HEAD_DIM:(e + 1) * HEAD_DIM, pl.ds(k0, tk)],
                                  _ones_rows(tk)], axis=0)
            acc_ref[e] = jnp.exp2(m_old - m_new) * acc_ref[e] + _dot(vt, p)
            m_ref[e] = m_new

    _pipelined_sweep(t0 // tk, max(1, tq // tk), tk, scores, masked, consume, s_ref, mt_ref)
    o_ref[0] = jnp.transpose(jnp.concatenate(
        [acc_ref[e, 0:HEAD_DIM, :] * (1.0 / acc_ref[e, HEAD_DIM:HEAD_DIM + 1, :]) for e in range(2)], axis=0))


def _fox(fqt, fkx, fvt, cq):
    b, _, s = fqt.shape
    tq = min(FOX_TQ, s)
    return pl.pallas_call(
        _fox_kernel,
        grid=(b, N_HEADS_FOX // 2, s // tq),
        in_specs=[
            pl.BlockSpec((1, LANES, tq), lambda i, p, j: (i, p, j)),
            pl.BlockSpec((1, s, 2 * LANES), lambda i, p, j: (i, 0, p)),
            pl.BlockSpec((1, LANES, s), lambda i, p, j: (i, p, 0)),
            pl.BlockSpec((1, N_HEADS_FOX, tq), lambda i, p, j: (i, 0, j)),
        ],
        out_specs=pl.BlockSpec((1, tq, LANES), lambda i, p, j: (i, j, p)),
        out_shape=jax.ShapeDtypeStruct((b, s, FOX_W), F32),
        scratch_shapes=[
            pltpu.VMEM((2, 2 * LANES, tq), BF16),
            pltpu.VMEM((2, 1, tq), F32),
            pltpu.VMEM((2, HEAD_DIM + BF16_ROWS, tq), F32),
            pltpu.VMEM((2, min(FOX_TK, s), tq), F32),
            pltpu.VMEM((2, 1, tq), F32),
        ],
        compiler_params=_params("parallel", "parallel", "arbitrary"),
        name="fox",
    )(fqt, fkx, fvt, cq)


def _tail_kernel(h_ref, oa_ref, ob_ref, p_ref, ga_ref, gb_ref, wa_ref, wb_ref,
                 g_ref, wg_ref, wu_ref, wd_ref, gg_ref, gp_ref, wgate_ref, wproj_ref, o_ref):
    oa = oa_ref[...]
    ob = ob_ref[...]
    na = (oa * _rms(oa) * ga_ref[...]).astype(BF16)
    nb = (ob * _rms(ob) * gb_ref[...]).astype(BF16)
    h = h_ref[...] + _dot(na, wa_ref[...]) + _dot(nb, wb_ref[...])
    h = _swiglu_half_step(h, g_ref, wg_ref, wu_ref, wd_ref)
    hn = (h * _rms(h) * gg_ref[...]).astype(BF16)
    gate = jax.nn.sigmoid(_dot(hn, wgate_ref[...]))
    e = _dot(p_ref[...].astype(BF16), wproj_ref[...])
    o_ref[...] = h + gate * (e * _rms(e) * gp_ref[...])


def _tail(h, oa, ob, p, ga, gb, wa, wb, g, wg, wu, wd, gg, gp, wgate, wproj):
    t, d = h.shape
    tm = min(FFN_TM, t)
    rows = lambda c: pl.BlockSpec((tm, c), lambda i: (i, 0))
    consts = (ga, gb, wa, wb, g, wg, wu, wd, gg, gp, wgate, wproj)
    return pl.pallas_call(
        _tail_kernel,
        grid=(t // tm,),
        in_specs=[rows(d), rows(oa.shape[1]), rows(ob.shape[1]), rows(p.shape[1])]
        + [_resident(c) for c in consts],
        out_specs=rows(d),
        out_shape=jax.ShapeDtypeStruct((t, d), F32),
        compiler_params=_params("parallel"),
        name="tail",
    )(h, oa, ob, p, *consts)


def _expand_heads(w, n):
    w = w.reshape(w.shape[0], n, HEAD_DIM)
    return jnp.concatenate([w, jnp.zeros_like(w)], axis=-1).reshape(w.shape[0], n * LANES)


def _pad_cols(w, width):
    return jnp.pad(w, ((0, 0), (0, width - w.shape[1])))


def _layout_w_in(w_in):
    splits = [int(v) for v in np.cumsum(PROJ_SIZES)[:-1]]
    qa, kc, vc, ksl, vsl, kwn, vwn, ga, qf, kf, vf, fl = jnp.split(w_in, splits, axis=-1)
    per_group = 3 * GROUP_NSA
    ga_x = jnp.concatenate([_pad_cols(ga[:, k * per_group:(k + 1) * per_group], LANES)
                            for k in range(N_KV_NSA)], axis=-1)
    cols = [qa, kc, vc, _expand_heads(ksl, N_KV_NSA), _expand_heads(kwn, N_KV_NSA), vsl, vwn,
            ga_x, qf, kf, vf, _pad_cols(fl, 2 * LANES)]
    return jnp.concatenate(cols, axis=-1).astype(BF16)


def _rope_tables(seq):
    pos = np.arange(seq, dtype=np.float64)
    inv = ROPE_THETA ** (-np.arange(0, ROT_DIM, 2, dtype=np.float64) / ROT_DIM)
    ang = pos[:, None] * inv[None, :]
    cos, sin = np.cos(ang), np.sin(ang)
    half = ROT_DIM // 2
    rest = HEAD_DIM - ROT_DIM
    ones = np.ones((seq, rest))
    zeros = np.zeros((seq, rest))
    zh = np.zeros((seq, half))
    cos_t = np.concatenate([cos, cos, ones], axis=-1)
    sin_lo = np.concatenate([-sin, zh, zeros], axis=-1)
    sin_hi = np.concatenate([zh, sin, zeros], axis=-1)
    tile2 = lambda t: np.concatenate([t, t], axis=-1)
    return jnp.asarray(np.stack([tile2(cos_t), tile2(sin_lo), tile2(sin_hi)]), F32)


def _layout_cmp_w1(w1):
    hid = w1.shape[1]
    w = w1.reshape(2, CMP_STRIDE, HEAD_DIM, hid)
    z = jnp.zeros_like(w)
    per_head = [jnp.concatenate([w, z], axis=2), jnp.concatenate([z, w], axis=2)]
    return jnp.stack(per_head).reshape(N_KV_NSA, 2, CMP_STRIDE * LANES, hid).astype(BF16)


def _layout_cmp_pos(pos):
    p = pos.reshape(2, CMP_STRIDE, HEAD_DIM)
    p = jnp.concatenate([p, p], axis=-1).reshape(2, 1, CMP_STRIDE * LANES)
    return jnp.broadcast_to(p, (2, SUBLANES, CMP_STRIDE * LANES)).astype(BF16)


def _overlap_t(seq):
    rows = seq // CMP_STRIDE
    n_sel = seq // SEL_BLOCK
    cmp_start = np.arange(rows) * CMP_STRIDE
    sel_start = np.arange(n_sel) * SEL_BLOCK
    ov = ((cmp_start[None, :] <= sel_start[:, None] + SEL_BLOCK - 1)
          & (cmp_start[None, :] + CMP_BLOCK - 1 >= sel_start[:, None]))
    ov[:, (seq - CMP_BLOCK) // CMP_STRIDE + 1:] = False
    return jnp.asarray(ov.astype(np.float32), BF16)


def _row(v, width=None):
    v = v.reshape(1, -1).astype(F32)
    return v if width is None else _pad_cols(v, width)


def kernel(x, p, ffn1_norm, ffn1_wg, ffn1_wu, ffn1_wd, mix_norm, w_in, b_forget, q_norm_nsa, k_norm_cmp, k_norm_slc, k_norm_win, cmp_pos_k, cmp_pos_v, cmp_k_w1, cmp_k_w2, cmp_v_w1, cmp_v_w2, q_norm_fox, k_norm_fox, out_norm_nsa, out_norm_fox, w_out, ffn2_norm, ffn2_wg, ffn2_wu, ffn2_wd, ple_gate_norm, ple_w_gate, ple_w_proj, ple_norm):
    b, s, d = x.shape
    assert s // SEL_BLOCK <= HEAD_DIM and (s // SEL_BLOCK) % SUBLANES == 0
    assert s % max(NSA_TK, FOX_TK, FOX_TQ) == 0
    depth = ffn1_norm.shape[0]
    t = b * s
    rope_tab = _rope_tables(s)
    ov_t = _overlap_t(s)
    tile4 = lambda g: jnp.concatenate([g] * 4).reshape(1, 2 * LANES).astype(F32)
    h = x.reshape(t, d)
    for i in range(depth):
        h = _ffn(h, _row(ffn1_norm[i]), ffn1_wg[i].astype(BF16), ffn1_wu[i].astype(BF16),
                 ffn1_wd[i].astype(BF16))

        head_gains = [tile4(g[i]) for g in (q_norm_nsa, k_norm_slc, k_norm_win, q_norm_fox, k_norm_fox)]
        gains = jnp.concatenate(
            head_gains + [jnp.zeros((SUBLANES - len(head_gains), 2 * LANES), F32)], axis=0)
        (qat, kc, vc, kslx, vslt, kwnx, vwnt, gates_t, fqt, fkx, fvt, cq) = _inproj(
            h.reshape(b, s, d), _row(mix_norm[i]), _layout_w_in(w_in[i]),
            _row(b_forget[i], LANES), gains, rope_tab)

        rows = s // CMP_STRIDE

        kcx, vcx = _compress(
            kc.reshape(b, rows, CMP_STRIDE * LANES), vc.reshape(b, rows, CMP_STRIDE * LANES),
            _layout_cmp_w1(cmp_k_w1[i]), _layout_cmp_w1(cmp_v_w1[i]),
            _layout_cmp_pos(cmp_pos_k[i]), _layout_cmp_pos(cmp_pos_v[i]),
            _pad_cols(cmp_k_w2[i], LANES).astype(BF16), _pad_cols(cmp_v_w2[i], LANES).astype(BF16),
            _row(k_norm_cmp[i], LANES))
        vct = jnp.transpose(vcx[..., :HEAD_DIM], (0, 1, 3, 2))

        o_a = _nsa(qat, kcx, vct, kslx, vslt, kwnx, vwnt, gates_t, ov_t)
        o_b = _fox(fqt, fkx, fvt, cq)

        w_o = w_out[i].astype(BF16)
        h = _tail(h, o_a.reshape(t, NSA_Q_W), o_b.reshape(t, FOX_W), p[i].reshape(t, -1),
                  _row(out_norm_nsa[i]), _row(out_norm_fox[i]), w_o[:NSA_Q_W], w_o[NSA_Q_W:],
                  _row(ffn2_norm[i]), ffn2_wg[i].astype(BF16), ffn2_wu[i].astype(BF16),
                  ffn2_wd[i].astype(BF16), _row(ple_gate_norm[i]), _row(ple_norm[i]),
                  ple_w_gate[i].astype(BF16), ple_w_proj[i].astype(BF16))
    return h.reshape(b, s, d)
```

```python
import functools

import numpy as np
import jax
import jax.numpy as jnp
from jax import lax
from jax.experimental import pallas as pl
from jax.experimental.pallas import tpu as pltpu

F32 = jnp.float32
BF16 = jnp.bfloat16

D_MODEL = 1024
HEAD_DIM = 64
N_HEADS_NSA = 8
N_KV_NSA = 2
GROUP_NSA = N_HEADS_NSA // N_KV_NSA
N_HEADS_FOX = 8
NSA_Q_W = N_HEADS_NSA * HEAD_DIM
NSA_KV_W = N_KV_NSA * HEAD_DIM
FOX_W = N_HEADS_FOX * HEAD_DIM
PROJ_SIZES = (NSA_Q_W, NSA_KV_W, NSA_KV_W, NSA_KV_W, NSA_KV_W, NSA_KV_W, NSA_KV_W,
              3 * N_HEADS_NSA, FOX_W, FOX_W, FOX_W, N_HEADS_FOX)
D_FF = 2816
D_PLE = 256
ROPE_THETA = 500000.0
ROT_DIM = HEAD_DIM // 4
CMP_BLOCK = 32
CMP_STRIDE = 16
CMP_HIDDEN = 256
SEL_BLOCK = 64
SEL_SHIFT = 6
SEL_TOPK = 16
WINDOW = 512
FORCED_SCORE = 1e6
EPS = 1e-6
NEG = -1e30
LOG2E = 1.4426950408889634

LANES = 128
LANE_SHIFT = 7
SUBLANES = 8
SUBLANE_SHIFT = 3
HEAD_SHIFT = 6
BF16_ROWS = 16
VMEM_LIMIT = 56 * 1024 * 1024
AUG_STRIDE = 8

C_QA = 0
C_KCVC = C_QA + NSA_Q_W
C_KSL = C_KCVC + 2 * LANES
C_KWN = C_KSL + N_KV_NSA * LANES
C_VSLWN = C_KWN + N_KV_NSA * LANES
C_GATE = C_VSLWN + 2 * LANES
C_FQ = C_GATE + N_KV_NSA * LANES
C_FK = C_FQ + FOX_W
C_FV = C_FK + FOX_W
C_FL = C_FV + FOX_W
C_END = C_FL + 2 * LANES

FFN_TM = 512
FFN_TF = 256
PROJ_TS = 512
NSA_TQ = 256
NSA_TK = 512
FOX_TQ = 512
FOX_TK = 512


def _dot(a, b):
    return jnp.dot(a, b, preferred_element_type=F32)


def _params(*sem):
    return pltpu.CompilerParams(dimension_semantics=sem, vmem_limit_bytes=VMEM_LIMIT)


def _rms(x):
    return lax.rsqrt(jnp.mean(x * x, axis=-1, keepdims=True) + EPS)


def _split3(x):
    hi = x.astype(BF16)
    r1 = x - hi.astype(F32)
    mid = r1.astype(BF16)
    lo = (r1 - mid.astype(F32)).astype(BF16)
    return hi, mid, lo


def _swiglu_half_step(x, g_ref, wg_ref, wu_ref, wd_ref):
    xn = (x * _rms(x) * g_ref[...]).astype(BF16)
    acc = None
    for j in range(wg_ref.shape[1] // FFN_TF):
        sl = slice(j * FFN_TF, (j + 1) * FFN_TF)
        gate = _dot(xn, wg_ref[:, sl])
        up = _dot(xn, wu_ref[:, sl])
        act = (gate * jax.nn.sigmoid(gate) * up).astype(BF16)
        part = _dot(act, wd_ref[sl, :])
        acc = part if acc is None else acc + part
    return x + 0.5 * acc


def _ffn_kernel(x_ref, g_ref, wg_ref, wu_ref, wd_ref, o_ref):
    o_ref[...] = _swiglu_half_step(x_ref[...], g_ref, wg_ref, wu_ref, wd_ref)


def _resident(a):
    return pl.BlockSpec(a.shape, lambda *_: (0,) * a.ndim, pipeline_mode=pl.Buffered(1))


def _ffn(x, g, wg, wu, wd):
    t, d = x.shape
    tm = min(FFN_TM, t)
    rows = pl.BlockSpec((tm, d), lambda i: (i, 0))
    return pl.pallas_call(
        _ffn_kernel,
        grid=(t // tm,),
        in_specs=[rows, _resident(g), _resident(wg), _resident(wu), _resident(wd)],
        out_specs=rows,
        out_shape=jax.ShapeDtypeStruct((t, d), F32),
        compiler_params=_params("parallel"),
        name="ffn",
    )(x, g, wg, wu, wd)


def _log_sigmoid(x):
    return -(jnp.maximum(-x, 0.0) + jnp.log(1.0 + jnp.exp(-jnp.abs(x))))


def _inproj_kernel(h_ref, g_ref, w_ref, bf_ref, gains_ref, rope_ref,
                   qa_ref, kc_ref, vc_ref, ksl_ref, vsl_ref, kwn_ref, vwn_ref,
                   gate_ref, fq_ref, fk_ref, fv_ref, cum_ref, carry_ref):
    si = pl.program_id(1)
    ts = h_ref.shape[1]
    wide = 2 * LANES
    x = h_ref[0]
    a = (x * _rms(x) * g_ref[...]).astype(BF16)

    row = lax.broadcasted_iota(jnp.int32, (2 * wide, wide), 0) & (wide - 1)
    col = lax.broadcasted_iota(jnp.int32, (2 * wide, wide), 1)
    same = lambda shift: jnp.right_shift(row, shift) == jnp.right_shift(col, shift)
    ones_head = jnp.where(same(HEAD_SHIFT), 1.0, 0.0).astype(BF16)
    ones_chunk = jnp.where(same(LANE_SHIFT), 1.0, 0.0).astype(BF16)
    cos_t, sin_lo, sin_hi = rope_ref[0], rope_ref[1], rope_ref[2]

    def proj(c0):
        return _dot(a, w_ref[:, c0:c0 + wide])

    def norm(u, ones_mat, gain_row):
        u2 = u * u
        hi = u2.astype(BF16)
        lo = (u2 - hi.astype(F32)).astype(BF16)
        ms = _dot(jnp.concatenate([hi, lo], axis=1), ones_mat) * (1.0 / HEAD_DIM)
        return u * lax.rsqrt(ms + EPS) * gains_ref[gain_row:gain_row + 1, :]

    def rope_half(u):
        return (u * cos_t + pltpu.roll(u, LANES - ROT_DIM // 2, 1) * sin_lo
                + pltpu.roll(u, ROT_DIM // 2, 1) * sin_hi)

    def rope(u):
        return jnp.concatenate([rope_half(u[:, :LANES]), rope_half(u[:, LANES:])], axis=1)

    def feature_major(u):
        return jnp.transpose(u.astype(BF16))

    scale = HEAD_DIM ** -0.5 * LOG2E
    pos = si * ts + lax.broadcasted_iota(jnp.int32, (ts, wide), 0)
    lane = lax.broadcasted_iota(jnp.int32, (ts, wide), 1) & (LANES - 1)
    sel_onehot = jnp.where(lane == HEAD_DIM + jnp.right_shift(pos, SEL_SHIFT), 1.0, 0.0)

    @pl.when(si == 0)
    def _():
        carry_ref[...] = jnp.zeros_like(carry_ref)

    aug = []

    def forget_stage(u):
        lf = _log_sigmoid(u[:, :LANES] + bf_ref[...])
        r_i = lax.broadcasted_iota(jnp.int32, (ts, ts), 0)
        c_i = lax.broadcasted_iota(jnp.int32, (ts, ts), 1)
        tri = jnp.where(r_i >= c_i, 1.0, 0.0).astype(BF16)
        hi, mid, lo = _split3(lf)
        c = _dot(tri, hi) + _dot(tri, mid) + _dot(tri, lo) + carry_ref[0:1, :]
        carry_ref[...] = jnp.broadcast_to(c[ts - 1:ts, :], carry_ref.shape)
        c = c * LOG2E
        cum_ref[0] = jnp.transpose(c)[:N_HEADS_FOX]
        n_pairs = N_HEADS_FOX // 2
        p_row = lax.broadcasted_iota(jnp.int32, (3 * LANES, n_pairs * LANES), 0)
        p_col = lax.broadcasted_iota(jnp.int32, (3 * LANES, n_pairs * LANES), 1)
        head = p_row & (LANES - 1)
        part = jnp.right_shift(p_row, LANE_SHIFT)
        target = jnp.right_shift(head, 1) * LANES + (head & 1) * AUG_STRIDE + part
        place = jnp.where((p_col == target) & (head < N_HEADS_FOX), 1.0, 0.0).astype(BF16)
        aug.append(_dot(jnp.concatenate(_split3(-c), axis=1), place).astype(BF16))

    def nsa_q_stage(c, u):
        qa_ref[0, c * wide:(c + 1) * wide, :] = feature_major(rope(norm(u, ones_head, 0)) * scale)

    def kcvc_stage(u):
        kc_ref[0] = rope_half(u[:, :LANES]).astype(BF16)
        vc_ref[0] = u[:, LANES:].astype(BF16)

    def values_stage(u):
        u = feature_major(u)
        vsl_ref[0] = u[:LANES]
        vwn_ref[0] = u[LANES:]

    def ksl_stage(u):
        ksl_ref[0] = (rope(norm(u, ones_chunk, 1)) + sel_onehot).astype(BF16)

    def kwn_stage(u):
        kwn_ref[0] = rope(norm(u, ones_chunk, 2)).astype(BF16)

    def gate_stage(u):
        gate_ref[0] = jnp.transpose(jax.nn.sigmoid(u))

    def fox_q_stage(c, u):
        fq_ref[0, c * wide:(c + 1) * wide, :] = feature_major(norm(u, ones_head, 3) * scale)

    def fox_v_stage(c, u):
        fv_ref[0, c * wide:(c + 1) * wide, :] = feature_major(u)

    def fox_k_stage(c, u):
        fk = norm(u, ones_head, 4).astype(BF16)
        for e in range(2):
            pair = 2 * c + e
            fk_ref[0, :, 2 * pair * LANES:(2 * pair + 1) * LANES] = fk[:, e * LANES:(e + 1) * LANES]
            fk_ref[0, :, (2 * pair + 1) * LANES:(2 * pair + 2) * LANES] = (
                aug[0][:, pair * LANES:(pair + 1) * LANES])

    stages = [(C_FL, forget_stage), (C_KCVC, kcvc_stage), (C_VSLWN, values_stage),
              (C_KSL, ksl_stage), (C_KWN, kwn_stage), (C_GATE, gate_stage)]
    for c in range(NSA_Q_W // wide):
        stages.append((C_QA + c * wide, functools.partial(nsa_q_stage, c)))
    for c in range(FOX_W // wide):
        stages.append((C_FQ + c * wide, functools.partial(fox_q_stage, c)))
        stages.append((C_FV + c * wide, functools.partial(fox_v_stage, c)))
        stages.append((C_FK + c * wide, functools.partial(fox_k_stage, c)))

    u_next = proj(stages[0][0])
    for idx, (_, epilogue) in enumerate(stages):
        u = u_next
        if idx + 1 < len(stages):
            u_next = proj(stages[idx + 1][0])
        epilogue(u)


def _inproj(h, g, w_all, bf_row, gains, rope_tab):
    b, s, d = h.shape
    ts = min(PROJ_TS, s)
    tok = lambda c: pl.BlockSpec((1, ts, c), lambda i, j: (i, j, 0))
    const = lambda shape: pl.BlockSpec(shape, lambda i, j: (0,) * len(shape))
    feat = lambda c: pl.BlockSpec((1, c, ts), lambda i, j: (i, 0, j))
    kvx = N_KV_NSA * LANES
    specs = [(NSA_Q_W, BF16, True), (LANES, BF16, False), (LANES, BF16, False), (kvx, BF16, False),
             (LANES, BF16, True), (kvx, BF16, False), (LANES, BF16, True), (kvx, F32, True),
             (FOX_W, BF16, True), (2 * FOX_W, BF16, False), (FOX_W, BF16, True),
             (N_HEADS_FOX, F32, True)]
    out_shape = [jax.ShapeDtypeStruct((b, c, s) if fm else (b, s, c), dt) for c, dt, fm in specs]
    return pl.pallas_call(
        _inproj_kernel,
        grid=(b, s // ts),
        in_specs=[
            tok(d),
            const((1, d)),
            const((d, C_END)),
            const((1, LANES)),
            const((SUBLANES, 2 * LANES)),
            pl.BlockSpec((3, ts, LANES), lambda i, j: (0, j, 0)),
        ],
        out_specs=[feat(c) if fm else tok(c) for c, _, fm in specs],
        out_shape=out_shape,
        scratch_shapes=[pltpu.VMEM((SUBLANES, LANES), F32)],
        compiler_params=_params("parallel", "arbitrary"),
        name="inproj",
    )(h, g, w_all, bf_row, gains, rope_tab)


def _compress_kernel(rk_ref, rv_ref, w1k_ref, w1v_ref,
                     pk_ref, pv_ref, w2k_ref, w2v_ref, gain_ref, kc_ref, vc_ref):
    def mlp(r_ref, w1_ref, p_ref, w2_ref, kh):
        top, bot = w1_ref[kh, 0], w1_ref[kh, 1]
        rows = r_ref[0]
        bias = _dot(p_ref[0], top) + _dot(p_ref[1], bot)
        second = pltpu.roll(_dot(rows, bot), rows.shape[0] - 1, 0)
        hid = _dot(rows, top) + second + bias[0:1, :]
        hid = hid * jax.nn.sigmoid(hid)
        return _dot(hid.astype(BF16), w2_ref[...])

    for kh in range(N_KV_NSA):
        kc = mlp(rk_ref, w1k_ref, pk_ref, w2k_ref, kh)
        ms = jnp.sum(kc * kc, axis=-1, keepdims=True) * (1.0 / HEAD_DIM)
        kc_ref[0, kh] = (kc * lax.rsqrt(ms + EPS) * gain_ref[...]).astype(BF16)
        vc_ref[0, kh] = mlp(rv_ref, w1v_ref, pv_ref, w2v_ref, kh).astype(BF16)


def _compress(rk, rv, w1k, w1v, pk, pv, w2k, w2v, gain):
    b, r, w = rk.shape
    rows = pl.BlockSpec((1, r, w), lambda i: (i, 0, 0))
    const = lambda a: pl.BlockSpec(a.shape, lambda i: (0,) * a.ndim)
    out = jax.ShapeDtypeStruct((b, N_KV_NSA, r, LANES), BF16)
    out_spec = pl.BlockSpec((1, N_KV_NSA, r, LANES), lambda i: (i, 0, 0, 0))
    return pl.pallas_call(
        _compress_kernel,
        grid=(b,),
        in_specs=[rows, rows, const(w1k), const(w1v), const(pk), const(pv),
                  const(w2k), const(w2v), const(gain)],
        out_specs=[out_spec, out_spec],
        out_shape=[out, out],
        compiler_params=_params("parallel"),
        name="compress",
    )(rk, rv, w1k, w1v, pk, pv, w2k, w2v, gain)


def _ones_rows(width):
    return jnp.ones((BF16_ROWS, width), BF16)


def _pipelined_sweep(n_full, n_tail, tk, scores, masked, consume, s_ref, mt_ref):
    def stash(s_list):
        for e, s in enumerate(s_list):
            s_ref[e] = s
            mt_ref[e] = jnp.max(s, axis=0, keepdims=True)

    def body(j, carry):
        k0 = pl.multiple_of(j * tk, tk)
        s_new = scores(k0)
        consume(pl.multiple_of(k0 - tk, tk))
        stash(s_new)
        return carry

    def tail_step(k0):
        s_new = scores(k0)
        consume(pl.multiple_of(k0 - tk, tk))
        stash(masked(s_new, k0))

    k_tail = pl.multiple_of(n_full * tk, tk)

    @pl.when(n_full == 0)
    def _():
        stash(masked(scores(k_tail), k_tail))

    @pl.when(n_full > 0)
    def _():
        stash(scores(0))
        lax.fori_loop(1, n_full, body, 0)
        tail_step(k_tail)

    for i in range(1, n_tail):
        tail_step(pl.multiple_of(k_tail + i * tk, tk))
    consume(pl.multiple_of(k_tail + (n_tail - 1) * tk, tk))


def _nsa_kernel(qt_ref, kc_ref, vct_ref, ksl_ref, vslt_ref, kwn_ref, vwnt_ref, gt_ref, ov_ref,
                o_ref, qaug_ref, m_ref, acc_ref, s_ref, mt_ref, part_ref):
    qi = pl.program_id(2)
    tq = qt_ref.shape[2]
    s_len = ksl_ref.shape[1]
    n_cmp_rows = kc_ref.shape[2]
    n_sel = ov_ref.shape[0]
    top = min(SEL_TOPK, n_sel)
    grp = GROUP_NSA
    cols = grp * tq
    tk = min(NSA_TK, s_len)
    t0 = qi * tq

    def tok_of_tile(rows):
        return t0 + lax.broadcasted_iota(jnp.int32, (rows, tq), 1)

    def softmax_cols(s, keep):
        s = s + jnp.concatenate([jnp.where(keep, 0.0, NEG)] * grp, axis=1)
        mx = jnp.max(s, axis=0, keepdims=True)
        e = jnp.exp2(s - mx)
        den = jnp.sum(e, axis=0, keepdims=True)
        return e, jnp.where(mx > 0.5 * NEG, 1.0 / den, 0.0)

    qt = qt_ref[0]
    qaug_ref[0:HEAD_DIM, :] = jnp.concatenate(
        [qt[g * HEAD_DIM:(g + 1) * HEAD_DIM] for g in range(grp)], axis=1)
    qaug_ref[HEAD_DIM:, :] = jnp.zeros((HEAD_DIM, cols), BF16)

    span = min(WINDOW + tq, s_len)
    start = pl.multiple_of(jnp.maximum(t0 + tq - span, 0), LANES)
    s_c = _dot(kc_ref[0, 0], qaug_ref[...])
    s_w = _dot(kwn_ref[0, pl.ds(start, span), :], qaug_ref[...])

    n_c = lax.broadcasted_iota(jnp.int32, (n_cmp_rows, tq), 0)
    keep_c = n_c * CMP_STRIDE + (CMP_BLOCK - 1) <= tok_of_tile(n_cmp_rows)
    e_c, inv_c = softmax_cols(s_c, keep_c)
    o_c = _dot(vct_ref[0, 0], e_c.astype(BF16)) * inv_c

    t_w = tok_of_tile(span)
    k_w = start + lax.broadcasted_iota(jnp.int32, (span, tq), 0)
    keep_w = (k_w <= t_w) & (t_w - k_w < WINDOW)
    e_w, inv_w = softmax_cols(s_w, keep_w)
    o_w = _dot(vwnt_ref[0, :, pl.ds(start, span)], e_w.astype(BF16)) * inv_w

    gt = gt_ref[0]
    for g in range(grp):
        sl = slice(g * tq, (g + 1) * tq)
        part_ref[g * HEAD_DIM:(g + 1) * HEAD_DIM, :] = (
            gt[3 * g:3 * g + 1] * o_c[:, sl] + gt[3 * g + 2:3 * g + 3] * o_w[:, sl])

    p_sum = e_c[:, 0:tq] * inv_c[:, 0:tq]
    for g in range(1, grp):
        p_sum = p_sum + e_c[:, g * tq:(g + 1) * tq] * inv_c[:, g * tq:(g + 1) * tq]
    p_hi = p_sum.astype(BF16)
    p_lo = (p_sum - p_hi.astype(F32)).astype(BF16)
    ov = ov_ref[...]
    imp = _dot(ov, p_hi) + _dot(ov, p_lo)
    j_blk = lax.broadcasted_iota(jnp.int32, (n_sel, tq), 0)
    cur = jnp.right_shift(t0 + lax.broadcasted_iota(jnp.int32, (n_sel, tq), 1), SEL_SHIFT)
    forced = (j_blk == 0) | (j_blk == cur) | (j_blk == cur - 1)
    imp = jnp.where(j_blk <= cur, jnp.where(forced, FORCED_SCORE, imp), -1.0)
    rows_g = SUBLANES
    sub = lax.broadcasted_iota(jnp.int32, (rows_g, tq), 0)
    n_grp = n_sel // rows_g

    def write_selection_bias(n_active):
        groups = [imp[rows_g * v:rows_g * (v + 1)] for v in range(n_active)]
        ranks = [jnp.zeros((rows_g, tq), F32) for _ in groups]
        for i in range(rows_g * n_active):
            r_i = jnp.broadcast_to(imp[i:i + 1, :], (rows_g, tq))
            for v, g_v in enumerate(groups):
                if i < rows_g * v:
                    beats = r_i >= g_v
                elif i >= rows_g * (v + 1):
                    beats = r_i > g_v
                else:
                    beats = (r_i > g_v) | ((r_i == g_v) & (sub > i - rows_g * v))
                ranks[v] = ranks[v] + jnp.where(beats, 1.0, 0.0)
        rows = [jnp.where((r_v < top) & (g_v >= 0.0), 0.0, NEG) for r_v, g_v in zip(ranks, groups)]
        rows += [jnp.full((rows_g, tq), NEG, F32)] * (n_grp - n_active)
        if n_sel < HEAD_DIM:
            rows.append(jnp.zeros((HEAD_DIM - n_sel, tq), F32))
        bias_t = jnp.concatenate(rows, axis=0).astype(BF16)
        qaug_ref[HEAD_DIM:, :] = jnp.concatenate([bias_t] * grp, axis=1)

    groups_needed = jnp.right_shift(t0 + tq - 1, SEL_SHIFT + SUBLANE_SHIFT) + 1
    for n_active in range(1, n_grp + 1):
        pl.when(groups_needed == n_active)(functools.partial(write_selection_bias, n_active))

    m_ref[...] = jnp.full_like(m_ref, NEG)
    acc_ref[...] = jnp.zeros_like(acc_ref)

    def slc_scores(k0):
        return [_dot(ksl_ref[0, pl.ds(k0, tk), :], qaug_ref[...])]

    def slc_masked(s_list, k0):
        key = k0 + lax.broadcasted_iota(jnp.int32, (tk, tq), 0)
        bias = jnp.where(key <= tok_of_tile(tk), 0.0, NEG)
        return [s_list[0] + jnp.concatenate([bias] * grp, axis=1)]

    def slc_consume(k0):
        m_old = m_ref[...]
        m_new = jnp.maximum(m_old, mt_ref[0])
        p = jnp.exp2(s_ref[0] - m_new).astype(BF16)
        vt = jnp.concatenate([vslt_ref[0, :, pl.ds(k0, tk)], _ones_rows(tk)], axis=0)
        acc_ref[...] = jnp.exp2(m_old - m_new) * acc_ref[...] + _dot(vt, p)
        m_ref[...] = m_new

    _pipelined_sweep(t0 // tk, max(1, tq // tk), tk, slc_scores, slc_masked, slc_consume, s_ref, mt_ref)
    o_s = acc_ref[0:HEAD_DIM, :] * (1.0 / acc_ref[HEAD_DIM:HEAD_DIM + 1, :])

    outs = [part_ref[g * HEAD_DIM:(g + 1) * HEAD_DIM, :]
            + gt[3 * g + 1:3 * g + 2] * o_s[:, g * tq:(g + 1) * tq] for g in range(grp)]
    o_ref[0] = jnp.transpose(jnp.concatenate(outs, axis=0))


def _nsa(qat, kcx, vct, kslx, vslt, kwnx, vwnt, gates_t, ov_t):
    b, _, s = qat.shape
    tq = min(NSA_TQ, s)
    assert tq & (tq - 1) == 0 and tq % LANES == 0
    r = kcx.shape[2]
    gw = GROUP_NSA * HEAD_DIM
    cols = GROUP_NSA * tq
    gate_rows = -(-3 * GROUP_NSA // SUBLANES) * SUBLANES
    key_major = pl.BlockSpec((1, s, LANES), lambda i, k, j: (i, 0, k))
    val_major = pl.BlockSpec((1, HEAD_DIM, s), lambda i, k, j: (i, k, 0))
    return pl.pallas_call(
        _nsa_kernel,
        grid=(b, N_KV_NSA, s // tq),
        in_specs=[
            pl.BlockSpec((1, gw, tq), lambda i, k, j: (i, k, j)),
            pl.BlockSpec((1, 1, r, LANES), lambda i, k, j: (i, k, 0, 0)),
            pl.BlockSpec((1, 1, HEAD_DIM, r), lambda i, k, j: (i, k, 0, 0)),
            key_major, val_major, key_major, val_major,
            pl.BlockSpec((1, gate_rows, tq), lambda i, k, j: (i, k * (LANES // gate_rows), j)),
            pl.BlockSpec(ov_t.shape, lambda i, k, j: (0, 0)),
        ],
        out_specs=pl.BlockSpec((1, tq, gw), lambda i, k, j: (i, j, k)),
        out_shape=jax.ShapeDtypeStruct((b, s, NSA_Q_W), F32),
        scratch_shapes=[
            pltpu.VMEM((LANES, cols), BF16),
            pltpu.VMEM((1, cols), F32),
            pltpu.VMEM((HEAD_DIM + BF16_ROWS, cols), F32),
            pltpu.VMEM((1, min(NSA_TK, s), cols), F32),
            pltpu.VMEM((1, 1, cols), F32),
            pltpu.VMEM((gw, tq), F32),
        ],
        compiler_params=_params("parallel", "parallel", "arbitrary"),
        name="nsa",
    )(qat, kcx, vct, kslx, vslt, kwnx, vwnt, gates_t, ov_t)


def _fox_kernel(qt_ref, k_ref, vt_ref, cq_ref, o_ref, qaug_ref, m_ref, acc_ref, s_ref, mt_ref):
    pair = pl.program_id(1)
    qi = pl.program_id(2)
    tq = qt_ref.shape[2]
    tk = min(FOX_TK, k_ref.shape[1])
    t0 = qi * tq
    qt = qt_ref[0]
    row = lax.broadcasted_iota(jnp.int32, (LANES, tq), 0)
    zero = jnp.zeros_like(qt)
    cq = []
    for e in range(2):
        qaug_ref[e, 0:LANES, :] = jnp.where((row >= e * HEAD_DIM) & (row < (e + 1) * HEAD_DIM), qt, zero)
        pick = (row >= e * AUG_STRIDE) & (row < e * AUG_STRIDE + 3)
        qaug_ref[e, LANES:, :] = jnp.where(pick, 1.0, 0.0).astype(BF16)
        cq.append(cq_ref[0, pl.ds(2 * pair + e, 1), :])

    m_ref[...] = jnp.full_like(m_ref, NEG)
    acc_ref[...] = jnp.zeros_like(acc_ref)

    def scores(k0):
        kt = k_ref[0, pl.ds(k0, tk), :]
        return [_dot(kt, qaug_ref[e]) for e in range(2)]

    def masked(s_list, k0):
        key = k0 + lax.broadcasted_iota(jnp.int32, (tk, tq), 0)
        tok = t0 + lax.broadcasted_iota(jnp.int32, (tk, tq), 1)
        return [jnp.where(key <= tok, sv, NEG) for sv in s_list]

    def consume(k0):
        for e in range(2):
            m_old = m_ref[e]
            m_new = jnp.maximum(m_old, mt_ref[e] + cq[e])
            p = jnp.exp2(s_ref[e] + (cq[e] - m_new)).astype(BF16)
            vt = jnp.concatenate([vt_ref[0, e * HEAD_DIM:(e + 1) * HEAD_DIM, pl.ds(k0, tk)],
                                  _ones_rows(tk)], axis=0)
            acc_ref[e] = jnp.exp2(m_old - m_new) * acc_ref[e] + _dot(vt, p)
            m_ref[e] = m_new

    _pipelined_sweep(t0 // tk, max(1, tq // tk), tk, scores, masked, consume, s_ref, mt_ref)
    o_ref[0] = jnp.transpose(jnp.concatenate(
        [acc_ref[e, 0:HEAD_DIM, :] * (1.0 / acc_ref[e, HEAD_DIM:HEAD_DIM + 1, :]) for e in range(2)], axis=0))


def _fox(fqt, fkx, fvt, cq):
    b, _, s = fqt.shape
    tq = min(FOX_TQ, s)
    return pl.pallas_call(
        _fox_kernel,
        grid=(b, N_HEADS_FOX // 2, s // tq),
        in_specs=[
            pl.BlockSpec((1, LANES, tq), lambda i, p, j: (i, p, j)),
            pl.BlockSpec((1, s, 2 * LANES), lambda i, p, j: (i, 0, p)),
            pl.BlockSpec((1, LANES, s), lambda i, p, j: (i, p, 0)),
            pl.BlockSpec((1, N_HEADS_FOX, tq), lambda i, p, j: (i, 0, j)),
        ],
        out_specs=pl.BlockSpec((1, tq, LANES), lambda i, p, j: (i, j, p)),
        out_shape=jax.ShapeDtypeStruct((b, s, FOX_W), F32),
        scratch_shapes=[
            pltpu.VMEM((2, 2 * LANES, tq), BF16),
            pltpu.VMEM((2, 1, tq), F32),
            pltpu.VMEM((2, HEAD_DIM + BF16_ROWS, tq), F32),
            pltpu.VMEM((2, min(FOX_TK, s), tq), F32),
            pltpu.VMEM((2, 1, tq), F32),
        ],
        compiler_params=_params("parallel", "parallel", "arbitrary"),
        name="fox",
    )(fqt, fkx, fvt, cq)


def _tail_kernel(h_ref, oa_ref, ob_ref, p_ref, ga_ref, gb_ref, wa_ref, wb_ref,
                 g_ref, wg_ref, wu_ref, wd_ref, gg_ref, gp_ref, wgate_ref, wproj_ref, o_ref):
    oa = oa_ref[...]
    ob = ob_ref[...]
    na = (oa * _rms(oa) * ga_ref[...]).astype(BF16)
    nb = (ob * _rms(ob) * gb_ref[...]).astype(BF16)
    h = h_ref[...] + _dot(na, wa_ref[...]) + _dot(nb, wb_ref[...])
    h = _swiglu_half_step(h, g_ref, wg_ref, wu_ref, wd_ref)
    hn = (h * _rms(h) * gg_ref[...]).astype(BF16)
    gate = jax.nn.sigmoid(_dot(hn, wgate_ref[...]))
    e = _dot(p_ref[...].astype(BF16), wproj_ref[...])
    o_ref[...] = h + gate * (e * _rms(e) * gp_ref[...])


def _tail(h, oa, ob, p, ga, gb, wa, wb, g, wg, wu, wd, gg, gp, wgate, wproj):
    t, d = h.shape
    tm = min(FFN_TM, t)
    rows = lambda c: pl.BlockSpec((tm, c), lambda i: (i, 0))
    consts = (ga, gb, wa, wb, g, wg, wu, wd, gg, gp, wgate, wproj)
    return pl.pallas_call(
        _tail_kernel,
        grid=(t // tm,),
        in_specs=[rows(d), rows(oa.shape[1]), rows(ob.shape[1]), rows(p.shape[1])]
        + [_resident(c) for c in consts],
        out_specs=rows(d),
        out_shape=jax.ShapeDtypeStruct((t, d), F32),
        compiler_params=_params("parallel"),
        name="tail",
    )(h, oa, ob, p, *consts)


def _expand_heads(w, n):
    w = w.reshape(w.shape[0], n, HEAD_DIM)
    return jnp.concatenate([w, jnp.zeros_like(w)], axis=-1).reshape(w.shape[0], n * LANES)


def _pad_cols(w, width):
    return jnp.pad(w, ((0, 0), (0, width - w.shape[1])))


def _layout_w_in(w_in):
    splits = [int(v) for v in np.cumsum(PROJ_SIZES)[:-1]]
    qa, kc, vc, ksl, vsl, kwn, vwn, ga, qf, kf, vf, fl = jnp.split(w_in, splits, axis=-1)
    per_group = 3 * GROUP_NSA
    ga_x = jnp.concatenate([_pad_cols(ga[:, k * per_group:(k + 1) * per_group], LANES)
                            for k in range(N_KV_NSA)], axis=-1)
    cols = [qa, kc, vc, _expand_heads(ksl, N_KV_NSA), _expand_heads(kwn, N_KV_NSA), vsl, vwn,
            ga_x, qf, kf, vf, _pad_cols(fl, 2 * LANES)]
    return jnp.concatenate(cols, axis=-1).astype(BF16)


def _rope_tables(seq):
    pos = np.arange(seq, dtype=np.float64)
    inv = ROPE_THETA ** (-np.arange(0, ROT_DIM, 2, dtype=np.float64) / ROT_DIM)
    ang = pos[:, None] * inv[None, :]
    cos, sin = np.cos(ang), np.sin(ang)
    half = ROT_DIM // 2
    rest = HEAD_DIM - ROT_DIM
    ones = np.ones((seq, rest))
    zeros = np.zeros((seq, rest))
    zh = np.zeros((seq, half))
    cos_t = np.concatenate([cos, cos, ones], axis=-1)
    sin_lo = np.concatenate([-sin, zh, zeros], axis=-1)
    sin_hi = np.concatenate([zh, sin, zeros], axis=-1)
    tile2 = lambda t: np.concatenate([t, t], axis=-1)
    return jnp.asarray(np.stack([tile2(cos_t), tile2(sin_lo), tile2(sin_hi)]), F32)


def _layout_cmp_w1(w1):
    hid = w1.shape[1]
    w = w1.reshape(2, CMP_STRIDE, HEAD_DIM, hid)
    z = jnp.zeros_like(w)
    per_head = [jnp.concatenate([w, z], axis=2), jnp.concatenate([z, w], axis=2)]
    return jnp.stack(per_head).reshape(N_KV_NSA, 2, CMP_STRIDE * LANES, hid).astype(BF16)


def _layout_cmp_pos(pos):
    p = pos.reshape(2, CMP_STRIDE, HEAD_DIM)
    p = jnp.concatenate([p, p], axis=-1).reshape(2, 1, CMP_STRIDE * LANES)
    return jnp.broadcast_to(p, (2, SUBLANES, CMP_STRIDE * LANES)).astype(BF16)


def _overlap_t(seq):
    rows = seq // CMP_STRIDE
    n_sel = seq // SEL_BLOCK
    cmp_start = np.arange(rows) * CMP_STRIDE
    sel_start = np.arange(n_sel) * SEL_BLOCK
    ov = ((cmp_start[None, :] <= sel_start[:, None] + SEL_BLOCK - 1)
          & (cmp_start[None, :] + CMP_BLOCK - 1 >= sel_start[:, None]))
    ov[:, (seq - CMP_BLOCK) // CMP_STRIDE + 1:] = False
    return jnp.asarray(ov.astype(np.float32), BF16)


def _row(v, width=None):
    v = v.reshape(1, -1).astype(F32)
    return v if width is None else _pad_cols(v, width)


def kernel(x, p, ffn1_norm, ffn1_wg, ffn1_wu, ffn1_wd, mix_norm, w_in, b_forget, q_norm_nsa, k_norm_cmp, k_norm_slc, k_norm_win, cmp_pos_k, cmp_pos_v, cmp_k_w1, cmp_k_w2, cmp_v_w1, cmp_v_w2, q_norm_fox, k_norm_fox, out_norm_nsa, out_norm_fox, w_out, ffn2_norm, ffn2_wg, ffn2_wu, ffn2_wd, ple_gate_norm, ple_w_gate, ple_w_proj, ple_norm):
    b, s, d = x.shape
    assert s // SEL_BLOCK <= HEAD_DIM and (s // SEL_BLOCK) % SUBLANES == 0
    assert s % max(NSA_TK, FOX_TK, FOX_TQ) == 0
    depth = ffn1_norm.shape[0]
    t = b * s
    rope_tab = _rope_tables(s)
    ov_t = _overlap_t(s)
    tile4 = lambda g: jnp.concatenate([g] * 4).reshape(1, 2 * LANES).astype(F32)
    h = x.reshape(t, d)
    for i in range(depth):
        h = _ffn(h, _row(ffn1_norm[i]), ffn1_wg[i].astype(BF16), ffn1_wu[i].astype(BF16),
                 ffn1_wd[i].astype(BF16))

        head_gains = [tile4(g[i]) for g in (q_norm_nsa, k_norm_slc, k_norm_win, q_norm_fox, k_norm_fox)]
        gains = jnp.concatenate(
            head_gains + [jnp.zeros((SUBLANES - len(head_gains), 2 * LANES), F32)], axis=0)
        (qat, kc, vc, kslx, vslt, kwnx, vwnt, gates_t, fqt, fkx, fvt, cq) = _inproj(
            h.reshape(b, s, d), _row(mix_norm[i]), _layout_w_in(w_in[i]),
            _row(b_forget[i], LANES), gains, rope_tab)

        rows = s // CMP_STRIDE

        kcx, vcx = _compress(
            kc.reshape(b, rows, CMP_STRIDE * LANES), vc.reshape(b, rows, CMP_STRIDE * LANES),
            _layout_cmp_w1(cmp_k_w1[i]), _layout_cmp_w1(cmp_v_w1[i]),
            _layout_cmp_pos(cmp_pos_k[i]), _layout_cmp_pos(cmp_pos_v[i]),
            _pad_cols(cmp_k_w2[i], LANES).astype(BF16), _pad_cols(cmp_v_w2[i], LANES).astype(BF16),
            _row(k_norm_cmp[i], LANES))
        vct = jnp.transpose(vcx[..., :HEAD_DIM], (0, 1, 3, 2))

        o_a = _nsa(qat, kcx, vct, kslx, vslt, kwnx, vwnt, gates_t, ov_t)
        o_b = _fox(fqt, fkx, fvt, cq)

        w_o = w_out[i].astype(BF16)
        h = _tail(h, o_a.reshape(t, NSA_Q_W), o_b.reshape(t, FOX_W), p[i].reshape(t, -1),
                  _row(out_norm_nsa[i]), _row(out_norm_fox[i]), w_o[:NSA_Q_W], w_o[NSA_Q_W:],
                  _row(ffn2_norm[i]), ffn2_wg[i].astype(BF16), ffn2_wu[i].astype(BF16),
                  ffn2_wd[i].astype(BF16), _row(ple_gate_norm[i]), _row(ple_norm[i]),
                  ple_w_gate[i].astype(BF16), ple_w_proj[i].astype(BF16))
    return h.reshape(b, s, d)
```
